```python
import math
import jax, jax.numpy as jnp
from jax import lax
import numpy as np

D_MODEL = 1024
BATCH = 2
SEQ = 16384
DEPTH = 2

GRID_W = 64
CTX_LEN = 256
EPS = 1e-6

HEAD_DIM = 64
ATTN_Q_HEADS = 8
ATTN_KV_HEADS = 2
ATTN_GROUP = ATTN_Q_HEADS // ATTN_KV_HEADS
Q_BLOCK = 128
ROPE_THETA = 10000.0

DN_HEADS = 8
DN_DK = 64
DN_DV = 64
DN_CONV = 5
DN_CHUNK = 64

A_Q = ATTN_Q_HEADS * HEAD_DIM
A_KV = ATTN_KV_HEADS * HEAD_DIM
B_QKV = 2 * DN_HEADS * DN_DK + DN_HEADS * DN_DV
B_GATES = 4 * DN_HEADS
B_Z = DN_HEADS * DN_DV
IN_COLS = A_Q + 2 * A_KV + B_QKV + B_GATES + B_Z
MIX_WIDTH = A_Q + B_Z

S5_GROUP = 16
S5_GROUPS = D_MODEL // S5_GROUP
S5_STATE = 64
S5_STEP_MIN = 1e-3
S5_STEP_MAX = 1e-1

N_EXPERTS = 32
TOP_K = 4
D_EXPERT = D_MODEL
SWIGLU_LIMIT = 7.0
SWIGLU_ALPHA = 1.702
MOE_BLOCK = 256

N_EVEN = (DEPTH + 1) // 2
N_ODD = DEPTH // 2

kernel_name = 'hybrid_attn_deltanet_s5_moe_dit'


def rms_norm(x, gain=None):
    xf = x.astype(jnp.float32)
    y = xf * lax.rsqrt(jnp.mean(xf * xf, axis=-1, keepdims=True) + EPS)
    if gain is not None:
        y = y * gain.astype(jnp.float32)
    return y.astype(x.dtype)


def l2_normalize(t):
    return t * lax.rsqrt(jnp.sum(t * t, axis=-1, keepdims=True) + EPS)


def modulate(h, shift, scale):
    return h * (1 + scale) + shift


def _flip_seq(t, direction):
    return jnp.flip(t, axis=1) if direction == 1 else t


def axial_rope(length):
    rows = length // GRID_W
    row = jnp.broadcast_to(jnp.arange(rows, dtype=jnp.int32)[:, None], (rows, GRID_W)).reshape(-1)
    col = jnp.broadcast_to(jnp.arange(GRID_W, dtype=jnp.int32)[None, :], (rows, GRID_W)).reshape(-1)
    n_axis = HEAD_DIM // 4
    inv_freq = ROPE_THETA ** (-jnp.arange(n_axis, dtype=jnp.float32) / n_axis)
    ang = jnp.concatenate([row.astype(jnp.float32)[:, None] * inv_freq,
                           col.astype(jnp.float32)[:, None] * inv_freq], axis=-1)
    return jnp.cos(ang), jnp.sin(ang)


def apply_rope(t, cos, sin):
    half = HEAD_DIM // 2
    t1 = t[..., :half].astype(jnp.float32)
    t2 = t[..., half:].astype(jnp.float32)
    c = cos[None, :, None, :]
    s = sin[None, :, None, :]
    return jnp.concatenate([t1 * c - t2 * s, t1 * s + t2 * c], axis=-1).astype(t.dtype)


def gqa_attention(q_ctx, k_ctx, v_ctx, q_lat, k_lat, v_lat, q_gain, k_gain, with_ctx):
    b, n_lat = q_lat.shape[:2]
    n_ctx = q_ctx.shape[1]

    def heads(t, n):
        return t.reshape(t.shape[0], t.shape[1], n, HEAD_DIM)

    kc = rms_norm(heads(k_ctx, ATTN_KV_HEADS), k_gain)
    vc = heads(v_ctx, ATTN_KV_HEADS)
    cos, sin = axial_rope(n_lat)
    ql = apply_rope(rms_norm(heads(q_lat, ATTN_Q_HEADS), q_gain), cos, sin)
    kl = apply_rope(rms_norm(heads(k_lat, ATTN_KV_HEADS), k_gain), cos, sin)
    vl = heads(v_lat, ATTN_KV_HEADS)
    k_all = jnp.concatenate([kc, kl], axis=1)
    v_all = jnp.concatenate([vc, vl], axis=1)
    scale = HEAD_DIM ** -0.5

    def attend(qb, keys, vals):
        s = jnp.einsum('bqkgd,bskd->bkgqs', qb, keys, preferred_element_type=jnp.float32) * scale
        p = jax.nn.softmax(s, axis=-1).astype(vals.dtype)
        return jnp.einsum('bkgqs,bskd->bqkgd', p, vals)

    n_blk = n_lat // Q_BLOCK
    q_blocks = ql.reshape(b, n_blk, Q_BLOCK, ATTN_KV_HEADS, ATTN_GROUP, HEAD_DIM).transpose(1, 0, 2, 3, 4, 5)
    o_lat = lax.map(lambda qb: attend(qb, k_all, v_all), q_blocks)
    o_lat = o_lat.transpose(1, 0, 2, 3, 4, 5).reshape(b, n_lat, A_Q)
    o_ctx = None
    if with_ctx:
        qc = rms_norm(heads(q_ctx, ATTN_Q_HEADS), q_gain).reshape(b, n_ctx, ATTN_KV_HEADS, ATTN_GROUP, HEAD_DIM)
        o_ctx = attend(qc, kc, vc).reshape(b, n_ctx, A_Q)
    return o_ctx, o_lat


def short_conv(u, w):
    pad = DN_CONV // 2
    y = lax.conv_general_dilated(u, w[:, None, :].astype(u.dtype), window_strides=(1,), padding=[(pad, pad)],
                                 dimension_numbers=('NWC', 'WIO', 'NWC'), feature_group_count=u.shape[-1])
    return jax.nn.silu(y)


def delta_rule_chunked(q, k, v, g, beta, state, with_out):
    b, n_tok, h = q.shape[:3]
    n = n_tok // DN_CHUNK

    def chunks(t):
        return t.reshape(b, n, DN_CHUNK, h, -1).transpose(1, 0, 3, 2, 4)

    kc, vc = chunks(k), chunks(v)
    gcum = jnp.cumsum(chunks(g[..., None])[..., 0], axis=-1)
    bc = chunks(beta[..., None])
    idx = jnp.arange(DN_CHUNK)
    lower = idx[:, None] >= idx[None, :]
    strict = idx[:, None] > idx[None, :]
    diff = gcum[..., :, None] - gcum[..., None, :]
    decay = jnp.where(lower, jnp.exp(jnp.where(lower, diff, 0.0)), 0.0)
    kbk = jnp.einsum('nbhid,nbhjd->nbhij', bc * kc, kc) * decay
    tri = jnp.where(strict, kbk, 0.0) + jnp.eye(DN_CHUNK, dtype=jnp.float32)
    rhs = jnp.concatenate([bc * vc, bc * kc * jnp.exp(gcum)[..., None]], axis=-1)
    sol = lax.linalg.triangular_solve(tri, rhs, left_side=True, lower=True, unit_diagonal=True)
    u_c, w_c = sol[..., :DN_DV], sol[..., DN_DV:]
    xs = (kc, u_c, w_c, gcum) + ((chunks(q), decay) if with_out else ())

    def step(s, xs_i):
        k_i, u_i, w_i, g_i = xs_i[:4]
        v_new = u_i - jnp.einsum('bhck,bhkv->bhcv', w_i, s)
        g_last = g_i[..., -1:]
        k_dec = k_i * jnp.exp(g_last - g_i)[..., None]
        s_next = s * jnp.exp(g_last)[..., None] + jnp.einsum('bhck,bhcv->bhkv', k_dec, v_new)
        if not with_out:
            return s_next, None
        q_i, dec_i = xs_i[4:]
        qk = jnp.einsum('bhik,bhjk->bhij', q_i, k_i) * dec_i
        o = (jnp.einsum('bhck,bhkv->bhcv', q_i * jnp.exp(g_i)[..., None], s)
             + jnp.einsum('bhij,bhjv->bhiv', qk, v_new))
        return s_next, o

    state, o = lax.scan(step, state, xs)
    if not with_out:
        return None, state
    return o.transpose(1, 0, 3, 2, 4).reshape(b, n_tok, h, DN_DV), state


def dn_prepare(qkv, ab, conv_w):
    b, n_tok = qkv.shape[:2]
    qkv = short_conv(qkv, conv_w).astype(jnp.float32)
    q, k, v = jnp.split(qkv, [DN_HEADS * DN_DK, 2 * DN_HEADS * DN_DK], axis=-1)
    q = l2_normalize(q.reshape(b, n_tok, DN_HEADS, DN_DK)) * DN_DK ** -0.5
    k = l2_normalize(k.reshape(b, n_tok, DN_HEADS, DN_DK))
    v = v.reshape(b, n_tok, DN_HEADS, DN_DV)
    ab = ab.astype(jnp.float32).reshape(b, n_tok, 4, DN_HEADS)
    return q, k, v, ab


def dn_direction(inputs, a_log, dt_bias, d, state, with_out):
    q, k, v, ab = inputs
    g = -jnp.exp(a_log[d].astype(jnp.float32)) * jax.nn.softplus(ab[:, :, d] + dt_bias[d].astype(jnp.float32))
    beta = jax.nn.sigmoid(ab[:, :, 2 + d])
    o, state = delta_rule_chunked(_flip_seq(q, d), _flip_seq(k, d), _flip_seq(v, d),
                                  _flip_seq(g, d), _flip_seq(beta, d), state, with_out)
    return (_flip_seq(o, d) if with_out else None), state


def gated_deltanet(qkv_c, ab_c, z_c, qkv_l, ab_l, z_l, conv_w, a_log, dt_bias, norm_w, with_ctx):
    ctx_in = dn_prepare(qkv_c, ab_c, conv_w)
    lat_in = dn_prepare(qkv_l, ab_l, conv_w)
    b = qkv_l.shape[0]
    o_ctx, o_lat = None, None
    for d in range(2):
        s0 = jnp.zeros((b, DN_HEADS, DN_DK, DN_DV), jnp.float32)
        oc, s_ctx = dn_direction(ctx_in, a_log, dt_bias, d, s0, with_ctx)
        ol, _ = dn_direction(lat_in, a_log, dt_bias, d, s_ctx, True)
        o_lat = ol if o_lat is None else o_lat + ol
        if with_ctx:
            o_ctx = oc if o_ctx is None else o_ctx + oc

    def finish(o, z):
        zf = z.astype(jnp.float32).reshape(o.shape)
        return (rms_norm(o, norm_w) * jax.nn.silu(zf)).reshape(o.shape[0], o.shape[1], B_Z)

    return (finish(o_ctx, z_c) if with_ctx else None), finish(o_lat, z_l)


def mixer_attn_deltanet(h_ctx, h_lat, w_in, q_gain, k_gain, conv_w, a_log, dt_bias, norm_w, w_out, with_ctx):
    offs = [A_Q, A_Q + A_KV, A_Q + 2 * A_KV, A_Q + 2 * A_KV + B_QKV, A_Q + 2 * A_KV + B_QKV + B_GATES]
    pc = jnp.split(h_ctx @ w_in, offs, axis=-1)
    pl = jnp.split(h_lat @ w_in, offs, axis=-1)
    a_ctx, a_lat = gqa_attention(pc[0], pc[1], pc[2], pl[0], pl[1], pl[2], q_gain, k_gain, with_ctx)
    b_ctx, b_lat = gated_deltanet(pc[3], pc[4], pc[5], pl[3], pl[4], pl[5], conv_w, a_log, dt_bias, norm_w, with_ctx)
    dt = h_lat.dtype
    out_lat = jnp.concatenate([a_lat.astype(dt), b_lat.astype(dt)], axis=-1) @ w_out
    out_ctx = None
    if with_ctx:
        out_ctx = jnp.concatenate([a_ctx.astype(dt), b_ctx.astype(dt)], axis=-1) @ w_out
    return out_ctx, out_lat


def _ssm_combine(left, right):
    a_l, b_l = left
    a_r, b_r = right
    return a_r * a_l, a_r * b_l + b_r


def s5_scan(u, lam_bar, b_bar, x0):
    bu = jnp.einsum('blgi,gpi->blgp', u, b_bar)
    if x0 is not None:
        bu = bu.at[:, 0].add(lam_bar * x0)
    a = jnp.broadcast_to(lam_bar, (1,) + bu.shape[1:])
    _, xs = lax.associative_scan(_ssm_combine, (a, bu), axis=1)
    return xs


def mixer_s5(h_ctx, h_lat, lam_re, lam_im, log_step, b_re, b_im, c_re, c_im, d_skip, w1, b1, w2, b2, with_ctx):
    def grouped(h):
        return h.astype(jnp.float32).reshape(h.shape[0], h.shape[1], S5_GROUPS, S5_GROUP)

    u_ctx, u_lat = grouped(h_ctx), grouped(h_lat)
    dsk = d_skip.astype(jnp.float32).reshape(S5_GROUPS, S5_GROUP)
    y_lat = dsk * u_lat
    y_ctx = dsk * u_ctx if with_ctx else None
    for d in range(2):
        lam = lax.complex(lam_re[d].astype(jnp.float32), lam_im[d].astype(jnp.float32))
        step = jnp.exp(log_step[d].astype(jnp.float32))[:, None]
        lam_bar = jnp.exp(lam * step)
        b_bar = ((lam_bar - 1.0) / lam)[..., None] * lax.complex(b_re[d].astype(jnp.float32), b_im[d].astype(jnp.float32))
        c_mat = lax.complex(c_re[d].astype(jnp.float32), c_im[d].astype(jnp.float32))
        xc = s5_scan(_flip_seq(u_ctx, d).astype(jnp.complex64), lam_bar, b_bar, None)
        xl = s5_scan(_flip_seq(u_lat, d).astype(jnp.complex64), lam_bar, b_bar, xc[:, -1])
        y_lat = y_lat + _flip_seq(jnp.einsum('blgp,gip->blgi', xl, c_mat).real, d)
        if with_ctx:
            y_ctx = y_ctx + _flip_seq(jnp.einsum('blgp,gip->blgi', xc, c_mat).real, d)

    def glu(y, h):
        z = jax.nn.gelu(y.reshape(y.shape[0], y.shape[1], D_MODEL))
        return ((z @ w1 + b1) * jax.nn.sigmoid(z @ w2 + b2)).astype(h.dtype)

    return (glu(y_ctx, h_ctx) if with_ctx else None), glu(y_lat, h_lat)


def moe_ffn(h, router_w, router_b, w_gate_up, b_gate_up, w_down, b_down):
    n_tok, d_model = h.shape
    logits = jnp.matmul(h, router_w, preferred_element_type=jnp.float32) + router_b.astype(jnp.float32)
    top_val, top_idx = lax.top_k(logits, TOP_K)
    probs = jax.nn.softmax(top_val, axis=-1)
    n_assign = n_tok * TOP_K
    flat_e = top_idx.reshape(-1).astype(jnp.int32)
    flat_tok = jnp.repeat(jnp.arange(n_tok, dtype=jnp.int32), TOP_K)
    order = jnp.argsort(flat_e)
    e_sorted, tok_sorted = flat_e[order], flat_tok[order]
    p_sorted = probs.reshape(-1)[order]
    counts = jnp.bincount(flat_e, length=N_EXPERTS).astype(jnp.int32)
    padded = (counts + MOE_BLOCK - 1) // MOE_BLOCK * MOE_BLOCK
    pad_end = jnp.cumsum(padded)
    pad_start = pad_end - padded
    start = jnp.cumsum(counts) - counts
    dest = pad_start[e_sorted] + jnp.arange(n_assign, dtype=jnp.int32) - start[e_sorted]
    n_blocks = -(-(n_assign + N_EXPERTS * (MOE_BLOCK - 1)) // MOE_BLOCK)
    n_rows = n_blocks * MOE_BLOCK
    row_tok = jnp.full((n_rows,), n_tok, jnp.int32).at[dest].set(tok_sorted)
    h_pad = jnp.concatenate([h, jnp.zeros((1, d_model), h.dtype)], axis=0)
    xb = h_pad[row_tok].reshape(n_blocks, MOE_BLOCK, d_model)
    block_e = jnp.minimum(jnp.searchsorted(pad_end, jnp.arange(n_blocks, dtype=jnp.int32) * MOE_BLOCK, side='right'),
                          N_EXPERTS - 1)

    def expert_block(args):
        xe, e = args
        gu = xe @ w_gate_up[e] + b_gate_up[e]
        gate, up = gu[..., :D_EXPERT], gu[..., D_EXPERT:]
        gate = jnp.minimum(gate, SWIGLU_LIMIT)
        up = jnp.clip(up, -SWIGLU_LIMIT, SWIGLU_LIMIT)
        act = (up + 1) * gate * jax.nn.sigmoid(SWIGLU_ALPHA * gate)
        return act @ w_down[e] + b_down[e]

    yb = lax.map(expert_block, (xb, block_e)).reshape(n_rows, d_model)
    contrib = yb[dest] * p_sorted[:, None].astype(yb.dtype)
    return jax.ops.segment_sum(contrib, tok_sorted, num_segments=n_tok)


def setup_inputs(seed: int = 0) -> dict:
    key = jax.random.key(seed)
    ks = iter(jax.random.split(key, 40))
    f32 = jnp.float32
    D = D_MODEL

    def nrm(shape, scale):
        return scale * jax.random.normal(next(ks), shape, f32)

    def unif(shape, lo, hi):
        return jax.random.uniform(next(ks), shape, f32, lo, hi)

    dt = jnp.exp(unif((N_EVEN, 2, DN_HEADS), math.log(1e-3), math.log(1e-1)))
    lam_im_base = jnp.pi * jnp.arange(S5_STATE, dtype=f32)
    return {
        'x': nrm((BATCH, SEQ, D), 1.0),
        'c': nrm((BATCH, D), 1.0),
        'ctx': nrm((BATCH, CTX_LEN, D), 1.0),
        'c_ctx': nrm((D,), 1.0),
        'ada_w': nrm((DEPTH, D, 6 * D), D ** -0.5),
        'ada_b': nrm((DEPTH, 6 * D), 0.02),
        'ab_w_in': nrm((N_EVEN, D, IN_COLS), D ** -0.5),
        'ab_q_gain': 1.0 + nrm((N_EVEN, HEAD_DIM), 0.02),
        'ab_k_gain': 1.0 + nrm((N_EVEN, HEAD_DIM), 0.02),
        'dn_conv_w': nrm((N_EVEN, DN_CONV, B_QKV), DN_CONV ** -0.5),
        'dn_a_log': jnp.log(unif((N_EVEN, 2, DN_HEADS), 1.0, 16.0)),
        'dn_dt_bias': dt + jnp.log(-jnp.expm1(-dt)),
        'dn_norm_w': 1.0 + nrm((N_EVEN, DN_DV), 0.02),
        'ab_w_out': nrm((N_EVEN, MIX_WIDTH, D), MIX_WIDTH ** -0.5),
        's5_lambda_re': -0.5 + nrm((N_ODD, 2, S5_GROUPS, S5_STATE), 0.02),
        's5_lambda_im': lam_im_base + nrm((N_ODD, 2, S5_GROUPS, S5_STATE), 0.02),
        's5_log_step': unif((N_ODD, 2, S5_GROUPS), math.log(S5_STEP_MIN), math.log(S5_STEP_MAX)),
        's5_b_re': nrm((N_ODD, 2, S5_GROUPS, S5_STATE, S5_GROUP), (2 * S5_GROUP) ** -0.5),
        's5_b_im': nrm((N_ODD, 2, S5_GROUPS, S5_STATE, S5_GROUP), (2 * S5_GROUP) ** -0.5),
        's5_c_re': nrm((N_ODD, 2, S5_GROUPS, S5_GROUP, S5_STATE), (2 * S5_STATE) ** -0.5),
        's5_c_im': nrm((N_ODD, 2, S5_GROUPS, S5_GROUP, S5_STATE), (2 * S5_STATE) ** -0.5),
        's5_d': nrm((N_ODD, D), 1.0),
        's5_glu_w1': nrm((N_ODD, D, D), D ** -0.5),
        's5_glu_b1': nrm((N_ODD, D), 0.02),
        's5_glu_w2': nrm((N_ODD, D, D), D ** -0.5),
        's5_glu_b2': nrm((N_ODD, D), 0.02),
        'moe_router_w': nrm((DEPTH, D, N_EXPERTS), D ** -0.5),
        'moe_router_b': nrm((DEPTH, N_EXPERTS), 0.01),
        'moe_w_gate_up': nrm((DEPTH, N_EXPERTS, D, 2 * D_EXPERT), D ** -0.5),
        'moe_b_gate_up': nrm((DEPTH, N_EXPERTS, 2 * D_EXPERT), 0.02),
        'moe_w_down': nrm((DEPTH, N_EXPERTS, D_EXPERT, D), D_EXPERT ** -0.5),
        'moe_b_down': nrm((DEPTH, N_EXPERTS, D), 0.02),
        'final_norm_w': 1.0 + nrm((D,), 0.02),
    }


def reference(x, c, ctx, c_ctx, ada_w, ada_b, ab_w_in, ab_q_gain, ab_k_gain, dn_conv_w, dn_a_log, dn_dt_bias,
              dn_norm_w, ab_w_out, s5_lambda_re, s5_lambda_im, s5_log_step, s5_b_re, s5_b_im, s5_c_re, s5_c_im,
              s5_d, s5_glu_w1, s5_glu_b1, s5_glu_w2, s5_glu_b2, moe_router_w, moe_router_b, moe_w_gate_up,
              moe_b_gate_up, moe_w_down, moe_b_down, final_norm_w):
    b, n_lat, d_model = x.shape
    n_ctx = ctx.shape[1]
    for i in range(DEPTH):
        with_ctx = i < DEPTH - 1
        mod = jax.nn.silu(c) @ ada_w[i] + ada_b[i]
        sh1, sc1, g1, sh2, sc2, g2 = jnp.split(mod[:, None, :], 6, axis=-1)
        mod_c = jax.nn.silu(c_ctx) @ ada_w[i] + ada_b[i]
        csh1, csc1, cg1, csh2, csc2, cg2 = jnp.split(mod_c, 6, axis=-1)
        h_lat = modulate(rms_norm(x), sh1, sc1)
        h_ctx = modulate(rms_norm(ctx), csh1, csc1)
        j = i // 2
        if i % 2 == 0:
            o_ctx, o_lat = mixer_attn_deltanet(h_ctx, h_lat, ab_w_in[j], ab_q_gain[j], ab_k_gain[j], dn_conv_w[j],
                                               dn_a_log[j], dn_dt_bias[j], dn_norm_w[j], ab_w_out[j], with_ctx)
        else:
            o_ctx, o_lat = mixer_s5(h_ctx, h_lat, s5_lambda_re[j], s5_lambda_im[j], s5_log_step[j], s5_b_re[j],
                                    s5_b_im[j], s5_c_re[j], s5_c_im[j], s5_d[j], s5_glu_w1[j], s5_glu_b1[j],
                                    s5_glu_w2[j], s5_glu_b2[j], with_ctx)
        x = x + g1 * o_lat
        h_lat = modulate(rms_norm(x), sh2, sc2)
        moe_args = (moe_router_w[i], moe_router_b[i], moe_w_gate_up[i], moe_b_gate_up[i], moe_w_down[i], moe_b_down[i])
        if with_ctx:
            ctx = ctx + cg1 * o_ctx
            h_ctx = modulate(rms_norm(ctx), csh2, csc2)
            tokens = jnp.concatenate([h_lat.reshape(-1, d_model), h_ctx.reshape(-1, d_model)], axis=0)
            y = moe_ffn(tokens, *moe_args)
            x = x + g2 * y[:b * n_lat].reshape(b, n_lat, d_model)
            ctx = ctx + cg2 * y[b * n_lat:].reshape(b, n_ctx, d_model)
        else:
            x = x + g2 * moe_ffn(h_lat.reshape(-1, d_model), *moe_args).reshape(b, n_lat, d_model)
    return rms_norm(x, final_norm_w)
```

```python
import functools
import math

import jax
import jax.numpy as jnp
from jax import lax
from jax.experimental import pallas as pl
from jax.experimental.pallas import tpu as pltpu

F32 = jnp.float32
BF16 = jnp.bfloat16
U32 = jnp.uint32
I32 = jnp.int32

EPS = 1e-6
GRID_W = 64
HEAD_DIM = 64
Q_HEADS = 8
KV_HEADS = 2
GROUP = Q_HEADS // KV_HEADS
ROPE_THETA = 10000.0
DN_HEADS = 8
DN_DK = 64
DN_CONV = 5
DN_CHUNK = 64
S5_GROUP = 16
S5_STATE = 64
N_EXPERTS = 32
TOP_K = 4
SWIGLU_LIMIT = 7.0
SWIGLU_ALPHA = 1.702

LANES = 128
SUBLANES = 8
TOKEN_TILE = 256
ATTN_Q_TILE = 128
ATTN_K_TILE = 512
DN_BLOCK_CHUNKS = 4
MOE_ROWS = 512
S5_PITCH_PAD = 4
VMEM_LIMIT = 56 * 1024 * 1024

MOD_SH1, MOD_SC1, MOD_G1, MOD_SH2, MOD_SC2, MOD_G2 = range(6)
MOD_ROWS = 8

COL_Q = 0
COL_DNQ = 512
COL_DNK = 1024
COL_DNV = 1536
COL_Z = 2048
COL_K = 2560
COL_V = 2688
COL_AB = 2816
IN_COLS_PAD = 2944


def _cparams(sem):
    return pltpu.CompilerParams(dimension_semantics=sem, vmem_limit_bytes=VMEM_LIMIT)


def _split3(a):
    a1 = a.astype(BF16)
    r = a - a1.astype(F32)
    a2 = r.astype(BF16)
    a3 = (r - a2.astype(F32)).astype(BF16)
    return a1, a2, a3


def _dot(a, b):
    return jnp.dot(a, b, preferred_element_type=F32)


def _dot_exact_lhs(m_bf16, x):
    x1, x2, x3 = _split3(x)
    return _dot(m_bf16, x1) + _dot(m_bf16, x2) + _dot(m_bf16, x3)


def _dot_exact_rhs(x, m_bf16):
    x1, x2, x3 = _split3(x)
    return _dot(x1, m_bf16) + _dot(x2, m_bf16) + _dot(x3, m_bf16)


def _dot_hi(a, b):
    a1, a2, _ = _split3(a)
    b1, b2, _ = _split3(b)
    return _dot(a1, b1) + _dot(a1, b2) + _dot(a2, b1)


def _rms(x):
    return x * lax.rsqrt(jnp.mean(x * x, axis=-1, keepdims=True) + EPS)


def _modulated(x_ref, sh_ref, sc_ref):
    return _rms(x_ref[...]) * (1.0 + sc_ref[0]) + sh_ref[0]


def _mod_spec(layer, row_fn, which):
    d = None

    def imap(b, t):
        return ((layer * MOD_ROWS + row_fn(b)) * 6 + which, 0, 0)

    return imap


def _ada_kernel(a_ref, w_ref, b_ref, o_ref):
    a = a_ref[...]
    a = a * jax.nn.sigmoid(a)
    o_ref[0] = _dot_hi(a, w_ref[0]) + b_ref[0]


def _ada(rows, ada_w, ada_b):
    depth, d, n = ada_w.shape
    tn = 1536
    return pl.pallas_call(
        _ada_kernel,
        grid=(depth, n // tn),
        in_specs=[pl.BlockSpec((MOD_ROWS, d), lambda l, j: (0, 0)),
                  pl.BlockSpec((1, d, tn), lambda l, j: (l, 0, j)),
                  pl.BlockSpec((1, 1, tn), lambda l, j: (l, 0, j))],
        out_specs=pl.BlockSpec((1, MOD_ROWS, tn), lambda l, j: (l, 0, j)),
        out_shape=jax.ShapeDtypeStruct((depth, MOD_ROWS, n), F32),
        compiler_params=_cparams(("arbitrary", "arbitrary")),
        name="ada_mod",
    )(rows, ada_w, ada_b.reshape(depth, 1, n))


def _inproj_kernel(x_ref, sh_ref, sc_ref, w_ref, o_ref):
    h = _modulated(x_ref, sh_ref, sc_ref)
    o_ref[...] = _dot(h.astype(BF16), w_ref[...])


def _inproj(x, mod3, layer, row_fn, w_bf16):
    b, l, d = x.shape
    tm = min(TOKEN_TILE, l)
    n = w_bf16.shape[1]
    mspec = lambda which: pl.BlockSpec((1, 1, d), _mod_spec(layer, row_fn, which))
    return pl.pallas_call(
        _inproj_kernel,
        grid=(b, l // tm),
        in_specs=[pl.BlockSpec((None, tm, d), lambda bb, t: (bb, t, 0)),
                  mspec(MOD_SH1), mspec(MOD_SC1),
                  pl.BlockSpec((d, n), lambda bb, t: (0, 0))],
        out_specs=pl.BlockSpec((None, tm, n), lambda bb, t: (bb, t, 0)),
        out_shape=jax.ShapeDtypeStruct((b, l, n), F32),
        compiler_params=_cparams(("arbitrary", "arbitrary")),
        name="in_proj",
    )(x, mod3, mod3, w_bf16)


def _head_rot(x, width):
    lane = lax.broadcasted_iota(I32, x.shape, 1)
    first = (lane % HEAD_DIM) < (HEAD_DIM // 2)
    return jnp.where(first, pltpu.roll(x, width - HEAD_DIM // 2, 1), pltpu.roll(x, HEAD_DIM // 2, 1))


def _qkprep_kernel(q_ref, k_ref, v_ref, qg_ref, kg_ref, cos_ref, sin_ref, gq_ref, gk_ref, qo_ref, kt_ref, vo_ref):
    cos = cos_ref[...]
    sin = sin_ref[...]
    q = q_ref[...]
    qn = q * lax.rsqrt(_dot_exact_rhs(q * q, gq_ref[...]) + EPS) * qg_ref[...]
    cos4 = jnp.concatenate([cos] * 4, axis=1)
    sin4 = jnp.concatenate([sin] * 4, axis=1)
    qr = qn * cos4 + _head_rot(qn, Q_HEADS * HEAD_DIM) * sin4
    qo_ref[...] = (qr * (HEAD_DIM ** -0.5 * math.log2(math.e))).astype(BF16)
    k = k_ref[...]
    kn = k * lax.rsqrt(_dot_exact_rhs(k * k, gk_ref[...]) + EPS) * kg_ref[...]
    kr = kn * cos + _head_rot(kn, KV_HEADS * HEAD_DIM) * sin
    kt = kr.T
    kt_ref[0] = kt[:HEAD_DIM].astype(BF16)
    kt_ref[1] = kt[HEAD_DIM:].astype(BF16)
    v = v_ref[...]
    vo_ref[0] = v[:, :HEAD_DIM].astype(BF16)
    vo_ref[1] = v[:, HEAD_DIM:].astype(BF16)


def _group_matrix(width, scale):
    g = jnp.arange(width) // HEAD_DIM
    return ((g[:, None] == g[None, :]).astype(F32) * scale).astype(BF16)


def _qkprep(proj, q_gain, k_gain, cos, sin, tk):
    b, l, _ = proj.shape
    qw = Q_HEADS * HEAD_DIM
    kw = KV_HEADS * HEAD_DIM
    qg = jnp.tile(q_gain.astype(F32), Q_HEADS).reshape(1, qw)
    kg = jnp.tile(k_gain.astype(F32), KV_HEADS).reshape(1, kw)
    const = lambda shape: pl.BlockSpec(shape, lambda bb, t: (0,) * len(shape))
    return pl.pallas_call(
        _qkprep_kernel,
        grid=(b, l // tk),
        in_specs=[pl.BlockSpec((None, tk, qw), lambda bb, t: (bb, t, COL_Q // qw)),
                  pl.BlockSpec((None, tk, kw), lambda bb, t: (bb, t, COL_K // kw)),
                  pl.BlockSpec((None, tk, kw), lambda bb, t: (bb, t, COL_V // kw)),
                  const((1, qw)), const((1, kw)),
                  pl.BlockSpec((tk, kw), lambda bb, t: (t, 0)),
                  pl.BlockSpec((tk, kw), lambda bb, t: (t, 0)),
                  const((qw, qw)), const((kw, kw))],
        out_specs=[pl.BlockSpec((None, tk, qw), lambda bb, t: (bb, t, 0)),
                   pl.BlockSpec((None, KV_HEADS, None, HEAD_DIM, tk), lambda bb, t: (bb, 0, t, 0, 0)),
                   pl.BlockSpec((None, KV_HEADS, tk, HEAD_DIM), lambda bb, t: (bb, 0, t, 0))],
        out_shape=[jax.ShapeDtypeStruct((b, l, qw), BF16),
                   jax.ShapeDtypeStruct((b, KV_HEADS, l // tk, HEAD_DIM, tk), BF16),
                   jax.ShapeDtypeStruct((b, KV_HEADS, l, HEAD_DIM), BF16)],
        compiler_params=_cparams(("arbitrary", "arbitrary")),
        name="qk_prep",
    )(proj, proj, proj, qg, kg, cos, sin, _group_matrix(qw, 1.0 / HEAD_DIM), _group_matrix(kw, 1.0 / HEAD_DIM))


def _rope_tables(length):
    rows = length // GRID_W
    row = jnp.broadcast_to(jnp.arange(rows, dtype=I32)[:, None], (rows, GRID_W)).reshape(-1).astype(F32)
    col = jnp.broadcast_to(jnp.arange(GRID_W, dtype=I32)[None, :], (rows, GRID_W)).reshape(-1).astype(F32)
    n_axis = HEAD_DIM // 4
    inv_freq = ROPE_THETA ** (-jnp.arange(n_axis, dtype=F32) / n_axis)
    ang = jnp.concatenate([row[:, None] * inv_freq, col[:, None] * inv_freq], axis=-1)
    c, s = jnp.cos(ang), jnp.sin(ang)
    cos_h = jnp.concatenate([c, c], axis=-1)
    sin_h = jnp.concatenate([-s, s], axis=-1)
    return jnp.tile(cos_h, (1, KV_HEADS)), jnp.tile(sin_h, (1, KV_HEADS))


def _attn_kernel(*refs, tq, n_lat_chunks, has_lat):
    if has_lat:
        q_ref, ktc_ref, vc_ref, ktl_ref, vl_ref, o_ref = refs
    else:
        q_ref, ktc_ref, vc_ref, o_ref = refs
    q = q_ref[...]
    qs = jnp.concatenate([q[:, g * HEAD_DIM:(g + 1) * HEAD_DIM] for g in range(GROUP)], axis=0)
    rows = GROUP * tq

    def step(kt, v, m, l, acc):
        s = _dot(qs, kt)
        m_new = jnp.maximum(m, jnp.max(s, axis=-1, keepdims=True))
        alpha = jnp.exp2(m - m_new)
        p = jnp.exp2(s - m_new)
        l = alpha * l + jnp.sum(p, axis=-1, keepdims=True)
        acc = alpha * acc + _dot(p.astype(BF16), v)
        return m_new, l, acc

    m0 = jnp.full((rows, 1), -1e30, F32)
    l0 = jnp.zeros((rows, 1), F32)
    a0 = jnp.zeros((rows, HEAD_DIM), F32)
    carry = step(ktc_ref[0], vc_ref[...], m0, l0, a0)
    if has_lat:
        tk = ktl_ref.shape[-1]

        def body(j, c):
            off = pl.multiple_of(j * tk, tk)
            return step(ktl_ref[j], vl_ref[pl.ds(off, tk), :], *c)

        carry = lax.fori_loop(0, n_lat_chunks, body, carry)
    _, l, acc = carry
    o = acc / l
    o_ref[...] = jnp.concatenate([o[g * tq:(g + 1) * tq] for g in range(GROUP)], axis=1).astype(o_ref.dtype)


def _attention(q, ktc, vc, ktl=None, vl=None):
    b, lq, qw = q.shape
    tq = min(ATTN_Q_TILE, lq)
    gw = GROUP * HEAD_DIM
    lc = vc.shape[2]
    has_lat = ktl is not None
    in_specs = [pl.BlockSpec((None, tq, gw), lambda bb, h, i: (bb, i, h)),
                pl.BlockSpec((None, None, 1, HEAD_DIM, lc), lambda bb, h, i: (bb, h, 0, 0, 0)),
                pl.BlockSpec((None, None, lc, HEAD_DIM), lambda bb, h, i: (bb, h, 0, 0))]
    args = [q, ktc, vc]
    n_chunks = 0
    if has_lat:
        n_chunks, tk = ktl.shape[2], ktl.shape[4]
        ll = vl.shape[2]
        in_specs += [pl.BlockSpec((None, None, n_chunks, HEAD_DIM, tk), lambda bb, h, i: (bb, h, 0, 0, 0)),
                     pl.BlockSpec((None, None, ll, HEAD_DIM), lambda bb, h, i: (bb, h, 0, 0))]
        args += [ktl, vl]
    return pl.pallas_call(
        functools.partial(_attn_kernel, tq=tq, n_lat_chunks=n_chunks, has_lat=has_lat),
        grid=(b, KV_HEADS, lq // tq),
        in_specs=in_specs,
        out_specs=pl.BlockSpec((None, tq, gw), lambda bb, h, i: (bb, i, h)),
        out_shape=jax.ShapeDtypeStruct((b, lq, qw), BF16),
        compiler_params=_cparams(("arbitrary", "arbitrary", "arbitrary")),
        name="gqa_attention",
    )(*args)


def _dnprep_kernel(qp, qc, qn, kp, kc, kn, vp, vc, vn, ab_ref, cw_ref, alog_ref, dtb_ref, gm_ref, trip_ref, tris_ref,
                   qo_ref, ko_ref, kto_ref, vo_ref, gc_ref, gct_ref, *, tm):
    t = pl.program_id(1)
    nt = pl.num_programs(1)
    mp = (t > 0).astype(F32)
    mn = (t < nt - 1).astype(F32)
    n_ext = tm + 2 * SUBLANES
    pad = DN_CONV // 2

    def conv(p_ref, c_ref, n_ref, w):
        ext = jnp.concatenate([p_ref[...] * mp, c_ref[...], n_ref[...] * mn], axis=0)
        acc = None
        for j in range(DN_CONV):
            s = (pad - j) % n_ext
            r = ext if s == 0 else pltpu.roll(ext, s, 0)
            term = r[SUBLANES:SUBLANES + tm] * w[j:j + 1, :]
            acc = term if acc is None else acc + term
        return acc * jax.nn.sigmoid(acc)

    cw = cw_ref[...]
    hw = DN_HEADS * DN_DK
    gm = gm_ref[...]
    q = conv(qp, qc, qn, cw[:, 0:hw])
    q = q * lax.rsqrt(_dot_exact_rhs(q * q, gm) + EPS) * (DN_DK ** -0.5)
    k = conv(kp, kc, kn, cw[:, hw:2 * hw])
    k = k * lax.rsqrt(_dot_exact_rhs(k * k, gm) + EPS)
    v = conv(vp, vc, vn, cw[:, 2 * hw:3 * hw])
    kt = k.T
    for h in range(DN_HEADS):
        sl = slice(h * DN_DK, (h + 1) * DN_DK)
        qo_ref[h] = q[:, sl]
        ko_ref[h] = k[:, sl]
        vo_ref[h] = v[:, sl]
        for c in range(tm // DN_CHUNK):
            kto_ref[h, c] = kt[sl, c * DN_CHUNK:(c + 1) * DN_CHUNK]

    ab = ab_ref[...]
    lane = lax.broadcasted_iota(I32, ab.shape, 1)
    a = ab + dtb_ref[...]
    sp = jnp.maximum(a, 0.0) + jnp.log(1.0 + jnp.exp(-jnp.abs(a)))
    g = -jnp.exp(alog_ref[...]) * sp
    g = jnp.where(lane < 2 * DN_HEADS, g, 0.0)
    gpre = _dot_exact_lhs(trip_ref[...], g)
    gsuf = _dot_exact_lhs(tris_ref[...], g)
    gc = jnp.where(lane < DN_HEADS, gpre, jnp.where(lane < 2 * DN_HEADS, gsuf, jax.nn.sigmoid(ab)))
    gc_ref[...] = gc
    gt = gc.T
    for c in range(tm // DN_CHUNK):
        gct_ref[c] = gt[:4 * DN_HEADS, c * DN_CHUNK:(c + 1) * DN_CHUNK]


def _dnprep(proj, conv_w, a_log, dt_bias):
    b, l, _ = proj.shape
    tm = min(TOKEN_TILE, l)
    hw = DN_HEADS * DN_DK
    nc = l // DN_CHUNK
    r8 = tm // SUBLANES
    last8 = l // SUBLANES - 1

    def cur(col):
        return pl.BlockSpec((None, tm, hw), lambda bb, t: (bb, t, col // hw))

    def prev(col):
        return pl.BlockSpec((None, SUBLANES, hw), lambda bb, t: (bb, jnp.maximum(t * r8 - 1, 0), col // hw))

    def nxt(col):
        return pl.BlockSpec((None, SUBLANES, hw), lambda bb, t: (bb, jnp.minimum((t + 1) * r8, last8), col // hw))

    const = lambda shape: pl.BlockSpec(shape, lambda bb, t: (0,) * len(shape))
    alog = jnp.zeros((1, LANES), F32).at[0, :2 * DN_HEADS].set(a_log.astype(F32).reshape(-1))
    dtb = jnp.zeros((1, LANES), F32).at[0, :2 * DN_HEADS].set(dt_bias.astype(F32).reshape(-1))
    ch = jnp.arange(tm) // DN_CHUNK
    same = ch[:, None] == ch[None, :]
    idx = jnp.arange(tm)
    trip = (same & (idx[:, None] >= idx[None, :])).astype(BF16)
    tris = (same & (idx[:, None] <= idx[None, :])).astype(BF16)
    gmat = _group_matrix(hw, 1.0)
    in_specs = []
    args = []
    for col in (COL_DNQ, COL_DNK, COL_DNV):
        in_specs += [prev(col), cur(col), nxt(col)]
        args += [proj, proj, proj]
    in_specs += [pl.BlockSpec((None, tm, LANES), lambda bb, t: (bb, t, COL_AB // LANES)),
                 const((DN_CONV, 3 * hw)), const((1, LANES)), const((1, LANES)), const((hw, hw)),
                 const((tm, tm)), const((tm, tm))]
    args += [proj, conv_w.astype(F32), alog, dtb, gmat, trip, tris]
    head_spec = pl.BlockSpec((None, DN_HEADS, tm, DN_DK), lambda bb, t: (bb, 0, t, 0))
    head_shape = jax.ShapeDtypeStruct((b, DN_HEADS, l, DN_DK), F32)
    return pl.pallas_call(
        functools.partial(_dnprep_kernel, tm=tm),
        grid=(b, l // tm),
        in_specs=in_specs,
        out_specs=[head_spec, head_spec,
                   pl.BlockSpec((None, DN_HEADS, tm // DN_CHUNK, DN_DK, DN_CHUNK), lambda bb, t: (bb, 0, t, 0, 0)),
                   head_spec,
                   pl.BlockSpec((None, tm, LANES), lambda bb, t: (bb, t, 0)),
                   pl.BlockSpec((None, tm // DN_CHUNK, 4 * DN_HEADS, DN_CHUNK), lambda bb, t: (bb, t, 0, 0))],
        out_shape=[head_shape, head_shape,
                   jax.ShapeDtypeStruct((b, DN_HEADS, nc, DN_DK, DN_CHUNK), F32),
                   head_shape,
                   jax.ShapeDtypeStruct((b, l, LANES), F32),
                   jax.ShapeDtypeStruct((b, nc, 4 * DN_HEADS, DN_CHUNK), F32)],
        compiler_params=_cparams(("arbitrary", "arbitrary")),
        name="dn_prep",
    )(*args)


def _unit_tri_inverse(a, eye, blks):
    inner = jnp.where(blks[0], a, 0.0)
    x = eye - inner
    p = inner
    for _ in range(2):
        pb = p.astype(BF16)
        p = _dot(pb, pb)
        x = x + _dot(x.astype(BF16), p.astype(BF16))
    for level in range(1, len(blks) + 1):
        outer = jnp.where(blks[level], a, 0.0) if level < len(blks) else a
        off = (outer - inner).astype(BF16)
        xb = x.astype(BF16)
        x = x - _dot(xb, _dot(off, xb).astype(BF16))
        inner = outer
    return x


def _dn_chunk_math(direction, qh, kh, kth, vh, gcol, grow, beta, g_last, s_h, incl, strict, eye, blks):
    diff = gcol - grow
    decay = jnp.where(incl, jnp.exp(jnp.where(incl, diff, 0.0)), 0.0)
    kb = kh * beta
    kt_b = kth.astype(BF16)
    a = jnp.where(strict, _dot(kb.astype(BF16), kt_b) * decay, 0.0)
    x = _unit_tri_inverse(a, eye, blks)
    rhs = jnp.concatenate([vh * beta, kb * jnp.exp(gcol)], axis=1)
    sol = _dot(x.astype(BF16), rhs.astype(BF16))
    u = sol[:, :DN_DK]
    w = sol[:, DN_DK:]
    sb = s_h.astype(BF16)
    v_new = u - _dot(w.astype(BF16), sb)
    vnb = v_new.astype(BF16)
    kdec_t = kth * jnp.exp(g_last - grow)
    s_next = s_h * jnp.exp(g_last) + _dot(kdec_t.astype(BF16), vnb)
    qk = _dot(qh.astype(BF16), kt_b) * decay
    o = _dot((qh * jnp.exp(gcol)).astype(BF16), sb) + _dot(qk.astype(BF16), vnb)
    return o, s_next


def _dnscan_kernel(qf, kf, ktf, vf, gcf, gctf, qb, kb_, ktb, vb, gcb, gctb, s0_ref, of_ref, ob_ref, sout_ref, s_scr, *, cb):
    i = pl.program_id(1)
    last = pl.num_programs(1) - 1

    @pl.when(i == 0)
    def _():
        s_scr[...] = s0_ref[...]

    row = lax.broadcasted_iota(I32, (DN_CHUNK, DN_CHUNK), 0)
    col = lax.broadcasted_iota(I32, (DN_CHUNK, DN_CHUNK), 1)
    eye = (row == col).astype(F32)
    blks = tuple((row // n) == (col // n) for n in (8, 16, 32))
    dirs = ((0, qf, kf, ktf, vf, gcf, gctf, of_ref, row >= col, row > col),
            (1, qb, kb_, ktb, vb, gcb, gctb, ob_ref, row <= col, row < col))
    for jj in range(cb):
        for d, q_ref, k_ref, kt_ref, v_ref, gc_ref, gct_ref, o_ref, incl, strict in dirs:
            c = jj if d == 0 else cb - 1 - jj
            rs = slice(c * DN_CHUNK, (c + 1) * DN_CHUNK)
            gc = gc_ref[rs, :]
            gct = gct_ref[c]
            last_row = DN_CHUNK - 1 if d == 0 else 0
            for h in range(DN_HEADS):
                gl = d * DN_HEADS + h
                bl = 2 * DN_HEADS + d * DN_HEADS + h
                gcol = gc[:, gl:gl + 1]
                grow = gct[gl:gl + 1, :]
                beta = gc[:, bl:bl + 1]
                g_last = gcol[last_row:last_row + 1, :]
                o, s_next = _dn_chunk_math(d, q_ref[h, rs, :], k_ref[h, rs, :], kt_ref[h, c], v_ref[h, rs, :],
                                           gcol, grow, beta, g_last, s_scr[d, h], incl, strict, eye, blks)
                o_ref[h, rs, :] = o
                s_scr[d, h] = s_next

    @pl.when(i == last)
    def _():
        sout_ref[...] = s_scr[...]


def _dnscan(q, k, kt, v, gc, gct, s0):
    b, nh, l, dk = q.shape
    cb = min(DN_BLOCK_CHUNKS, l // DN_CHUNK)
    tb = cb * DN_CHUNK
    nblk = l // tb
    fwd = lambda i: i
    bwd = lambda i: nblk - 1 - i

    def specs(ix):
        head = pl.BlockSpec((None, nh, tb, dk), lambda bb, i: (bb, 0, ix(i), 0))
        return [head, head,
                pl.BlockSpec((None, nh, cb, dk, DN_CHUNK), lambda bb, i: (bb, 0, ix(i), 0, 0)),
                head,
                pl.BlockSpec((None, tb, LANES), lambda bb, i: (bb, ix(i), 0)),
                pl.BlockSpec((None, cb, 4 * DN_HEADS, DN_CHUNK), lambda bb, i: (bb, ix(i), 0, 0))]

    state_spec = pl.BlockSpec((None, 2, nh, dk, dk), lambda bb, i: (bb, 0, 0, 0, 0))
    o_shape = jax.ShapeDtypeStruct((b, nh, l, dk), F32)
    return pl.pallas_call(
        functools.partial(_dnscan_kernel, cb=cb),
        grid=(b, nblk),
        in_specs=specs(fwd) + specs(bwd) + [state_spec],
        out_specs=[pl.BlockSpec((None, nh, tb, dk), lambda bb, i: (bb, 0, fwd(i), 0)),
                   pl.BlockSpec((None, nh, tb, dk), lambda bb, i: (bb, 0, bwd(i), 0)),
                   state_spec],
        out_shape=[o_shape, o_shape, jax.ShapeDtypeStruct((b, 2, nh, dk, dk), F32)],
        scratch_shapes=[pltpu.VMEM((2, nh, dk, dk), F32)],
        compiler_params=_cparams(("arbitrary", "arbitrary")),
        name="dn_scan",
    )(q, k, kt, v, gc, gct, q, k, kt, v, gc, gct, s0)


def _mix0_out_kernel(x_ref, g1_ref, a_ref, of_ref, ob_ref, z_ref, nw_ref, w_ref, o_ref):
    o = of_ref[...] + ob_ref[...]
    on = o * lax.rsqrt(jnp.mean(o * o, axis=-1, keepdims=True) + EPS) * nw_ref[...]
    ot = jnp.concatenate([on[h] for h in range(DN_HEADS)], axis=1)
    z = z_ref[...]
    bmix = ot * (z * jax.nn.sigmoid(z))
    mix = jnp.concatenate([a_ref[...], bmix.astype(BF16)], axis=1)
    o_ref[...] = x_ref[...] + g1_ref[0] * _dot(mix, w_ref[...])


def _mix0_out(x, mod3, layer, row_fn, attn, o_f, o_b, proj, norm_w, w_out_bf16):
    b, l, d = x.shape
    tm = min(TOKEN_TILE, l)
    aw = Q_HEADS * HEAD_DIM
    zw = DN_HEADS * DN_DK
    head = pl.BlockSpec((None, DN_HEADS, tm, DN_DK), lambda bb, t: (bb, 0, t, 0))
    return pl.pallas_call(
        _mix0_out_kernel,
        grid=(b, l // tm),
        in_specs=[pl.BlockSpec((None, tm, d), lambda bb, t: (bb, t, 0)),
                  pl.BlockSpec((1, 1, d), _mod_spec(layer, row_fn, MOD_G1)),
                  pl.BlockSpec((None, tm, aw), lambda bb, t: (bb, t, 0)),
                  head, head,
                  pl.BlockSpec((None, tm, zw), lambda bb, t: (bb, t, COL_Z // zw)),
                  pl.BlockSpec((1, DN_DK), lambda bb, t: (0, 0)),
                  pl.BlockSpec((aw + zw, d), lambda bb, t: (0, 0))],
        out_specs=pl.BlockSpec((None, tm, d), lambda bb, t: (bb, t, 0)),
        out_shape=jax.ShapeDtypeStruct((b, l, d), F32),
        compiler_params=_cparams(("arbitrary", "arbitrary")),
        name="mix0_out",
    )(x, mod3, attn, o_f, o_b, proj, norm_w.astype(F32).reshape(1, DN_DK), w_out_bf16)


def _s5_kernel(x_ref, sh_ref, sc_ref, wb_ref, wc_ref, lam_ref, x0_ref, y_ref, xfin_ref, bre, bim, st, *, tm, pitch):
    d = pl.program_id(1)
    i = pl.program_id(2)
    nsub = SUBLANES
    nj = bre.shape[0]
    h = _modulated(x_ref, sh_ref, sc_ref).astype(BF16)
    half = nj * LANES
    for s in range(nsub):
        o = _dot(h[:, s * LANES:(s + 1) * LANES], wb_ref[s])
        for j in range(nj):
            bre[j, s * pitch:s * pitch + tm, :] = o[:, j * LANES:(j + 1) * LANES]
            bim[j, s * pitch:s * pitch + tm, :] = o[:, half + j * LANES:half + (j + 1) * LANES]

    @pl.when(i == 0)
    def _():
        st[...] = x0_ref[...]

    lam_re = lam_ref[0]
    lam_im = lam_ref[1]

    def body(k, carry):
        xr, xi = carry
        t = jnp.where(d == 0, k, tm - 1 - k)
        br = jnp.concatenate([bre[j, pl.ds(t, nsub, stride=pitch), :] for j in range(nj)], axis=1)
        bi = jnp.concatenate([bim[j, pl.ds(t, nsub, stride=pitch), :] for j in range(nj)], axis=1)
        nr = lam_re * xr - lam_im * xi + br
        ni = lam_re * xi + lam_im * xr + bi
        for j in range(nj):
            bre[j, pl.ds(t, nsub, stride=pitch), :] = nr[:, j * LANES:(j + 1) * LANES]
            bim[j, pl.ds(t, nsub, stride=pitch), :] = ni[:, j * LANES:(j + 1) * LANES]
        return nr, ni

    xr, xi = lax.fori_loop(0, tm, body, (st[0], st[1]), unroll=8)
    st[0] = xr
    st[1] = xi
    xfin_ref[0] = xr
    xfin_ref[1] = xi
    for s in range(nsub):
        parts = [bre[j, s * pitch:s * pitch + tm, :] for j in range(nj)]
        parts += [bim[j, s * pitch:s * pitch + tm, :] for j in range(nj)]
        xs = jnp.concatenate(parts, axis=1).astype(BF16)
        y_ref[:, s * LANES:(s + 1) * LANES] = _dot(xs, wc_ref[s])


def _s5_scan(x, mod3, layer, row_fn, wb, wc, lam, x0):
    b, l, d = x.shape
    tm = min(TOKEN_TILE, l)
    nt = l // tm
    pitch = tm + S5_PITCH_PAD
    sw = wb.shape[-1] // 2
    nj = sw // LANES
    tile = lambda dd, i: jnp.where(dd == 0, i, nt - 1 - i)
    mspec = lambda which: pl.BlockSpec((1, 1, d), lambda bb, dd, i: _mod_spec(layer, row_fn, which)(bb, i))
    return pl.pallas_call(
        functools.partial(_s5_kernel, tm=tm, pitch=pitch),
        grid=(b, 2, nt),
        in_specs=[pl.BlockSpec((None, tm, d), lambda bb, dd, i: (bb, tile(dd, i), 0)),
                  mspec(MOD_SH1), mspec(MOD_SC1),
                  pl.BlockSpec((None, SUBLANES, LANES, 2 * sw), lambda bb, dd, i: (dd, 0, 0, 0)),
                  pl.BlockSpec((None, SUBLANES, 2 * sw, LANES), lambda bb, dd, i: (dd, 0, 0, 0)),
                  pl.BlockSpec((None, 2, SUBLANES, sw), lambda bb, dd, i: (dd, 0, 0, 0)),
                  pl.BlockSpec((None, None, 2, SUBLANES, sw), lambda bb, dd, i: (bb, dd, 0, 0, 0))],
        out_specs=[pl.BlockSpec((None, None, tm, d), lambda bb, dd, i: (dd, bb, tile(dd, i), 0)),
                   pl.BlockSpec((None, None, 2, SUBLANES, sw), lambda bb, dd, i: (bb, dd, 0, 0, 0))],
        out_shape=[jax.ShapeDtypeStruct((2, b, l, d), F32),
                   jax.ShapeDtypeStruct((b, 2, 2, SUBLANES, sw), F32)],
        scratch_shapes=[pltpu.VMEM((nj, SUBLANES * pitch, LANES), F32),
                        pltpu.VMEM((nj, SUBLANES * pitch, LANES), F32),
                        pltpu.VMEM((2, SUBLANES, sw), F32)],
        compiler_params=_cparams(("arbitrary", "arbitrary", "arbitrary")),
        name="s5_scan",
    )(x, mod3, mod3, wb, wc, lam, x0)


def _s5_weights(lam_re, lam_im, log_step, b_re, b_im, c_re, c_im):
    lam = lax.complex(lam_re.astype(F32), lam_im.astype(F32))
    step = jnp.exp(log_step.astype(F32))[..., None]
    lam_bar = jnp.exp(lam * step)
    b_bar = ((lam_bar - 1.0) / lam)[..., None] * lax.complex(b_re.astype(F32), b_im.astype(F32))
    ng, p = lam.shape[1], lam.shape[2]
    gl = ng // SUBLANES
    eye = jnp.eye(gl, dtype=F32)

    def wb_of(part):
        t = part.reshape(2, SUBLANES, gl, p, S5_GROUP)
        w = jnp.einsum('dsgpi,gh->dsgihp', t, eye)
        return w.reshape(2, SUBLANES, gl * S5_GROUP, gl * p)

    def wc_of(part):
        t = part.reshape(2, SUBLANES, gl, S5_GROUP, p)
        w = jnp.einsum('dsgip,gh->dshpgi', t, eye)
        return w.reshape(2, SUBLANES, gl * p, gl * S5_GROUP)

    wb = jnp.concatenate([wb_of(jnp.real(b_bar)), wb_of(jnp.imag(b_bar))], axis=-1).astype(BF16)
    wc = jnp.concatenate([wc_of(c_re.astype(F32)), -wc_of(c_im.astype(F32))], axis=-2).astype(BF16)
    lam_t = jnp.stack([jnp.real(lam_bar).reshape(2, SUBLANES, gl * p),
                       jnp.imag(lam_bar).reshape(2, SUBLANES, gl * p)], axis=1)
    return wb, wc, lam_t


def _s5_glu_kernel(x_ref, sh_ref, sc_ref, g1_ref, yf_ref, yb_ref, dsk_ref, w1_ref, b1_ref, w2_ref, b2_ref, o_ref):
    x = x_ref[...]
    h = _rms(x) * (1.0 + sc_ref[0]) + sh_ref[0]
    y = dsk_ref[...] * h + yf_ref[...] + yb_ref[...]
    z = jax.nn.gelu(y).astype(BF16)
    o = (_dot(z, w1_ref[...]) + b1_ref[...]) * jax.nn.sigmoid(_dot(z, w2_ref[...]) + b2_ref[...])
    o_ref[...] = x + g1_ref[0] * o


def _s5_glu(x, mod3, layer, row_fn, y2, d_skip, w1, b1, w2, b2):
    b, l, d = x.shape
    tm = min(TOKEN_TILE, l)
    mspec = lambda which: pl.BlockSpec((1, 1, d), _mod_spec(layer, row_fn, which))
    const = lambda shape: pl.BlockSpec(shape, lambda bb, t: (0,) * len(shape))
    return pl.pallas_call(
        _s5_glu_kernel,
        grid=(b, l // tm),
        in_specs=[pl.BlockSpec((None, tm, d), lambda bb, t: (bb, t, 0)),
                  mspec(MOD_SH1), mspec(MOD_SC1), mspec(MOD_G1),
                  pl.BlockSpec((None, None, tm, d), lambda bb, t: (0, bb, t, 0)),
                  pl.BlockSpec((None, None, tm, d), lambda bb, t: (1, bb, t, 0)),
                  const((1, d)), const((d, d)), const((1, d)), const((d, d)), const((1, d))],
        out_specs=pl.BlockSpec((None, tm, d), lambda bb, t: (bb, t, 0)),
        out_shape=jax.ShapeDtypeStruct((b, l, d), F32),
        compiler_params=_cparams(("arbitrary", "arbitrary")),
        name="s5_glu",
    )(x, mod3, mod3, mod3, y2, y2, d_skip.astype(F32).reshape(1, d), w1.astype(BF16), b1.astype(F32).reshape(1, d),
      w2.astype(BF16), b2.astype(F32).reshape(1, d))


def _route_kernel(x_ref, sh_ref, sc_ref, rw_ref, rb_ref, cin_ref, tri_ref, hp_ref, meta_ref, cnt_ref, carry):
    first = (pl.program_id(0) == 0) & (pl.program_id(1) == 0)

    @pl.when(first)
    def _():
        carry[...] = cin_ref[...]

    h = _modulated(x_ref, sh_ref, sc_ref)
    half = h.shape[1] // 2
    lo = pltpu.bitcast(h[:, :half].astype(BF16).astype(F32), U32)
    hi = pltpu.bitcast(h[:, half:].astype(BF16).astype(F32), U32)
    hp_ref[...] = (lo >> 16) | (hi & jnp.uint32(0xFFFF0000))

    logits = _dot_hi(h, rw_ref[...]) + rb_ref[...]
    lane = lax.broadcasted_iota(I32, logits.shape, 1)
    cur = jnp.where(lane < N_EXPERTS, logits, -jnp.inf)
    vals, idxs = [], []
    hot = jnp.zeros(logits.shape, F32)
    for _ in range(TOP_K):
        m = jnp.max(cur, axis=-1, keepdims=True)
        idx = jnp.min(jnp.where(cur == m, lane, LANES), axis=-1, keepdims=True)
        sel = lane == idx
        hot = hot + sel.astype(F32)
        cur = jnp.where(sel, -jnp.inf, cur)
        vals.append(m)
        idxs.append(idx)
    es = [jnp.exp(v - vals[0]) for v in vals]
    tot = es[0] + es[1] + es[2] + es[3]
    before = _dot(tri_ref[...], hot.astype(BF16)) + carry[...]
    meta = jnp.zeros(logits.shape, F32)
    for k in range(TOP_K):
        rank = jnp.sum(jnp.where(lane == idxs[k], before, 0.0), axis=-1, keepdims=True)
        meta = jnp.where(lane == k, idxs[k].astype(F32), meta)
        meta = jnp.where(lane == TOP_K + k, es[k] / tot, meta)
        meta = jnp.where(lane == 2 * TOP_K + k, rank, meta)
    meta_ref[...] = meta
    carry[...] = carry[...] + jnp.sum(hot, axis=0, keepdims=True)
    cnt_ref[...] = carry[...]


def _route(x, mod3, layer, row_fn, rw_pad, rb_pad, counts_in):
    b, l, d = x.shape
    tm = min(TOKEN_TILE, l)
    idx = jnp.arange(tm)
    tri = (idx[:, None] > idx[None, :]).astype(BF16)
    mspec = lambda which: pl.BlockSpec((1, 1, d), _mod_spec(layer, row_fn, which))
    const = lambda shape: pl.BlockSpec(shape, lambda bb, t: (0,) * len(shape))
    return pl.pallas_call(
        _route_kernel,
        grid=(b, l // tm),
        in_specs=[pl.BlockSpec((None, tm, d), lambda bb, t: (bb, t, 0)),
                  mspec(MOD_SH2), mspec(MOD_SC2),
                  const((d, LANES)), const((1, LANES)), const((1, LANES)), const((tm, tm))],
        out_specs=[pl.BlockSpec((None, tm, d // 2), lambda bb, t: (bb, t, 0)),
                   pl.BlockSpec((None, tm, LANES), lambda bb, t: (bb, t, 0)),
                   const((1, LANES))],
        out_shape=[jax.ShapeDtypeStruct((b, l, d // 2), U32),
                   jax.ShapeDtypeStruct((b, l, LANES), F32),
                   jax.ShapeDtypeStruct((1, LANES), F32)],
        scratch_shapes=[pltpu.VMEM((1, LANES), F32)],
        compiler_params=_cparams(("arbitrary", "arbitrary")),
        name="moe_route",
    )(x, mod3, mod3, rw_pad, rb_pad, counts_in, tri)


def _dispatch_kernel(dest_hbm, hp_hbm, xs_in, xs_out, dest_smem, sem_i, sem, *, tt):
    del xs_in
    i = pl.program_id(0)
    cp = pltpu.make_async_copy(dest_hbm.at[i], dest_smem, sem_i)
    cp.start()
    cp.wait()
    base = i * tt

    def row_copy(r, k):
        return pltpu.make_async_copy(hp_hbm.at[pl.ds(base + r, 1)], xs_out.at[pl.ds(dest_smem[r * TOP_K + k], 1)], sem)

    def issue(r, c):
        for k in range(TOP_K):
            row_copy(r, k).start()
        return c

    lax.fori_loop(0, tt, issue, 0)

    def drain(r, c):
        for k in range(TOP_K):
            row_copy(r, k).wait()
        return c

    lax.fori_loop(0, tt, drain, 0)


def _dispatch(dest, hp, xs):
    n, w = hp.shape
    tt = min(TOKEN_TILE, n)
    return pl.pallas_call(
        functools.partial(_dispatch_kernel, tt=tt),
        grid=(n // tt,),
        in_specs=[pl.BlockSpec(memory_space=pl.ANY)] * 3,
        out_specs=pl.BlockSpec(memory_space=pl.ANY),
        out_shape=jax.ShapeDtypeStruct(xs.shape, xs.dtype),
        scratch_shapes=[pltpu.SMEM((tt * TOP_K,), I32), pltpu.SemaphoreType.DMA, pltpu.SemaphoreType.DMA],
        input_output_aliases={2: 0},
        compiler_params=_cparams(("arbitrary",)),
        name="moe_dispatch",
    )(dest.reshape(n // tt, tt * TOP_K), hp, xs)


def _expert_kernel(be_ref, nact_ref, xs_ref, wgu_ref, bgu_ref, wd_ref, bd_ref, y_ref, wgu_bf, wd_bf):
    i = pl.program_id(0)
    e = be_ref[i]
    prev = be_ref[jnp.maximum(i - 1, 0)]

    @pl.when((i == 0) | (e != prev))
    def _():
        wgu_bf[...] = wgu_ref[0].astype(BF16)
        wd_bf[...] = wd_ref[0].astype(BF16)

    @pl.when(i < nact_ref[0])
    def _():
        w = xs_ref[...]
        lo = pltpu.bitcast(w << 16, F32)
        hi = pltpu.bitcast(w & jnp.uint32(0xFFFF0000), F32)
        x = jnp.concatenate([lo, hi], axis=1).astype(BF16)
        gu = _dot(x, wgu_bf[...]) + bgu_ref[0]
        de = gu.shape[1] // 2
        gate = jnp.minimum(gu[:, :de], SWIGLU_LIMIT)
        up = jnp.clip(gu[:, de:], -SWIGLU_LIMIT, SWIGLU_LIMIT)
        act = (up + 1.0) * gate * jax.nn.sigmoid(SWIGLU_ALPHA * gate)
        y_ref[...] = _dot(act.astype(BF16), wd_bf[...]) + bd_ref[0]

    @pl.when(i >= nact_ref[0])
    def _():
        y_ref[...] = jnp.zeros(y_ref.shape, y_ref.dtype)


def _experts(block_e, n_active, xs, w_gu, b_gu, w_d, b_d):
    n_rows, hw = xs.shape
    ne, d, de2 = w_gu.shape
    nblk = n_rows // MOE_ROWS
    grid_spec = pltpu.PrefetchScalarGridSpec(
        num_scalar_prefetch=2,
        grid=(nblk,),
        in_specs=[pl.BlockSpec((MOE_ROWS, hw), lambda i, be, na: (i, 0)),
                  pl.BlockSpec((1, d, de2), lambda i, be, na: (be[i], 0, 0)),
                  pl.BlockSpec((1, 1, de2), lambda i, be, na: (be[i], 0, 0)),
                  pl.BlockSpec((1, de2 // 2, d), lambda i, be, na: (be[i], 0, 0)),
                  pl.BlockSpec((1, 1, d), lambda i, be, na: (be[i], 0, 0))],
        out_specs=pl.BlockSpec((MOE_ROWS, d), lambda i, be, na: (i, 0)),
        scratch_shapes=[pltpu.VMEM((d, de2), BF16), pltpu.VMEM((de2 // 2, d), BF16)],
    )
    return pl.pallas_call(
        _expert_kernel,
        grid_spec=grid_spec,
        out_shape=jax.ShapeDtypeStruct((n_rows, d), F32),
        compiler_params=_cparams(("arbitrary",)),
        name="moe_experts",
    )(block_e, n_active, xs, w_gu, b_gu.reshape(ne, 1, de2), w_d, b_d.reshape(ne, 1, d))


def _combine_kernel(*refs, tt, final):
    if final:
        dest_hbm, yb_hbm, x_ref, meta_ref, g2_ref, fw_ref, o_ref, dest_smem, buf, sem_i, sem = refs
    else:
        dest_hbm, yb_hbm, x_ref, meta_ref, g2_ref, o_ref, dest_smem, buf, sem_i, sem = refs
    i = pl.program_id(0) * pl.num_programs(1) + pl.program_id(1)
    cp = pltpu.make_async_copy(dest_hbm.at[i], dest_smem, sem_i)
    cp.start()
    cp.wait()

    def row_copy(r, k):
        return pltpu.make_async_copy(yb_hbm.at[pl.ds(dest_smem[r * TOP_K + k], 1)], buf.at[k, pl.ds(r, 1)], sem)

    def issue(r, c):
        for k in range(TOP_K):
            row_copy(r, k).start()
        return c

    lax.fori_loop(0, tt, issue, 0)

    def drain(r, c):
        for k in range(TOP_K):
            row_copy(r, k).wait()
        return c

    lax.fori_loop(0, tt, drain, 0)
    meta = meta_ref[...]
    y = meta[:, TOP_K:TOP_K + 1] * buf[0]
    for k in range(1, TOP_K):
        y = y + meta[:, TOP_K + k:TOP_K + k + 1] * buf[k]
    xn = x_ref[...] + g2_ref[0] * y
    if final:
        xn = _rms(xn) * fw_ref[...]
    o_ref[...] = xn


def _combine(dest, yb, x, meta, mod3, layer, row_fn, final_w=None):
    b, l, d = x.shape
    tt = min(TOKEN_TILE, l)
    nt = l // tt
    final = final_w is not None
    in_specs = [pl.BlockSpec(memory_space=pl.ANY), pl.BlockSpec(memory_space=pl.ANY),
                pl.BlockSpec((None, tt, d), lambda bb, t: (bb, t, 0)),
                pl.BlockSpec((None, tt, LANES), lambda bb, t: (bb, t, 0)),
                pl.BlockSpec((1, 1, d), _mod_spec(layer, row_fn, MOD_G2))]
    args = [dest.reshape(b * nt, tt * TOP_K), yb, x, meta, mod3]
    if final:
        in_specs.append(pl.BlockSpec((1, d), lambda bb, t: (0, 0)))
        args.append(final_w.astype(F32).reshape(1, d))
    return pl.pallas_call(
        functools.partial(_combine_kernel, tt=tt, final=final),
        grid=(b, nt),
        in_specs=in_specs,
        out_specs=pl.BlockSpec((None, tt, d), lambda bb, t: (bb, t, 0)),
        out_shape=jax.ShapeDtypeStruct((b, l, d), F32),
        scratch_shapes=[pltpu.SMEM((tt * TOP_K,), I32), pltpu.VMEM((TOP_K, tt, d), F32),
                        pltpu.SemaphoreType.DMA, pltpu.SemaphoreType.DMA],
        compiler_params=_cparams(("arbitrary", "arbitrary")),
        name="moe_combine",
    )(*args)


def _moe(parts, mod3, layer, router_w, router_b, w_gu, b_gu, w_d, b_d, final_w=None):
    d = router_w.shape[0]
    rw_pad = jnp.zeros((d, LANES), F32).at[:, :N_EXPERTS].set(router_w.astype(F32))
    rb_pad = jnp.zeros((1, LANES), F32).at[0, :N_EXPERTS].set(router_b.astype(F32))
    counts = jnp.zeros((1, LANES), F32)
    routed = []
    for x, row_fn in parts:
        hp, meta, counts = _route(x, mod3, layer, row_fn, rw_pad, rb_pad, counts)
        routed.append((hp, meta))
    n_assign = sum(x.shape[0] * x.shape[1] for x, _ in parts) * TOP_K
    nblk = -(-(n_assign + N_EXPERTS * (MOE_ROWS - 1)) // MOE_ROWS)
    cnt = counts[0, :N_EXPERTS].astype(I32)
    padded = (cnt + MOE_ROWS - 1) // MOE_ROWS * MOE_ROWS
    pad_end = jnp.cumsum(padded)
    pad_start = pad_end - padded
    block_e = jnp.minimum(jnp.searchsorted(pad_end, jnp.arange(nblk, dtype=I32) * MOE_ROWS, side='right'),
                          N_EXPERTS - 1).astype(I32)
    n_active = (pad_end[-1] // MOE_ROWS).astype(I32).reshape(1)
    xs = jnp.zeros((nblk * MOE_ROWS, d // 2), U32)
    dests = []
    for (x, _), (hp, meta) in zip(parts, routed):
        e_idx = meta[..., :TOP_K].astype(I32)
        rank = meta[..., 2 * TOP_K:3 * TOP_K].astype(I32)
        dest = (pad_start[e_idx] + rank).reshape(-1)
        dests.append(dest)
        xs = _dispatch(dest, hp.reshape(-1, d // 2), xs)
    yb = _experts(block_e, n_active, xs, w_gu, b_gu, w_d, b_d)
    outs = []
    for idx, ((x, row_fn), (hp, meta)) in enumerate(zip(parts, routed)):
        fw = final_w if idx == 0 else None
        outs.append(_combine(dests[idx], yb, x, meta, mod3, layer, row_fn, fw))
    return outs


def kernel(x, c, ctx, c_ctx, ada_w, ada_b, ab_w_in, ab_q_gain, ab_k_gain, dn_conv_w, dn_a_log, dn_dt_bias, dn_norm_w, ab_w_out, s5_lambda_re, s5_lambda_im, s5_log_step, s5_b_re, s5_b_im, s5_c_re, s5_c_im, s5_d, s5_glu_w1, s5_glu_b1, s5_glu_w2, s5_glu_b2, moe_router_w, moe_router_b, moe_w_gate_up, moe_b_gate_up, moe_w_down, moe_b_down, final_norm_w):
    b, l, d = x.shape
    lc = ctx.shape[1]
    depth = ada_w.shape[0]
    assert depth == 2 and b < MOD_ROWS
    lat_row = lambda bb: bb
    ctx_row = lambda bb: b

    rows = jnp.zeros((MOD_ROWS, d), F32).at[:b].set(c.astype(F32)).at[b].set(c_ctx.astype(F32))
    mod3 = _ada(rows, ada_w, ada_b).reshape(depth * MOD_ROWS * 6, 1, d)

    w_in = ab_w_in[0]
    aq, akv, bqk = Q_HEADS * HEAD_DIM, KV_HEADS * HEAD_DIM, DN_HEADS * DN_DK
    o_k, o_v, o_dn, o_ab, o_z = aq, aq + akv, aq + 2 * akv, aq + 2 * akv + 3 * bqk, aq + 2 * akv + 3 * bqk + 4 * DN_HEADS
    w_perm = jnp.concatenate([w_in[:, :aq], w_in[:, o_dn:o_ab], w_in[:, o_z:], w_in[:, o_k:o_v], w_in[:, o_v:o_dn],
                              w_in[:, o_ab:o_z], jnp.zeros((d, IN_COLS_PAD - w_in.shape[1]), w_in.dtype)], axis=1).astype(BF16)
    proj_l = _inproj(x, mod3, 0, lat_row, w_perm)
    proj_c = _inproj(ctx, mod3, 0, ctx_row, w_perm)

    cos, sin = _rope_tables(l)
    ones = jnp.ones((lc, KV_HEADS * HEAD_DIM), F32)
    q_l, kt_l, v_l = _qkprep(proj_l, ab_q_gain[0], ab_k_gain[0], cos, sin, min(ATTN_K_TILE, l))
    q_c, kt_c, v_c = _qkprep(proj_c, ab_q_gain[0], ab_k_gain[0], ones, jnp.zeros_like(ones), lc)
    a_l = _attention(q_l, kt_c, v_c, kt_l, v_l)
    a_c = _attention(q_c, kt_c, v_c)

    dq_c, dk_c, dkt_c, dv_c, gc_c, gct_c = _dnprep(proj_c, dn_conv_w[0], dn_a_log[0], dn_dt_bias[0])
    dq_l, dk_l, dkt_l, dv_l, gc_l, gct_l = _dnprep(proj_l, dn_conv_w[0], dn_a_log[0], dn_dt_bias[0])
    s_zero = jnp.zeros((b, 2, DN_HEADS, DN_DK, DN_DK), F32)
    of_c, ob_c, s_ctx = _dnscan(dq_c, dk_c, dkt_c, dv_c, gc_c, gct_c, s_zero)
    of_l, ob_l, _ = _dnscan(dq_l, dk_l, dkt_l, dv_l, gc_l, gct_l, s_ctx)

    w_out = ab_w_out[0].astype(BF16)
    x = _mix0_out(x, mod3, 0, lat_row, a_l, of_l, ob_l, proj_l, dn_norm_w[0], w_out)
    ctx = _mix0_out(ctx, mod3, 0, ctx_row, a_c, of_c, ob_c, proj_c, dn_norm_w[0], w_out)

    x, ctx = _moe([(x, lat_row), (ctx, ctx_row)], mod3, 0, moe_router_w[0], moe_router_b[0],
                  moe_w_gate_up[0], moe_b_gate_up[0], moe_w_down[0], moe_b_down[0])

    wb, wc, lam_t = _s5_weights(s5_lambda_re[0], s5_lambda_im[0], s5_log_step[0], s5_b_re[0], s5_b_im[0],
                                s5_c_re[0], s5_c_im[0])
    sw = lam_t.shape[-1]
    x0 = jnp.zeros((b, 2, 2, SUBLANES, sw), F32)
    _, x_ctx = _s5_scan(ctx, mod3, 1, ctx_row, wb, wc, lam_t, x0)
    y2, _ = _s5_scan(x, mod3, 1, lat_row, wb, wc, lam_t, x_ctx)
    x = _s5_glu(x, mod3, 1, lat_row, y2, s5_d[0], s5_glu_w1[0], s5_glu_b1[0], s5_glu_w2[0], s5_glu_b2[0])

    (x,) = _moe([(x, lat_row)], mod3, 1, moe_router_w[1], moe_router_b[1], moe_w_gate_up[1], moe_b_gate_up[1],
                moe_w_down[1], moe_b_down[1], final_w=final_norm_w)
    return x
```

```python
import functools
import math

import jax
import jax.numpy as jnp
from jax import lax
from jax.experimental import pallas as pl
from jax.experimental.pallas import tpu as pltpu

F32 = jnp.float32
BF16 = jnp.bfloat16
U32 = jnp.uint32
I32 = jnp.int32

EPS = 1e-6
GRID_W = 64
HEAD_DIM = 64
Q_HEADS = 8
KV_HEADS = 2
GROUP = Q_HEADS // KV_HEADS
ROPE_THETA = 10000.0
DN_HEADS = 8
DN_DK = 64
DN_CONV = 5
DN_CHUNK = 64
S5_GROUP = 16
S5_STATE = 64
N_EXPERTS = 32
TOP_K = 4
SWIGLU_LIMIT = 7.0
SWIGLU_ALPHA = 1.702

LANES = 128
SUBLANES = 8
TOKEN_TILE = 256
ATTN_Q_TILE = 256
ATTN_K_TILE = 1024
DN_BLOCK_CHUNKS = 4
MOE_ROWS = 512
S5_PITCH_PAD = 4
VMEM_LIMIT = 56 * 1024 * 1024

MOD_SH1, MOD_SC1, MOD_G1, MOD_SH2, MOD_SC2, MOD_G2 = range(6)
MOD_ROWS = 8

COL_Q = 0
COL_DNQ = 512
COL_DNK = 1024
COL_DNV = 1536
COL_Z = 2048
COL_K = 2560
COL_V = 2688
COL_AB = 2816
IN_COLS_PAD = 2944


def _cparams(sem):
    return pltpu.CompilerParams(dimension_semantics=sem, vmem_limit_bytes=VMEM_LIMIT)


def _split3(a):
    a1 = a.astype(BF16)
    r = a - a1.astype(F32)
    a2 = r.astype(BF16)
    a3 = (r - a2.astype(F32)).astype(BF16)
    return a1, a2, a3


def _dot(a, b):
    return jnp.dot(a, b, preferred_element_type=F32)


def _dot_exact_lhs(m_bf16, x):
    x1, x2, x3 = _split3(x)
    return _dot(m_bf16, x1) + _dot(m_bf16, x2) + _dot(m_bf16, x3)


def _dot_exact_rhs(x, m_bf16):
    x1, x2, x3 = _split3(x)
    return _dot(x1, m_bf16) + _dot(x2, m_bf16) + _dot(x3, m_bf16)


def _dot_hi(a, b):
    a1, a2, _ = _split3(a)
    b1, b2, _ = _split3(b)
    return _dot(a1, b1) + _dot(a1, b2) + _dot(a2, b1)


def _rms(x):
    return x * lax.rsqrt(jnp.mean(x * x, axis=-1, keepdims=True) + EPS)


def _modulated(x_ref, sh_ref, sc_ref):
    return _rms(x_ref[...]) * (1.0 + sc_ref[0]) + sh_ref[0]


def _mod_spec(layer, row_fn, which):
    d = None

    def imap(b, t):
        return ((layer * MOD_ROWS + row_fn(b)) * 6 + which, 0, 0)

    return imap


def _ada_kernel(a_ref, w_ref, b_ref, o_ref):
    a = a_ref[...]
    a = a * jax.nn.sigmoid(a)
    o_ref[0] = _dot_hi(a, w_ref[0]) + b_ref[0]


def _ada(rows, ada_w, ada_b):
    depth, d, n = ada_w.shape
    tn = 1536
    return pl.pallas_call(
        _ada_kernel,
        grid=(depth, n // tn),
        in_specs=[pl.BlockSpec((MOD_ROWS, d), lambda l, j: (0, 0)),
                  pl.BlockSpec((1, d, tn), lambda l, j: (l, 0, j)),
                  pl.BlockSpec((1, 1, tn), lambda l, j: (l, 0, j))],
        out_specs=pl.BlockSpec((1, MOD_ROWS, tn), lambda l, j: (l, 0, j)),
        out_shape=jax.ShapeDtypeStruct((depth, MOD_ROWS, n), F32),
        compiler_params=_cparams(("arbitrary", "arbitrary")),
        name="ada_mod",
    )(rows, ada_w, ada_b.reshape(depth, 1, n))


def _inproj_kernel(x_ref, sh_ref, sc_ref, w_ref, o_ref):
    h = _modulated(x_ref, sh_ref, sc_ref)
    o_ref[...] = _dot(h.astype(BF16), w_ref[...])


def _inproj(x, mod3, layer, row_fn, w_bf16):
    b, l, d = x.shape
    tm = min(TOKEN_TILE, l)
    n = w_bf16.shape[1]
    mspec = lambda which: pl.BlockSpec((1, 1, d), _mod_spec(layer, row_fn, which))
    return pl.pallas_call(
        _inproj_kernel,
        grid=(b, l // tm),
        in_specs=[pl.BlockSpec((None, tm, d), lambda bb, t: (bb, t, 0)),
                  mspec(MOD_SH1), mspec(MOD_SC1),
                  pl.BlockSpec((d, n), lambda bb, t: (0, 0))],
        out_specs=pl.BlockSpec((None, tm, n), lambda bb, t: (bb, t, 0)),
        out_shape=jax.ShapeDtypeStruct((b, l, n), F32),
        compiler_params=_cparams(("arbitrary", "arbitrary")),
        name="in_proj",
    )(x, mod3, mod3, w_bf16)


def _head_rot(x, width):
    lane = lax.broadcasted_iota(I32, x.shape, 1)
    first = (lane % HEAD_DIM) < (HEAD_DIM // 2)
    return jnp.where(first, pltpu.roll(x, width - HEAD_DIM // 2, 1), pltpu.roll(x, HEAD_DIM // 2, 1))


def _qkprep_kernel(q_ref, k_ref, v_ref, qg_ref, kg_ref, cos_ref, sin_ref, gq_ref, gk_ref, qo_ref, kt_ref, vo_ref):
    cos = cos_ref[...]
    sin = sin_ref[...]
    q = q_ref[...]
    qn = q * lax.rsqrt(_dot_exact_rhs(q * q, gq_ref[...]) + EPS) * qg_ref[...]
    cos4 = jnp.concatenate([cos] * 4, axis=1)
    sin4 = jnp.concatenate([sin] * 4, axis=1)
    qr = qn * cos4 + _head_rot(qn, Q_HEADS * HEAD_DIM) * sin4
    qo_ref[...] = (qr * (HEAD_DIM ** -0.5 * math.log2(math.e))).astype(BF16)
    k = k_ref[...]
    kn = k * lax.rsqrt(_dot_exact_rhs(k * k, gk_ref[...]) + EPS) * kg_ref[...]
    kr = kn * cos + _head_rot(kn, KV_HEADS * HEAD_DIM) * sin
    kt = kr.T
    kt_ref[0] = kt[:HEAD_DIM].astype(BF16)
    kt_ref[1] = kt[HEAD_DIM:].astype(BF16)
    v = v_ref[...]
    vo_ref[0] = v[:, :HEAD_DIM].astype(BF16)
    vo_ref[1] = v[:, HEAD_DIM:].astype(BF16)


def _group_matrix(width, scale):
    g = jnp.arange(width) // HEAD_DIM
    return ((g[:, None] == g[None, :]).astype(F32) * scale).astype(BF16)


def _qkprep(proj, q_gain, k_gain, cos, sin, tk):
    b, l, _ = proj.shape
    qw = Q_HEADS * HEAD_DIM
    kw = KV_HEADS * HEAD_DIM
    qg = jnp.tile(q_gain.astype(F32), Q_HEADS).reshape(1, qw)
    kg = jnp.tile(k_gain.astype(F32), KV_HEADS).reshape(1, kw)
    const = lambda shape: pl.BlockSpec(shape, lambda bb, t: (0,) * len(shape))
    return pl.pallas_call(
        _qkprep_kernel,
        grid=(b, l // tk),
        in_specs=[pl.BlockSpec((None, tk, qw), lambda bb, t: (bb, t, COL_Q // qw)),
                  pl.BlockSpec((None, tk, kw), lambda bb, t: (bb, t, COL_K // kw)),
                  pl.BlockSpec((None, tk, kw), lambda bb, t: (bb, t, COL_V // kw)),
                  const((1, qw)), const((1, kw)),
                  pl.BlockSpec((tk, kw), lambda bb, t: (t, 0)),
                  pl.BlockSpec((tk, kw), lambda bb, t: (t, 0)),
                  const((qw, qw)), const((kw, kw))],
        out_specs=[pl.BlockSpec((None, tk, qw), lambda bb, t: (bb, t, 0)),
                   pl.BlockSpec((None, KV_HEADS, None, HEAD_DIM, tk), lambda bb, t: (bb, 0, t, 0, 0)),
                   pl.BlockSpec((None, KV_HEADS, tk, HEAD_DIM), lambda bb, t: (bb, 0, t, 0))],
        out_shape=[jax.ShapeDtypeStruct((b, l, qw), BF16),
                   jax.ShapeDtypeStruct((b, KV_HEADS, l // tk, HEAD_DIM, tk), BF16),
                   jax.ShapeDtypeStruct((b, KV_HEADS, l, HEAD_DIM), BF16)],
        compiler_params=_cparams(("arbitrary", "arbitrary")),
        name="qk_prep",
    )(proj, proj, proj, qg, kg, cos, sin, _group_matrix(qw, 1.0 / HEAD_DIM), _group_matrix(kw, 1.0 / HEAD_DIM))


def _rope_tables(length):
    rows = length // GRID_W
    row = jnp.broadcast_to(jnp.arange(rows, dtype=I32)[:, None], (rows, GRID_W)).reshape(-1).astype(F32)
    col = jnp.broadcast_to(jnp.arange(GRID_W, dtype=I32)[None, :], (rows, GRID_W)).reshape(-1).astype(F32)
    n_axis = HEAD_DIM // 4
    inv_freq = ROPE_THETA ** (-jnp.arange(n_axis, dtype=F32) / n_axis)
    ang = jnp.concatenate([row[:, None] * inv_freq, col[:, None] * inv_freq], axis=-1)
    c, s = jnp.cos(ang), jnp.sin(ang)
    cos_h = jnp.concatenate([c, c], axis=-1)
    sin_h = jnp.concatenate([-s, s], axis=-1)
    return jnp.tile(cos_h, (1, KV_HEADS)), jnp.tile(sin_h, (1, KV_HEADS))


def _attn_kernel(*refs, tq, n_lat_chunks, has_lat):
    if has_lat:
        q_ref, ktc_ref, vc_ref, ktl_ref, vl_ref, o_ref = refs
    else:
        q_ref, ktc_ref, vc_ref, o_ref = refs
    q = q_ref[...]
    qs = jnp.concatenate([q[:, g * HEAD_DIM:(g + 1) * HEAD_DIM] for g in range(GROUP)], axis=0)
    rows = GROUP * tq

    def step(kt, v, m, l, acc):
        s = _dot(qs, kt)
        m_new = jnp.maximum(m, jnp.max(s, axis=-1, keepdims=True))
        alpha = jnp.exp2(m - m_new)
        p = jnp.exp2(s - m_new)
        l = alpha * l + jnp.sum(p, axis=-1, keepdims=True)
        acc = alpha * acc + _dot(p.astype(BF16), v)
        return m_new, l, acc

    m0 = jnp.full((rows, 1), -1e30, F32)
    l0 = jnp.zeros((rows, 1), F32)
    a0 = jnp.zeros((rows, HEAD_DIM), F32)
    carry = step(ktc_ref[0], vc_ref[...], m0, l0, a0)
    if has_lat:
        tk = ktl_ref.shape[-1]

        def body(j, c):
            off = pl.multiple_of(j * tk, tk)
            return step(ktl_ref[j], vl_ref[pl.ds(off, tk), :], *c)

        carry = lax.fori_loop(0, n_lat_chunks, body, carry, unroll=2)
    _, l, acc = carry
    o = acc / l
    o_ref[...] = jnp.concatenate([o[g * tq:(g + 1) * tq] for g in range(GROUP)], axis=1).astype(o_ref.dtype)


def _attention(q, ktc, vc, ktl=None, vl=None):
    b, lq, qw = q.shape
    tq = min(ATTN_Q_TILE, lq)
    gw = GROUP * HEAD_DIM
    lc = vc.shape[2]
    has_lat = ktl is not None
    in_specs = [pl.BlockSpec((None, tq, gw), lambda bb, h, i: (bb, i, h)),
                pl.BlockSpec((None, None, 1, HEAD_DIM, lc), lambda bb, h, i: (bb, h, 0, 0, 0)),
                pl.BlockSpec((None, None, lc, HEAD_DIM), lambda bb, h, i: (bb, h, 0, 0))]
    args = [q, ktc, vc]
    n_chunks = 0
    if has_lat:
        n_chunks, tk = ktl.shape[2], ktl.shape[4]
        ll = vl.shape[2]
        in_specs += [pl.BlockSpec((None, None, n_chunks, HEAD_DIM, tk), lambda bb, h, i: (bb, h, 0, 0, 0)),
                     pl.BlockSpec((None, None, ll, HEAD_DIM), lambda bb, h, i: (bb, h, 0, 0))]
        args += [ktl, vl]
    return pl.pallas_call(
        functools.partial(_attn_kernel, tq=tq, n_lat_chunks=n_chunks, has_lat=has_lat),
        grid=(b, KV_HEADS, lq // tq),
        in_specs=in_specs,
        out_specs=pl.BlockSpec((None, tq, gw), lambda bb, h, i: (bb, i, h)),
        out_shape=jax.ShapeDtypeStruct((b, lq, qw), BF16),
        compiler_params=_cparams(("arbitrary", "arbitrary", "arbitrary")),
        name="gqa_attention",
    )(*args)


def _dnprep_kernel(qp, qc, qn, kp, kc, kn, vp, vc, vn, ab_ref, cw_ref, alog_ref, dtb_ref, gm_ref, trip_ref, tris_ref,
                   qo_ref, ko_ref, kto_ref, vo_ref, gc_ref, gct_ref, *, tm):
    t = pl.program_id(1)
    nt = pl.num_programs(1)
    mp = (t > 0).astype(F32)
    mn = (t < nt - 1).astype(F32)
    n_ext = tm + 2 * SUBLANES
    pad = DN_CONV // 2

    def conv(p_ref, c_ref, n_ref, w):
        ext = jnp.concatenate([p_ref[...] * mp, c_ref[...], n_ref[...] * mn], axis=0)
        acc = None
        for j in range(DN_CONV):
            s = (pad - j) % n_ext
            r = ext if s == 0 else pltpu.roll(ext, s, 0)
            term = r[SUBLANES:SUBLANES + tm] * w[j:j + 1, :]
            acc = term if acc is None else acc + term
        return acc * jax.nn.sigmoid(acc)

    cw = cw_ref[...]
    hw = DN_HEADS * DN_DK
    gm = gm_ref[...]
    q = conv(qp, qc, qn, cw[:, 0:hw])
    q = q * lax.rsqrt(_dot_exact_rhs(q * q, gm) + EPS) * (DN_DK ** -0.5)
    k = conv(kp, kc, kn, cw[:, hw:2 * hw])
    k = k * lax.rsqrt(_dot_exact_rhs(k * k, gm) + EPS)
    v = conv(vp, vc, vn, cw[:, 2 * hw:3 * hw])
    kt = k.T
    for h in range(DN_HEADS):
        sl = slice(h * DN_DK, (h + 1) * DN_DK)
        qo_ref[h] = q[:, sl]
        ko_ref[h] = k[:, sl]
        vo_ref[h] = v[:, sl]
        for c in range(tm // DN_CHUNK):
            kto_ref[h, c] = kt[sl, c * DN_CHUNK:(c + 1) * DN_CHUNK]

    ab = ab_ref[...]
    lane = lax.broadcasted_iota(I32, ab.shape, 1)
    a = ab + dtb_ref[...]
    sp = jnp.maximum(a, 0.0) + jnp.log(1.0 + jnp.exp(-jnp.abs(a)))
    g = -jnp.exp(alog_ref[...]) * sp
    g = jnp.where(lane < 2 * DN_HEADS, g, 0.0)
    gpre = _dot_exact_lhs(trip_ref[...], g)
    gsuf = _dot_exact_lhs(tris_ref[...], g)
    gc = jnp.where(lane < DN_HEADS, gpre, jnp.where(lane < 2 * DN_HEADS, gsuf, jax.nn.sigmoid(ab)))
    gc_ref[...] = gc
    gt = gc.T
    for c in range(tm // DN_CHUNK):
        gct_ref[c] = gt[:4 * DN_HEADS, c * DN_CHUNK:(c + 1) * DN_CHUNK]


def _dnprep(proj, conv_w, a_log, dt_bias):
    b, l, _ = proj.shape
    tm = min(TOKEN_TILE, l)
    hw = DN_HEADS * DN_DK
    nc = l // DN_CHUNK
    r8 = tm // SUBLANES
    last8 = l // SUBLANES - 1

    def cur(col):
        return pl.BlockSpec((None, tm, hw), lambda bb, t: (bb, t, col // hw))

    def prev(col):
        return pl.BlockSpec((None, SUBLANES, hw), lambda bb, t: (bb, jnp.maximum(t * r8 - 1, 0), col // hw))

    def nxt(col):
        return pl.BlockSpec((None, SUBLANES, hw), lambda bb, t: (bb, jnp.minimum((t + 1) * r8, last8), col // hw))

    const = lambda shape: pl.BlockSpec(shape, lambda bb, t: (0,) * len(shape))
    alog = jnp.zeros((1, LANES), F32).at[0, :2 * DN_HEADS].set(a_log.astype(F32).reshape(-1))
    dtb = jnp.zeros((1, LANES), F32).at[0, :2 * DN_HEADS].set(dt_bias.astype(F32).reshape(-1))
    ch = jnp.arange(tm) // DN_CHUNK
    same = ch[:, None] == ch[None, :]
    idx = jnp.arange(tm)
    trip = (same & (idx[:, None] >= idx[None, :])).astype(BF16)
    tris = (same & (idx[:, None] <= idx[None, :])).astype(BF16)
    gmat = _group_matrix(hw, 1.0)
    in_specs = []
    args = []
    for col in (COL_DNQ, COL_DNK, COL_DNV):
        in_specs += [prev(col), cur(col), nxt(col)]
        args += [proj, proj, proj]
    in_specs += [pl.BlockSpec((None, tm, LANES), lambda bb, t: (bb, t, COL_AB // LANES)),
                 const((DN_CONV, 3 * hw)), const((1, LANES)), const((1, LANES)), const((hw, hw)),
                 const((tm, tm)), const((tm, tm))]
    args += [proj, conv_w.astype(F32), alog, dtb, gmat, trip, tris]
    head_spec = pl.BlockSpec((None, DN_HEADS, tm, DN_DK), lambda bb, t: (bb, 0, t, 0))
    head_shape = jax.ShapeDtypeStruct((b, DN_HEADS, l, DN_DK), F32)
    return pl.pallas_call(
        functools.partial(_dnprep_kernel, tm=tm),
        grid=(b, l // tm),
        in_specs=in_specs,
        out_specs=[head_spec, head_spec,
                   pl.BlockSpec((None, DN_HEADS, tm // DN_CHUNK, DN_DK, DN_CHUNK), lambda bb, t: (bb, 0, t, 0, 0)),
                   head_spec,
                   pl.BlockSpec((None, tm, LANES), lambda bb, t: (bb, t, 0)),
                   pl.BlockSpec((None, tm // DN_CHUNK, 4 * DN_HEADS, DN_CHUNK), lambda bb, t: (bb, t, 0, 0))],
        out_shape=[head_shape, head_shape,
                   jax.ShapeDtypeStruct((b, DN_HEADS, nc, DN_DK, DN_CHUNK), F32),
                   head_shape,
                   jax.ShapeDtypeStruct((b, l, LANES), F32),
                   jax.ShapeDtypeStruct((b, nc, 4 * DN_HEADS, DN_CHUNK), F32)],
        compiler_params=_cparams(("arbitrary", "arbitrary")),
        name="dn_prep",
    )(*args)


def _bmm(a, b):
    return jnp.einsum('nij,njk->nik', a.astype(BF16), b.astype(BF16), preferred_element_type=F32)


def _unit_tri_inverse(a, eye, blks):
    inner = jnp.where(blks[0], a, 0.0)
    x = eye - inner
    p = inner
    for _ in range(2):
        p = _bmm(p, p)
        x = x + _bmm(x, p)
    for level in range(1, len(blks) + 1):
        outer = jnp.where(blks[level], a, 0.0) if level < len(blks) else a
        x = x - _bmm(x, _bmm(outer - inner, x))
        inner = outer
    return x


def _dn_local(q, k, kt, v, gcol, grow, beta, g_last, incl, strict, eye, blks):
    decay = jnp.where(incl, jnp.exp(jnp.where(incl, gcol - grow, 0.0)), 0.0)
    kb = k * beta
    a = jnp.where(strict, _bmm(kb, kt) * decay, 0.0)
    x = _unit_tri_inverse(a, eye, blks)
    u = _bmm(x, v * beta)
    w = _bmm(x, kb * jnp.exp(gcol))
    qk = _bmm(q, kt) * decay
    return u, w, qk, q * jnp.exp(gcol), kt * jnp.exp(g_last - grow), jnp.exp(g_last)


def _dnscan_kernel(qf, kf, ktf, vf, gcf, gctf, qb, kb_, ktb, vb, gcb, gctb, s0_ref, of_ref, ob_ref, sout_ref, s_scr, *, cb):
    i = pl.program_id(1)
    last = pl.num_programs(1) - 1
    nh = DN_HEADS

    @pl.when(i == 0)
    def _():
        s_scr[...] = s0_ref[...]

    row = lax.broadcasted_iota(I32, (DN_CHUNK, DN_CHUNK), 0)
    col = lax.broadcasted_iota(I32, (DN_CHUNK, DN_CHUNK), 1)
    eye = (row == col).astype(F32)
    blks = tuple((row // n) == (col // n) for n in (8, 16, 32))
    dirs = ((0, qf, kf, ktf, vf, gcf, gctf, row >= col, row > col),
            (1, qb, kb_, ktb, vb, gcb, gctb, row <= col, row < col))
    local = []
    for d, q_ref, k_ref, kt_ref, v_ref, gc_ref, gct_ref, incl, strict in dirs:
        last_row = DN_CHUNK - 1 if d == 0 else 0
        qs, ks, kts, vs, gcols, grows, betas, glasts = [], [], [], [], [], [], [], []
        for c in range(cb):
            rs = slice(c * DN_CHUNK, (c + 1) * DN_CHUNK)
            gc = gc_ref[rs, :]
            gct = gct_ref[c]
            for h in range(nh):
                gl = d * nh + h
                bl = 2 * nh + d * nh + h
                qs.append(q_ref[h, rs, :])
                ks.append(k_ref[h, rs, :])
                kts.append(kt_ref[h, c])
                vs.append(v_ref[h, rs, :])
                gcols.append(gc[:, gl:gl + 1])
                grows.append(gct[gl:gl + 1, :])
                betas.append(gc[:, bl:bl + 1])
                glasts.append(gc[last_row:last_row + 1, gl:gl + 1])
        st = lambda xs: jnp.stack(xs, axis=0)
        local.append(_dn_local(st(qs), st(ks), st(kts), st(vs), st(gcols), st(grows), st(betas), st(glasts),
                               incl, strict, eye, blks))

    s_all = jnp.concatenate([s_scr[0], s_scr[1]], axis=0)
    for jj in range(cb):
        sel = (slice(jj * nh, (jj + 1) * nh), slice((cb - 1 - jj) * nh, (cb - jj) * nh))
        u, w, qk, qg, kdt, el = (jnp.concatenate([local[0][t][sel[0]], local[1][t][sel[1]]], axis=0) for t in range(6))
        v_new = u - _bmm(w, s_all)
        o = _bmm(qg, s_all) + _bmm(qk, v_new)
        s_all = s_all * el + _bmm(kdt, v_new)
        for d, o_ref in ((0, of_ref), (1, ob_ref)):
            c = jj if d == 0 else cb - 1 - jj
            for h in range(nh):
                o_ref[h, c * DN_CHUNK:(c + 1) * DN_CHUNK, :] = o[d * nh + h]
    s_scr[0] = s_all[:nh]
    s_scr[1] = s_all[nh:]

    @pl.when(i == last)
    def _():
        sout_ref[...] = s_scr[...]


def _dnscan(q, k, kt, v, gc, gct, s0):
    b, nh, l, dk = q.shape
    cb = min(DN_BLOCK_CHUNKS, l // DN_CHUNK)
    tb = cb * DN_CHUNK
    nblk = l // tb
    fwd = lambda i: i
    bwd = lambda i: nblk - 1 - i

    def specs(ix):
        head = pl.BlockSpec((None, nh, tb, dk), lambda bb, i: (bb, 0, ix(i), 0))
        return [head, head,
                pl.BlockSpec((None, nh, cb, dk, DN_CHUNK), lambda bb, i: (bb, 0, ix(i), 0, 0)),
                head,
                pl.BlockSpec((None, tb, LANES), lambda bb, i: (bb, ix(i), 0)),
                pl.BlockSpec((None, cb, 4 * DN_HEADS, DN_CHUNK), lambda bb, i: (bb, ix(i), 0, 0))]

    state_spec = pl.BlockSpec((None, 2, nh, dk, dk), lambda bb, i: (bb, 0, 0, 0, 0))
    o_shape = jax.ShapeDtypeStruct((b, nh, l, dk), F32)
    return pl.pallas_call(
        functools.partial(_dnscan_kernel, cb=cb),
        grid=(b, nblk),
        in_specs=specs(fwd) + specs(bwd) + [state_spec],
        out_specs=[pl.BlockSpec((None, nh, tb, dk), lambda bb, i: (bb, 0, fwd(i), 0)),
                   pl.BlockSpec((None, nh, tb, dk), lambda bb, i: (bb, 0, bwd(i), 0)),
                   state_spec],
        out_shape=[o_shape, o_shape, jax.ShapeDtypeStruct((b, 2, nh, dk, dk), F32)],
        scratch_shapes=[pltpu.VMEM((2, nh, dk, dk), F32)],
        compiler_params=_cparams(("arbitrary", "arbitrary")),
        name="dn_scan",
    )(q, k, kt, v, gc, gct, q, k, kt, v, gc, gct, s0)


def _mix0_out_kernel(x_ref, g1_ref, a_ref, of_ref, ob_ref, z_ref, nw_ref, w_ref, o_ref):
    o = of_ref[...] + ob_ref[...]
    on = o * lax.rsqrt(jnp.mean(o * o, axis=-1, keepdims=True) + EPS) * nw_ref[...]
    ot = jnp.concatenate([on[h] for h in range(DN_HEADS)], axis=1)
    z = z_ref[...]
    bmix = ot * (z * jax.nn.sigmoid(z))
    mix = jnp.concatenate([a_ref[...], bmix.astype(BF16)], axis=1)
    o_ref[...] = x_ref[...] + g1_ref[0] * _dot(mix, w_ref[...])


def _mix0_out(x, mod3, layer, row_fn, attn, o_f, o_b, proj, norm_w, w_out_bf16):
    b, l, d = x.shape
    tm = min(TOKEN_TILE, l)
    aw = Q_HEADS * HEAD_DIM
    zw = DN_HEADS * DN_DK
    head = pl.BlockSpec((None, DN_HEADS, tm, DN_DK), lambda bb, t: (bb, 0, t, 0))
    return pl.pallas_call(
        _mix0_out_kernel,
        grid=(b, l // tm),
        in_specs=[pl.BlockSpec((None, tm, d), lambda bb, t: (bb, t, 0)),
                  pl.BlockSpec((1, 1, d), _mod_spec(layer, row_fn, MOD_G1)),
                  pl.BlockSpec((None, tm, aw), lambda bb, t: (bb, t, 0)),
                  head, head,
                  pl.BlockSpec((None, tm, zw), lambda bb, t: (bb, t, COL_Z // zw)),
                  pl.BlockSpec((1, DN_DK), lambda bb, t: (0, 0)),
                  pl.BlockSpec((aw + zw, d), lambda bb, t: (0, 0))],
        out_specs=pl.BlockSpec((None, tm, d), lambda bb, t: (bb, t, 0)),
        out_shape=jax.ShapeDtypeStruct((b, l, d), F32),
        compiler_params=_cparams(("arbitrary", "arbitrary")),
        name="mix0_out",
    )(x, mod3, attn, o_f, o_b, proj, norm_w.astype(F32).reshape(1, DN_DK), w_out_bf16)


def _s5_kernel(x_ref, sh_ref, sc_ref, wb_ref, wc_ref, lam_ref, x0_ref, y_ref, xfin_ref, bre, bim, st, *, tm, pitch):
    d = pl.program_id(1)
    i = pl.program_id(2)
    nsub = SUBLANES
    nj = bre.shape[0]
    h = _modulated(x_ref, sh_ref, sc_ref).astype(BF16)
    half = nj * LANES
    for s in range(nsub):
        o = _dot(h[:, s * LANES:(s + 1) * LANES], wb_ref[s])
        for j in range(nj):
            bre[j, s * pitch:s * pitch + tm, :] = o[:, j * LANES:(j + 1) * LANES]
            bim[j, s * pitch:s * pitch + tm, :] = o[:, half + j * LANES:half + (j + 1) * LANES]

    @pl.when(i == 0)
    def _():
        st[...] = x0_ref[...]

    lam_re = lam_ref[0]
    lam_im = lam_ref[1]

    def body(k, carry):
        xr, xi = carry
        t = jnp.where(d == 0, k, tm - 1 - k)
        br = jnp.concatenate([bre[j, pl.ds(t, nsub, stride=pitch), :] for j in range(nj)], axis=1)
        bi = jnp.concatenate([bim[j, pl.ds(t, nsub, stride=pitch), :] for j in range(nj)], axis=1)
        nr = lam_re * xr - lam_im * xi + br
        ni = lam_re * xi + lam_im * xr + bi
        for j in range(nj):
            bre[j, pl.ds(t, nsub, stride=pitch), :] = nr[:, j * LANES:(j + 1) * LANES]
            bim[j, pl.ds(t, nsub, stride=pitch), :] = ni[:, j * LANES:(j + 1) * LANES]
        return nr, ni

    xr, xi = lax.fori_loop(0, tm, body, (st[0], st[1]), unroll=8)
    st[0] = xr
    st[1] = xi
    xfin_ref[0] = xr
    xfin_ref[1] = xi
    for s in range(nsub):
        parts = [bre[j, s * pitch:s * pitch + tm, :] for j in range(nj)]
        parts += [bim[j, s * pitch:s * pitch + tm, :] for j in range(nj)]
        xs = jnp.concatenate(parts, axis=1).astype(BF16)
        y_ref[:, s * LANES:(s + 1) * LANES] = _dot(xs, wc_ref[s])


def _s5_scan(x, mod3, layer, row_fn, wb, wc, lam, x0):
    b, l, d = x.shape
    tm = min(TOKEN_TILE, l)
    nt = l // tm
    pitch = tm + S5_PITCH_PAD
    sw = wb.shape[-1] // 2
    nj = sw // LANES
    tile = lambda dd, i: jnp.where(dd == 0, i, nt - 1 - i)
    mspec = lambda which: pl.BlockSpec((1, 1, d), lambda bb, dd, i: _mod_spec(layer, row_fn, which)(bb, i))
    return pl.pallas_call(
        functools.partial(_s5_kernel, tm=tm, pitch=pitch),
        grid=(b, 2, nt),
        in_specs=[pl.BlockSpec((None, tm, d), lambda bb, dd, i: (bb, tile(dd, i), 0)),
                  mspec(MOD_SH1), mspec(MOD_SC1),
                  pl.BlockSpec((None, SUBLANES, LANES, 2 * sw), lambda bb, dd, i: (dd, 0, 0, 0)),
                  pl.BlockSpec((None, SUBLANES, 2 * sw, LANES), lambda bb, dd, i: (dd, 0, 0, 0)),
                  pl.BlockSpec((None, 2, SUBLANES, sw), lambda bb, dd, i: (dd, 0, 0, 0)),
                  pl.BlockSpec((None, None, 2, SUBLANES, sw), lambda bb, dd, i: (bb, dd, 0, 0, 0))],
        out_specs=[pl.BlockSpec((None, None, tm, d), lambda bb, dd, i: (dd, bb, tile(dd, i), 0)),
                   pl.BlockSpec((None, None, 2, SUBLANES, sw), lambda bb, dd, i: (bb, dd, 0, 0, 0))],
        out_shape=[jax.ShapeDtypeStruct((2, b, l, d), F32),
                   jax.ShapeDtypeStruct((b, 2, 2, SUBLANES, sw), F32)],
        scratch_shapes=[pltpu.VMEM((nj, SUBLANES * pitch, LANES), F32),
                        pltpu.VMEM((nj, SUBLANES * pitch, LANES), F32),
                        pltpu.VMEM((2, SUBLANES, sw), F32)],
        compiler_params=_cparams(("arbitrary", "arbitrary", "arbitrary")),
        name="s5_scan",
    )(x, mod3, mod3, wb, wc, lam, x0)


def _s5_weights(lam_re, lam_im, log_step, b_re, b_im, c_re, c_im):
    lam = lax.complex(lam_re.astype(F32), lam_im.astype(F32))
    step = jnp.exp(log_step.astype(F32))[..., None]
    lam_bar = jnp.exp(lam * step)
    b_bar = ((lam_bar - 1.0) / lam)[..., None] * lax.complex(b_re.astype(F32), b_im.astype(F32))
    ng, p = lam.shape[1], lam.shape[2]
    gl = ng // SUBLANES
    eye = jnp.eye(gl, dtype=F32)

    def wb_of(part):
        t = part.reshape(2, SUBLANES, gl, p, S5_GROUP)
        w = jnp.einsum('dsgpi,gh->dsgihp', t, eye)
        return w.reshape(2, SUBLANES, gl * S5_GROUP, gl * p)

    def wc_of(part):
        t = part.reshape(2, SUBLANES, gl, S5_GROUP, p)
        w = jnp.einsum('dsgip,gh->dshpgi', t, eye)
        return w.reshape(2, SUBLANES, gl * p, gl * S5_GROUP)

    wb = jnp.concatenate([wb_of(jnp.real(b_bar)), wb_of(jnp.imag(b_bar))], axis=-1).astype(BF16)
    wc = jnp.concatenate([wc_of(c_re.astype(F32)), -wc_of(c_im.astype(F32))], axis=-2).astype(BF16)
    lam_t = jnp.stack([jnp.real(lam_bar).reshape(2, SUBLANES, gl * p),
                       jnp.imag(lam_bar).reshape(2, SUBLANES, gl * p)], axis=1)
    return wb, wc, lam_t


def _s5_glu_kernel(x_ref, sh_ref, sc_ref, g1_ref, yf_ref, yb_ref, dsk_ref, w1_ref, b1_ref, w2_ref, b2_ref, o_ref):
    x = x_ref[...]
    h = _rms(x) * (1.0 + sc_ref[0]) + sh_ref[0]
    y = dsk_ref[...] * h + yf_ref[...] + yb_ref[...]
    z = jax.nn.gelu(y).astype(BF16)
    o = (_dot(z, w1_ref[...]) + b1_ref[...]) * jax.nn.sigmoid(_dot(z, w2_ref[...]) + b2_ref[...])
    o_ref[...] = x + g1_ref[0] * o


def _s5_glu(x, mod3, layer, row_fn, y2, d_skip, w1, b1, w2, b2):
    b, l, d = x.shape
    tm = min(TOKEN_TILE, l)
    mspec = lambda which: pl.BlockSpec((1, 1, d), _mod_spec(layer, row_fn, which))
    const = lambda shape: pl.BlockSpec(shape, lambda bb, t: (0,) * len(shape))
    return pl.pallas_call(
        _s5_glu_kernel,
        grid=(b, l // tm),
        in_specs=[pl.BlockSpec((None, tm, d), lambda bb, t: (bb, t, 0)),
                  mspec(MOD_SH1), mspec(MOD_SC1), mspec(MOD_G1),
                  pl.BlockSpec((None, None, tm, d), lambda bb, t: (0, bb, t, 0)),
                  pl.BlockSpec((None, None, tm, d), lambda bb, t: (1, bb, t, 0)),
                  const((1, d)), const((d, d)), const((1, d)), const((d, d)), const((1, d))],
        out_specs=pl.BlockSpec((None, tm, d), lambda bb, t: (bb, t, 0)),
        out_shape=jax.ShapeDtypeStruct((b, l, d), F32),
        compiler_params=_cparams(("arbitrary", "arbitrary")),
        name="s5_glu",
    )(x, mod3, mod3, mod3, y2, y2, d_skip.astype(F32).reshape(1, d), w1.astype(BF16), b1.astype(F32).reshape(1, d),
      w2.astype(BF16), b2.astype(F32).reshape(1, d))


def _route_kernel(x_ref, sh_ref, sc_ref, rw_ref, rb_ref, cin_ref, tri_ref, hp_ref, meta_ref, cnt_ref, carry):
    first = (pl.program_id(0) == 0) & (pl.program_id(1) == 0)

    @pl.when(first)
    def _():
        carry[...] = cin_ref[...]

    h = _modulated(x_ref, sh_ref, sc_ref)
    half = h.shape[1] // 2
    lo = pltpu.bitcast(h[:, :half].astype(BF16).astype(F32), U32)
    hi = pltpu.bitcast(h[:, half:].astype(BF16).astype(F32), U32)
    hp_ref[...] = (lo >> 16) | (hi & jnp.uint32(0xFFFF0000))

    logits = _dot_hi(h, rw_ref[...]) + rb_ref[...]
    lane = lax.broadcasted_iota(I32, logits.shape, 1)
    cur = jnp.where(lane < N_EXPERTS, logits, -jnp.inf)
    vals, idxs = [], []
    hot = jnp.zeros(logits.shape, F32)
    for _ in range(TOP_K):
        m = jnp.max(cur, axis=-1, keepdims=True)
        idx = jnp.min(jnp.where(cur == m, lane, LANES), axis=-1, keepdims=True)
        sel = lane == idx
        hot = hot + sel.astype(F32)
        cur = jnp.where(sel, -jnp.inf, cur)
        vals.append(m)
        idxs.append(idx)
    es = [jnp.exp(v - vals[0]) for v in vals]
    tot = es[0] + es[1] + es[2] + es[3]
    before = _dot(tri_ref[...], hot.astype(BF16)) + carry[...]
    meta = jnp.zeros(logits.shape, F32)
    for k in range(TOP_K):
        rank = jnp.sum(jnp.where(lane == idxs[k], before, 0.0), axis=-1, keepdims=True)
        meta = jnp.where(lane == k, idxs[k].astype(F32), meta)
        meta = jnp.where(lane == TOP_K + k, es[k] / tot, meta)
        meta = jnp.where(lane == 2 * TOP_K + k, rank, meta)
    meta_ref[...] = meta
    carry[...] = carry[...] + jnp.sum(hot, axis=0, keepdims=True)
    cnt_ref[...] = carry[...]


def _route(x, mod3, layer, row_fn, rw_pad, rb_pad, counts_in):
    b, l, d = x.shape
    tm = min(TOKEN_TILE, l)
    idx = jnp.arange(tm)
    tri = (idx[:, None] > idx[None, :]).astype(BF16)
    mspec = lambda which: pl.BlockSpec((1, 1, d), _mod_spec(layer, row_fn, which))
    const = lambda shape: pl.BlockSpec(shape, lambda bb, t: (0,) * len(shape))
    return pl.pallas_call(
        _route_kernel,
        grid=(b, l // tm),
        in_specs=[pl.BlockSpec((None, tm, d), lambda bb, t: (bb, t, 0)),
                  mspec(MOD_SH2), mspec(MOD_SC2),
                  const((d, LANES)), const((1, LANES)), const((1, LANES)), const((tm, tm))],
        out_specs=[pl.BlockSpec((None, tm, d // 2), lambda bb, t: (bb, t, 0)),
                   pl.BlockSpec((None, tm, LANES), lambda bb, t: (bb, t, 0)),
                   const((1, LANES))],
        out_shape=[jax.ShapeDtypeStruct((b, l, d // 2), U32),
                   jax.ShapeDtypeStruct((b, l, LANES), F32),
                   jax.ShapeDtypeStruct((1, LANES), F32)],
        scratch_shapes=[pltpu.VMEM((1, LANES), F32)],
        compiler_params=_cparams(("arbitrary", "arbitrary")),
        name="moe_route",
    )(x, mod3, mod3, rw_pad, rb_pad, counts_in, tri)


def _dest_kernel(meta_ref, ps_ref, o_ref):
    meta = meta_ref[...]
    lane = lax.broadcasted_iota(I32, meta.shape, 1)
    ps = ps_ref[...]
    dst = jnp.zeros(meta.shape, F32)
    for k in range(TOP_K):
        idx = meta[:, k:k + 1].astype(I32)
        base = jnp.sum(jnp.where(lane == idx, ps, 0.0), axis=-1, keepdims=True)
        dst = jnp.where(lane == k, base + meta[:, 2 * TOP_K + k:2 * TOP_K + k + 1], dst)
    dt = dst.T
    o_ref[0] = jnp.concatenate([dt[k:k + 1] for k in range(TOP_K)], axis=1).astype(I32)


def _dest(meta, pad_start_row):
    b, l, _ = meta.shape
    tt = min(TOKEN_TILE, l)
    nt = l // tt
    return pl.pallas_call(
        _dest_kernel,
        grid=(b, nt),
        in_specs=[pl.BlockSpec((None, tt, LANES), lambda bb, t: (bb, t, 0)),
                  pl.BlockSpec((1, LANES), lambda bb, t: (0, 0))],
        out_specs=pl.BlockSpec((1, 1, TOP_K * tt), lambda bb, t: (bb * nt + t, 0, 0)),
        out_shape=jax.ShapeDtypeStruct((b * nt, 1, TOP_K * tt), I32),
        compiler_params=_cparams(("arbitrary", "arbitrary")),
        name="moe_dest",
    )(meta, pad_start_row)


def _dispatch_kernel(dest_hbm, hp_ref, xs_in, xs_out, dest_smem, sem_i, sem, *, tt):
    del xs_in
    i = pl.program_id(0)
    cp = pltpu.make_async_copy(dest_hbm.at[i, 0], dest_smem, sem_i)
    cp.start()
    cp.wait()

    def row_copy(r, k):
        return pltpu.make_async_copy(hp_ref.at[pl.ds(r, 1)], xs_out.at[pl.ds(dest_smem[k * tt + r], 1)], sem)

    def issue(r, c):
        for k in range(TOP_K):
            row_copy(r, k).start()
        return c

    lax.fori_loop(0, tt, issue, 0, unroll=4)

    def drain(r, c):
        for k in range(TOP_K):
            row_copy(r, k).wait()
        return c

    lax.fori_loop(0, tt, drain, 0, unroll=4)


def _dispatch(dest, hp, xs):
    n, w = hp.shape
    tt = dest.shape[-1] // TOP_K
    return pl.pallas_call(
        functools.partial(_dispatch_kernel, tt=tt),
        grid=(n // tt,),
        in_specs=[pl.BlockSpec(memory_space=pl.ANY),
                  pl.BlockSpec((tt, w), lambda i: (i, 0)),
                  pl.BlockSpec(memory_space=pl.ANY)],
        out_specs=pl.BlockSpec(memory_space=pl.ANY),
        out_shape=jax.ShapeDtypeStruct(xs.shape, xs.dtype),
        scratch_shapes=[pltpu.SMEM((tt * TOP_K,), I32), pltpu.SemaphoreType.DMA, pltpu.SemaphoreType.DMA],
        input_output_aliases={2: 0},
        compiler_params=_cparams(("arbitrary",)),
        name="moe_dispatch",
    )(dest, hp, xs)


def _expert_kernel(be_ref, nact_ref, xs_ref, wgu_ref, bgu_ref, wd_ref, bd_ref, y_ref, wgu_bf, wd_bf):
    i = pl.program_id(0)
    e = be_ref[i]
    prev = be_ref[jnp.maximum(i - 1, 0)]

    @pl.when((i == 0) | (e != prev))
    def _():
        wgu_bf[...] = wgu_ref[0].astype(BF16)
        wd_bf[...] = wd_ref[0].astype(BF16)

    @pl.when(i < nact_ref[0])
    def _():
        w = xs_ref[...]
        lo = pltpu.bitcast(w << 16, F32)
        hi = pltpu.bitcast(w & jnp.uint32(0xFFFF0000), F32)
        x = jnp.concatenate([lo, hi], axis=1).astype(BF16)
        gu = _dot(x, wgu_bf[...]) + bgu_ref[0]
        de = gu.shape[1] // 2
        gate = jnp.minimum(gu[:, :de], SWIGLU_LIMIT)
        up = jnp.clip(gu[:, de:], -SWIGLU_LIMIT, SWIGLU_LIMIT)
        act = (up + 1.0) * gate * jax.nn.sigmoid(SWIGLU_ALPHA * gate)
        y_ref[...] = _dot(act.astype(BF16), wd_bf[...]) + bd_ref[0]

    @pl.when(i >= nact_ref[0])
    def _():
        y_ref[...] = jnp.zeros(y_ref.shape, y_ref.dtype)


def _experts(block_e, n_active, xs, w_gu, b_gu, w_d, b_d):
    n_rows, hw = xs.shape
    ne, d, de2 = w_gu.shape
    nblk = n_rows // MOE_ROWS
    grid_spec = pltpu.PrefetchScalarGridSpec(
        num_scalar_prefetch=2,
        grid=(nblk,),
        in_specs=[pl.BlockSpec((MOE_ROWS, hw), lambda i, be, na: (i, 0)),
                  pl.BlockSpec((1, d, de2), lambda i, be, na: (be[i], 0, 0)),
                  pl.BlockSpec((1, 1, de2), lambda i, be, na: (be[i], 0, 0)),
                  pl.BlockSpec((1, de2 // 2, d), lambda i, be, na: (be[i], 0, 0)),
                  pl.BlockSpec((1, 1, d), lambda i, be, na: (be[i], 0, 0))],
        out_specs=pl.BlockSpec((MOE_ROWS, d), lambda i, be, na: (i, 0)),
        scratch_shapes=[pltpu.VMEM((d, de2), BF16), pltpu.VMEM((de2 // 2, d), BF16)],
    )
    return pl.pallas_call(
        _expert_kernel,
        grid_spec=grid_spec,
        out_shape=jax.ShapeDtypeStruct((n_rows, d), F32),
        compiler_params=_cparams(("arbitrary",)),
        name="moe_experts",
    )(block_e, n_active, xs, w_gu, b_gu.reshape(ne, 1, de2), w_d, b_d.reshape(ne, 1, d))


def _combine_kernel(*refs, tt, final):
    if final:
        dest_hbm, yb_hbm, x_ref, meta_ref, g2_ref, fw_ref, o_ref, dest_smem, buf, sem_i, sem = refs
    else:
        dest_hbm, yb_hbm, x_ref, meta_ref, g2_ref, o_ref, dest_smem, buf, sem_i, sem = refs
    i = pl.program_id(0) * pl.num_programs(1) + pl.program_id(1)
    cp = pltpu.make_async_copy(dest_hbm.at[i, 0], dest_smem, sem_i)
    cp.start()
    cp.wait()

    def row_copy(r, k):
        return pltpu.make_async_copy(yb_hbm.at[pl.ds(dest_smem[k * tt + r], 1)], buf.at[k, pl.ds(r, 1)], sem)

    def issue(r, c):
        for k in range(TOP_K):
            row_copy(r, k).start()
        return c

    lax.fori_loop(0, tt, issue, 0, unroll=4)

    def drain(r, c):
        for k in range(TOP_K):
            row_copy(r, k).wait()
        return c

    lax.fori_loop(0, tt, drain, 0, unroll=4)
    meta = meta_ref[...]
    y = meta[:, TOP_K:TOP_K + 1] * buf[0]
    for k in range(1, TOP_K):
        y = y + meta[:, TOP_K + k:TOP_K + k + 1] * buf[k]
    xn = x_ref[...] + g2_ref[0] * y
    if final:
        xn = _rms(xn) * fw_ref[...]
    o_ref[...] = xn


def _combine(dest, yb, x, meta, mod3, layer, row_fn, final_w=None):
    b, l, d = x.shape
    tt = min(TOKEN_TILE, l)
    nt = l // tt
    final = final_w is not None
    in_specs = [pl.BlockSpec(memory_space=pl.ANY), pl.BlockSpec(memory_space=pl.ANY),
                pl.BlockSpec((None, tt, d), lambda bb, t: (bb, t, 0)),
                pl.BlockSpec((None, tt, LANES), lambda bb, t: (bb, t, 0)),
                pl.BlockSpec((1, 1, d), _mod_spec(layer, row_fn, MOD_G2))]
    args = [dest, yb, x, meta, mod3]
    if final:
        in_specs.append(pl.BlockSpec((1, d), lambda bb, t: (0, 0)))
        args.append(final_w.astype(F32).reshape(1, d))
    return pl.pallas_call(
        functools.partial(_combine_kernel, tt=tt, final=final),
        grid=(b, nt),
        in_specs=in_specs,
        out_specs=pl.BlockSpec((None, tt, d), lambda bb, t: (bb, t, 0)),
        out_shape=jax.ShapeDtypeStruct((b, l, d), F32),
        scratch_shapes=[pltpu.SMEM((tt * TOP_K,), I32), pltpu.VMEM((TOP_K, tt, d), F32),
                        pltpu.SemaphoreType.DMA, pltpu.SemaphoreType.DMA],
        compiler_params=_cparams(("arbitrary", "arbitrary")),
        name="moe_combine",
    )(*args)


def _moe(parts, mod3, layer, router_w, router_b, w_gu, b_gu, w_d, b_d, final_w=None):
    d = router_w.shape[0]
    rw_pad = jnp.zeros((d, LANES), F32).at[:, :N_EXPERTS].set(router_w.astype(F32))
    rb_pad = jnp.zeros((1, LANES), F32).at[0, :N_EXPERTS].set(router_b.astype(F32))
    counts = jnp.zeros((1, LANES), F32)
    routed = []
    for x, row_fn in parts:
        hp, meta, counts = _route(x, mod3, layer, row_fn, rw_pad, rb_pad, counts)
        routed.append((hp, meta))
    n_assign = sum(x.shape[0] * x.shape[1] for x, _ in parts) * TOP_K
    nblk = -(-(n_assign + N_EXPERTS * (MOE_ROWS - 1)) // MOE_ROWS)
    cnt = counts[0, :N_EXPERTS].astype(I32)
    padded = (cnt + MOE_ROWS - 1) // MOE_ROWS * MOE_ROWS
    pad_end = jnp.cumsum(padded)
    pad_start = pad_end - padded
    blk_row = jnp.arange(nblk, dtype=I32)[:, None] * MOE_ROWS
    block_e = jnp.minimum(jnp.sum((pad_end[None, :] <= blk_row).astype(I32), axis=1), N_EXPERTS - 1).astype(I32)
    n_active = (pad_end[-1] // MOE_ROWS).astype(I32).reshape(1)
    ps_row = jnp.zeros((1, LANES), F32).at[0, :N_EXPERTS].set(pad_start.astype(F32))
    xs = jnp.zeros((nblk * MOE_ROWS, d // 2), U32)
    dests = []
    for (x, _), (hp, meta) in zip(parts, routed):
        dest = _dest(meta, ps_row)
        dests.append(dest)
        xs = _dispatch(dest, hp.reshape(-1, d // 2), xs)
    yb = _experts(block_e, n_active, xs, w_gu, b_gu, w_d, b_d)
    outs = []
    for idx, ((x, row_fn), (hp, meta)) in enumerate(zip(parts, routed)):
        fw = final_w if idx == 0 else None
        outs.append(_combine(dests[idx], yb, x, meta, mod3, layer, row_fn, fw))
    return outs


def kernel(x, c, ctx, c_ctx, ada_w, ada_b, ab_w_in, ab_q_gain, ab_k_gain, dn_conv_w, dn_a_log, dn_dt_bias, dn_norm_w, ab_w_out, s5_lambda_re, s5_lambda_im, s5_log_step, s5_b_re, s5_b_im, s5_c_re, s5_c_im, s5_d, s5_glu_w1, s5_glu_b1, s5_glu_w2, s5_glu_b2, moe_router_w, moe_router_b, moe_w_gate_up, moe_b_gate_up, moe_w_down, moe_b_down, final_norm_w):
    b, l, d = x.shape
    lc = ctx.shape[1]
    depth = ada_w.shape[0]
    assert depth == 2 and b < MOD_ROWS
    lat_row = lambda bb: bb
    ctx_row = lambda bb: b

    rows = jnp.zeros((MOD_ROWS, d), F32).at[:b].set(c.astype(F32)).at[b].set(c_ctx.astype(F32))
    mod3 = _ada(rows, ada_w, ada_b).reshape(depth * MOD_ROWS * 6, 1, d)

    w_in = ab_w_in[0]
    aq, akv, bqk = Q_HEADS * HEAD_DIM, KV_HEADS * HEAD_DIM, DN_HEADS * DN_DK
    o_k, o_v, o_dn, o_ab, o_z = aq, aq + akv, aq + 2 * akv, aq + 2 * akv + 3 * bqk, aq + 2 * akv + 3 * bqk + 4 * DN_HEADS
    w_perm = jnp.concatenate([w_in[:, :aq], w_in[:, o_dn:o_ab], w_in[:, o_z:], w_in[:, o_k:o_v], w_in[:, o_v:o_dn],
                              w_in[:, o_ab:o_z], jnp.zeros((d, IN_COLS_PAD - w_in.shape[1]), w_in.dtype)], axis=1).astype(BF16)
    proj_l = _inproj(x, mod3, 0, lat_row, w_perm)
    proj_c = _inproj(ctx, mod3, 0, ctx_row, w_perm)

    cos, sin = _rope_tables(l)
    ones = jnp.ones((lc, KV_HEADS * HEAD_DIM), F32)
    q_l, kt_l, v_l = _qkprep(proj_l, ab_q_gain[0], ab_k_gain[0], cos, sin, min(ATTN_K_TILE, l))
    q_c, kt_c, v_c = _qkprep(proj_c, ab_q_gain[0], ab_k_gain[0], ones, jnp.zeros_like(ones), lc)
    a_l = _attention(q_l, kt_c, v_c, kt_l, v_l)
    a_c = _attention(q_c, kt_c, v_c)

    dq_c, dk_c, dkt_c, dv_c, gc_c, gct_c = _dnprep(proj_c, dn_conv_w[0], dn_a_log[0], dn_dt_bias[0])
    dq_l, dk_l, dkt_l, dv_l, gc_l, gct_l = _dnprep(proj_l, dn_conv_w[0], dn_a_log[0], dn_dt_bias[0])
    s_zero = jnp.zeros((b, 2, DN_HEADS, DN_DK, DN_DK), F32)
    of_c, ob_c, s_ctx = _dnscan(dq_c, dk_c, dkt_c, dv_c, gc_c, gct_c, s_zero)
    of_l, ob_l, _ = _dnscan(dq_l, dk_l, dkt_l, dv_l, gc_l, gct_l, s_ctx)

    w_out = ab_w_out[0].astype(BF16)
    x = _mix0_out(x, mod3, 0, lat_row, a_l, of_l, ob_l, proj_l, dn_norm_w[0], w_out)
    ctx = _mix0_out(ctx, mod3, 0, ctx_row, a_c, of_c, ob_c, proj_c, dn_norm_w[0], w_out)

    x, ctx = _moe([(x, lat_row), (ctx, ctx_row)], mod3, 0, moe_router_w[0], moe_router_b[0],
                  moe_w_gate_up[0], moe_b_gate_up[0], moe_w_down[0], moe_b_down[0])

    wb, wc, lam_t = _s5_weights(s5_lambda_re[0], s5_lambda_im[0], s5_log_step[0], s5_b_re[0], s5_b_im[0],
                                s5_c_re[0], s5_c_im[0])
    sw = lam_t.shape[-1]
    x0 = jnp.zeros((b, 2, 2, SUBLANES, sw), F32)
    _, x_ctx = _s5_scan(ctx, mod3, 1, ctx_row, wb, wc, lam_t, x0)
    y2, _ = _s5_scan(x, mod3, 1, lat_row, wb, wc, lam_t, x_ctx)
    x = _s5_glu(x, mod3, 1, lat_row, y2, s5_d[0], s5_glu_w1[0], s5_glu_b1[0], s5_glu_w2[0], s5_glu_b2[0])

    (x,) = _moe([(x, lat_row)], mod3, 1, moe_router_w[1], moe_router_b[1], moe_w_gate_up[1], moe_b_gate_up[1],
                moe_w_down[1], moe_b_down[1], final_w=final_norm_w)
    return x
```

```python
import functools
import math

import jax
import jax.numpy as jnp
from jax import lax
from jax.experimental import pallas as pl
from jax.experimental.pallas import tpu as pltpu

F32 = jnp.float32
BF16 = jnp.bfloat16
U32 = jnp.uint32
I32 = jnp.int32

EPS = 1e-6
GRID_W = 64
HEAD_DIM = 64
Q_HEADS = 8
KV_HEADS = 2
GROUP = Q_HEADS // KV_HEADS
ROPE_THETA = 10000.0
DN_HEADS = 8
DN_DK = 64
DN_CONV = 5
DN_CHUNK = 64
S5_GROUP = 16
S5_STATE = 64
N_EXPERTS = 32
TOP_K = 4
SWIGLU_LIMIT = 7.0
SWIGLU_ALPHA = 1.702

LANES = 128
SUBLANES = 8
TOKEN_TILE = 256
ATTN_Q_TILE = 256
ATTN_K_TILE = 1024
DN_BLOCK_CHUNKS = 4
MOE_ROWS = 512
S5_PITCH_PAD = 4
VMEM_LIMIT = 56 * 1024 * 1024

MOD_SH1, MOD_SC1, MOD_G1, MOD_SH2, MOD_SC2, MOD_G2 = range(6)
MOD_ROWS = 8

COL_Q = 0
COL_DNQ = 512
COL_DNK = 1024
COL_DNV = 1536
COL_Z = 2048
COL_K = 2560
COL_V = 2688
COL_AB = 2816
IN_COLS_PAD = 2944


def _cparams(sem):
    return pltpu.CompilerParams(dimension_semantics=sem, vmem_limit_bytes=VMEM_LIMIT)


def _split3(a):
    a1 = a.astype(BF16)
    r = a - a1.astype(F32)
    a2 = r.astype(BF16)
    a3 = (r - a2.astype(F32)).astype(BF16)
    return a1, a2, a3


def _dot(a, b):
    return jnp.dot(a, b, preferred_element_type=F32)


def _dot_exact_lhs(m_bf16, x):
    x1, x2, x3 = _split3(x)
    return _dot(m_bf16, x1) + _dot(m_bf16, x2) + _dot(m_bf16, x3)


def _dot_exact_rhs(x, m_bf16):
    x1, x2, x3 = _split3(x)
    return _dot(x1, m_bf16) + _dot(x2, m_bf16) + _dot(x3, m_bf16)


def _dot_hi(a, b):
    a1, a2, _ = _split3(a)
    b1, b2, _ = _split3(b)
    return _dot(a1, b1) + _dot(a1, b2) + _dot(a2, b1)


def _rms(x):
    return x * lax.rsqrt(jnp.mean(x * x, axis=-1, keepdims=True) + EPS)


def _modulated(x_ref, sh_ref, sc_ref):
    return _rms(x_ref[...]) * (1.0 + sc_ref[0]) + sh_ref[0]


def _mod_spec(layer, row_fn, which):
    d = None

    def imap(b, t):
        return ((layer * MOD_ROWS + row_fn(b)) * 6 + which, 0, 0)

    return imap


def _ada_kernel(a_ref, w_ref, b_ref, o_ref):
    a = a_ref[...]
    a = a * jax.nn.sigmoid(a)
    o_ref[0] = _dot_hi(a, w_ref[0]) + b_ref[0]


def _ada(rows, ada_w, ada_b):
    depth, d, n = ada_w.shape
    tn = 1536
    return pl.pallas_call(
        _ada_kernel,
        grid=(depth, n // tn),
        in_specs=[pl.BlockSpec((MOD_ROWS, d), lambda l, j: (0, 0)),
                  pl.BlockSpec((1, d, tn), lambda l, j: (l, 0, j)),
                  pl.BlockSpec((1, 1, tn), lambda l, j: (l, 0, j))],
        out_specs=pl.BlockSpec((1, MOD_ROWS, tn), lambda l, j: (l, 0, j)),
        out_shape=jax.ShapeDtypeStruct((depth, MOD_ROWS, n), F32),
        compiler_params=_cparams(("arbitrary", "arbitrary")),
        name="ada_mod",
    )(rows, ada_w, ada_b.reshape(depth, 1, n))


def _inproj_kernel(x_ref, sh_ref, sc_ref, w_ref, o_ref):
    h = _modulated(x_ref, sh_ref, sc_ref)
    o_ref[...] = _dot(h.astype(BF16), w_ref[...])


def _inproj(x, mod3, layer, row_fn, w_bf16):
    b, l, d = x.shape
    tm = min(TOKEN_TILE, l)
    n = w_bf16.shape[1]
    mspec = lambda which: pl.BlockSpec((1, 1, d), _mod_spec(layer, row_fn, which))
    return pl.pallas_call(
        _inproj_kernel,
        grid=(b, l // tm),
        in_specs=[pl.BlockSpec((None, tm, d), lambda bb, t: (bb, t, 0)),
                  mspec(MOD_SH1), mspec(MOD_SC1),
                  pl.BlockSpec((d, n), lambda bb, t: (0, 0))],
        out_specs=pl.BlockSpec((None, tm, n), lambda bb, t: (bb, t, 0)),
        out_shape=jax.ShapeDtypeStruct((b, l, n), F32),
        compiler_params=_cparams(("arbitrary", "arbitrary")),
        name="in_proj",
    )(x, mod3, mod3, w_bf16)


def _head_rot(x, width):
    lane = lax.broadcasted_iota(I32, x.shape, 1)
    first = (lane % HEAD_DIM) < (HEAD_DIM // 2)
    return jnp.where(first, pltpu.roll(x, width - HEAD_DIM // 2, 1), pltpu.roll(x, HEAD_DIM // 2, 1))


def _qkprep_kernel(q_ref, k_ref, v_ref, qg_ref, kg_ref, cos_ref, sin_ref, gq_ref, gk_ref, qo_ref, kt_ref, vo_ref):
    cos = cos_ref[...]
    sin = sin_ref[...]
    q = q_ref[...]
    qn = q * lax.rsqrt(_dot_exact_rhs(q * q, gq_ref[...]) + EPS) * qg_ref[...]
    cos4 = jnp.concatenate([cos] * 4, axis=1)
    sin4 = jnp.concatenate([sin] * 4, axis=1)
    qr = qn * cos4 + _head_rot(qn, Q_HEADS * HEAD_DIM) * sin4
    qo_ref[...] = (qr * (HEAD_DIM ** -0.5 * math.log2(math.e))).astype(BF16)
    k = k_ref[...]
    kn = k * lax.rsqrt(_dot_exact_rhs(k * k, gk_ref[...]) + EPS) * kg_ref[...]
    kr = kn * cos + _head_rot(kn, KV_HEADS * HEAD_DIM) * sin
    kt = kr.T
    kt_ref[0] = kt[:HEAD_DIM].astype(BF16)
    kt_ref[1] = kt[HEAD_DIM:].astype(BF16)
    v = v_ref[...]
    ones = jnp.ones((v.shape[0], HEAD_DIM), F32)
    vo_ref[0] = jnp.concatenate([v[:, :HEAD_DIM], ones], axis=1).astype(BF16)
    vo_ref[1] = jnp.concatenate([v[:, HEAD_DIM:], ones], axis=1).astype(BF16)


def _group_matrix(width, scale):
    g = jnp.arange(width) // HEAD_DIM
    return ((g[:, None] == g[None, :]).astype(F32) * scale).astype(BF16)


def _qkprep(proj, q_gain, k_gain, cos, sin, tk):
    b, l, _ = proj.shape
    qw = Q_HEADS * HEAD_DIM
    kw = KV_HEADS * HEAD_DIM
    qg = jnp.tile(q_gain.astype(F32), Q_HEADS).reshape(1, qw)
    kg = jnp.tile(k_gain.astype(F32), KV_HEADS).reshape(1, kw)
    const = lambda shape: pl.BlockSpec(shape, lambda bb, t: (0,) * len(shape))
    return pl.pallas_call(
        _qkprep_kernel,
        grid=(b, l // tk),
        in_specs=[pl.BlockSpec((None, tk, qw), lambda bb, t: (bb, t, COL_Q // qw)),
                  pl.BlockSpec((None, tk, kw), lambda bb, t: (bb, t, COL_K // kw)),
                  pl.BlockSpec((None, tk, kw), lambda bb, t: (bb, t, COL_V // kw)),
                  const((1, qw)), const((1, kw)),
                  pl.BlockSpec((tk, kw), lambda bb, t: (t, 0)),
                  pl.BlockSpec((tk, kw), lambda bb, t: (t, 0)),
                  const((qw, qw)), const((kw, kw))],
        out_specs=[pl.BlockSpec((None, tk, qw), lambda bb, t: (bb, t, 0)),
                   pl.BlockSpec((None, KV_HEADS, None, HEAD_DIM, tk), lambda bb, t: (bb, 0, t, 0, 0)),
                   pl.BlockSpec((None, KV_HEADS, tk, 2 * HEAD_DIM), lambda bb, t: (bb, 0, t, 0))],
        out_shape=[jax.ShapeDtypeStruct((b, l, qw), BF16),
                   jax.ShapeDtypeStruct((b, KV_HEADS, l // tk, HEAD_DIM, tk), BF16),
                   jax.ShapeDtypeStruct((b, KV_HEADS, l, 2 * HEAD_DIM), BF16)],
        compiler_params=_cparams(("arbitrary", "arbitrary")),
        name="qk_prep",
    )(proj, proj, proj, qg, kg, cos, sin, _group_matrix(qw, 1.0 / HEAD_DIM), _group_matrix(kw, 1.0 / HEAD_DIM))


def _rope_tables(length):
    rows = length // GRID_W
    row = jnp.broadcast_to(jnp.arange(rows, dtype=I32)[:, None], (rows, GRID_W)).reshape(-1).astype(F32)
    col = jnp.broadcast_to(jnp.arange(GRID_W, dtype=I32)[None, :], (rows, GRID_W)).reshape(-1).astype(F32)
    n_axis = HEAD_DIM // 4
    inv_freq = ROPE_THETA ** (-jnp.arange(n_axis, dtype=F32) / n_axis)
    ang = jnp.concatenate([row[:, None] * inv_freq, col[:, None] * inv_freq], axis=-1)
    c, s = jnp.cos(ang), jnp.sin(ang)
    cos_h = jnp.concatenate([c, c], axis=-1)
    sin_h = jnp.concatenate([-s, s], axis=-1)
    return jnp.tile(cos_h, (1, KV_HEADS)), jnp.tile(sin_h, (1, KV_HEADS))


def _attn_kernel(*refs, tq, n_lat_chunks, has_lat):
    if has_lat:
        q_ref, ktc_ref, vc_ref, ktl_ref, vl_ref, o_ref = refs
    else:
        q_ref, ktc_ref, vc_ref, o_ref = refs
    q = q_ref[...]
    qs = jnp.concatenate([q[:, g * HEAD_DIM:(g + 1) * HEAD_DIM] for g in range(GROUP)], axis=0)
    rows = GROUP * tq

    def step(kt, v_ones, m, acc):
        s = _dot(qs, kt)
        m_new = jnp.maximum(m, jnp.max(s, axis=-1, keepdims=True))
        p = jnp.exp2((s - m_new).astype(BF16))
        acc = jnp.exp2(m - m_new) * acc + _dot(p, v_ones)
        return m_new, acc

    m0 = jnp.full((rows, 1), -1e30, F32)
    a0 = jnp.zeros((rows, 2 * HEAD_DIM), F32)
    carry = step(ktc_ref[0], vc_ref[...], m0, a0)
    if has_lat:
        tk = ktl_ref.shape[-1]

        def body(j, c):
            off = pl.multiple_of(j * tk, tk)
            return step(ktl_ref[j], vl_ref[pl.ds(off, tk), :], *c)

        carry = lax.fori_loop(0, n_lat_chunks, body, carry, unroll=2)
    _, acc = carry
    o = acc[:, :HEAD_DIM] / acc[:, HEAD_DIM:HEAD_DIM + 1]
    o_ref[...] = jnp.concatenate([o[g * tq:(g + 1) * tq] for g in range(GROUP)], axis=1).astype(o_ref.dtype)


def _attention(q, ktc, vc, ktl=None, vl=None):
    b, lq, qw = q.shape
    tq = min(ATTN_Q_TILE, lq)
    gw = GROUP * HEAD_DIM
    lc = vc.shape[2]
    has_lat = ktl is not None
    in_specs = [pl.BlockSpec((None, tq, gw), lambda bb, h, i: (bb, i, h)),
                pl.BlockSpec((None, None, 1, HEAD_DIM, lc), lambda bb, h, i: (bb, h, 0, 0, 0)),
                pl.BlockSpec((None, None, lc, 2 * HEAD_DIM), lambda bb, h, i: (bb, h, 0, 0))]
    args = [q, ktc, vc]
    n_chunks = 0
    if has_lat:
        n_chunks, tk = ktl.shape[2], ktl.shape[4]
        ll = vl.shape[2]
        in_specs += [pl.BlockSpec((None, None, n_chunks, HEAD_DIM, tk), lambda bb, h, i: (bb, h, 0, 0, 0)),
                     pl.BlockSpec((None, None, ll, 2 * HEAD_DIM), lambda bb, h, i: (bb, h, 0, 0))]
        args += [ktl, vl]
    return pl.pallas_call(
        functools.partial(_attn_kernel, tq=tq, n_lat_chunks=n_chunks, has_lat=has_lat),
        grid=(b, KV_HEADS, lq // tq),
        in_specs=in_specs,
        out_specs=pl.BlockSpec((None, tq, gw), lambda bb, h, i: (bb, i, h)),
        out_shape=jax.ShapeDtypeStruct((b, lq, qw), BF16),
        compiler_params=_cparams(("arbitrary", "arbitrary", "arbitrary")),
        name="gqa_attention",
    )(*args)


def _dnprep_kernel(qp, qc, qn, kp, kc, kn, vp, vc, vn, ab_ref, cw_ref, alog_ref, dtb_ref, gm_ref, trip_ref, tris_ref,
                   qo_ref, ko_ref, kto_ref, vo_ref, gc_ref, gct_ref, *, tm):
    t = pl.program_id(1)
    nt = pl.num_programs(1)
    mp = (t > 0).astype(F32)
    mn = (t < nt - 1).astype(F32)
    n_ext = tm + 2 * SUBLANES
    pad = DN_CONV // 2

    def conv(p_ref, c_ref, n_ref, w):
        ext = jnp.concatenate([p_ref[...] * mp, c_ref[...], n_ref[...] * mn], axis=0)
        acc = None
        for j in range(DN_CONV):
            s = (pad - j) % n_ext
            r = ext if s == 0 else pltpu.roll(ext, s, 0)
            term = r[SUBLANES:SUBLANES + tm] * w[j:j + 1, :]
            acc = term if acc is None else acc + term
        return acc * jax.nn.sigmoid(acc)

    cw = cw_ref[...]
    hw = DN_HEADS * DN_DK
    gm = gm_ref[...]
    q = conv(qp, qc, qn, cw[:, 0:hw])
    q = q * lax.rsqrt(_dot_exact_rhs(q * q, gm) + EPS) * (DN_DK ** -0.5)
    k = conv(kp, kc, kn, cw[:, hw:2 * hw])
    k = k * lax.rsqrt(_dot_exact_rhs(k * k, gm) + EPS)
    v = conv(vp, vc, vn, cw[:, 2 * hw:3 * hw])
    kt = k.T
    for h in range(DN_HEADS):
        sl = slice(h * DN_DK, (h + 1) * DN_DK)
        qo_ref[h] = q[:, sl]
        ko_ref[h] = k[:, sl]
        vo_ref[h] = v[:, sl]
        for c in range(tm // DN_CHUNK):
            kto_ref[h, c] = kt[sl, c * DN_CHUNK:(c + 1) * DN_CHUNK]

    ab = ab_ref[...]
    lane = lax.broadcasted_iota(I32, ab.shape, 1)
    a = ab + dtb_ref[...]
    sp = jnp.maximum(a, 0.0) + jnp.log(1.0 + jnp.exp(-jnp.abs(a)))
    g = -jnp.exp(alog_ref[...]) * sp
    g = jnp.where(lane < 2 * DN_HEADS, g, 0.0)
    gpre = _dot_exact_lhs(trip_ref[...], g)
    gsuf = _dot_exact_lhs(tris_ref[...], g)
    gc = jnp.where(lane < DN_HEADS, gpre, jnp.where(lane < 2 * DN_HEADS, gsuf, jax.nn.sigmoid(ab)))
    gc_ref[...] = gc
    gt = gc.T
    for c in range(tm // DN_CHUNK):
        gct_ref[c] = gt[:4 * DN_HEADS, c * DN_CHUNK:(c + 1) * DN_CHUNK]


def _dnprep(proj, conv_w, a_log, dt_bias):
    b, l, _ = proj.shape
    tm = min(TOKEN_TILE, l)
    hw = DN_HEADS * DN_DK
    nc = l // DN_CHUNK
    r8 = tm // SUBLANES
    last8 = l // SUBLANES - 1

    def cur(col):
        return pl.BlockSpec((None, tm, hw), lambda bb, t: (bb, t, col // hw))

    def prev(col):
        return pl.BlockSpec((None, SUBLANES, hw), lambda bb, t: (bb, jnp.maximum(t * r8 - 1, 0), col // hw))

    def nxt(col):
        return pl.BlockSpec((None, SUBLANES, hw), lambda bb, t: (bb, jnp.minimum((t + 1) * r8, last8), col // hw))

    const = lambda shape: pl.BlockSpec(shape, lambda bb, t: (0,) * len(shape))
    alog = jnp.zeros((1, LANES), F32).at[0, :2 * DN_HEADS].set(a_log.astype(F32).reshape(-1))
    dtb = jnp.zeros((1, LANES), F32).at[0, :2 * DN_HEADS].set(dt_bias.astype(F32).reshape(-1))
    ch = jnp.arange(tm) // DN_CHUNK
    same = ch[:, None] == ch[None, :]
    idx = jnp.arange(tm)
    trip = (same & (idx[:, None] >= idx[None, :])).astype(BF16)
    tris = (same & (idx[:, None] <= idx[None, :])).astype(BF16)
    gmat = _group_matrix(hw, 1.0)
    in_specs = []
    args = []
    for col in (COL_DNQ, COL_DNK, COL_DNV):
        in_specs += [prev(col), cur(col), nxt(col)]
        args += [proj, proj, proj]
    in_specs += [pl.BlockSpec((None, tm, LANES), lambda bb, t: (bb, t, COL_AB // LANES)),
                 const((DN_CONV, 3 * hw)), const((1, LANES)), const((1, LANES)), const((hw, hw)),
                 const((tm, tm)), const((tm, tm))]
    args += [proj, conv_w.astype(F32), alog, dtb, gmat, trip, tris]
    head_spec = pl.BlockSpec((None, DN_HEADS, tm, DN_DK), lambda bb, t: (bb, 0, t, 0))
    head_shape = jax.ShapeDtypeStruct((b, DN_HEADS, l, DN_DK), F32)
    return pl.pallas_call(
        functools.partial(_dnprep_kernel, tm=tm),
        grid=(b, l // tm),
        in_specs=in_specs,
        out_specs=[head_spec, head_spec,
                   pl.BlockSpec((None, DN_HEADS, tm // DN_CHUNK, DN_DK, DN_CHUNK), lambda bb, t: (bb, 0, t, 0, 0)),
                   head_spec,
                   pl.BlockSpec((None, tm, LANES), lambda bb, t: (bb, t, 0)),
                   pl.BlockSpec((None, tm // DN_CHUNK, 4 * DN_HEADS, DN_CHUNK), lambda bb, t: (bb, t, 0, 0))],
        out_shape=[head_shape, head_shape,
                   jax.ShapeDtypeStruct((b, DN_HEADS, nc, DN_DK, DN_CHUNK), F32),
                   head_shape,
                   jax.ShapeDtypeStruct((b, l, LANES), F32),
                   jax.ShapeDtypeStruct((b, nc, 4 * DN_HEADS, DN_CHUNK), F32)],
        compiler_params=_cparams(("arbitrary", "arbitrary")),
        name="dn_prep",
    )(*args)


def _bmm(a, b):
    return jnp.einsum('nij,njk->nik', a.astype(BF16), b.astype(BF16), preferred_element_type=F32)


def _unit_tri_inverse(a, eye, blks):
    inner = jnp.where(blks[0], a, 0.0)
    x = eye - inner
    p = inner
    for _ in range(2):
        p = _bmm(p, p)
        x = x + _bmm(x, p)
    for level in range(1, len(blks) + 1):
        outer = jnp.where(blks[level], a, 0.0) if level < len(blks) else a
        x = x - _bmm(x, _bmm(outer - inner, x))
        inner = outer
    return x


def _dn_local(q, k, kt, v, gcol, grow, beta, g_last, incl, strict, eye, blks):
    decay = jnp.where(incl, jnp.exp(jnp.where(incl, gcol - grow, 0.0)), 0.0)
    kb = k * beta
    a = jnp.where(strict, _bmm(kb, kt) * decay, 0.0)
    x = _unit_tri_inverse(a, eye, blks)
    u = _bmm(x, v * beta)
    w = _bmm(x, kb * jnp.exp(gcol))
    qk = _bmm(q, kt) * decay
    return u, w, qk, q * jnp.exp(gcol), kt * jnp.exp(g_last - grow), jnp.exp(g_last)


def _dnscan_kernel(qf, kf, ktf, vf, gcf, gctf, qb, kb_, ktb, vb, gcb, gctb, s0_ref, of_ref, ob_ref, sout_ref, s_scr, *, cb):
    i = pl.program_id(1)
    last = pl.num_programs(1) - 1
    nh = DN_HEADS

    @pl.when(i == 0)
    def _():
        s_scr[...] = s0_ref[...]

    row = lax.broadcasted_iota(I32, (DN_CHUNK, DN_CHUNK), 0)
    col = lax.broadcasted_iota(I32, (DN_CHUNK, DN_CHUNK), 1)
    eye = (row == col).astype(F32)
    blks = tuple((row // n) == (col // n) for n in (8, 16, 32))
    dirs = ((0, qf, kf, ktf, vf, gcf, gctf, row >= col, row > col),
            (1, qb, kb_, ktb, vb, gcb, gctb, row <= col, row < col))
    local = []
    for d, q_ref, k_ref, kt_ref, v_ref, gc_ref, gct_ref, incl, strict in dirs:
        last_row = DN_CHUNK - 1 if d == 0 else 0
        qs, ks, kts, vs, gcols, grows, betas, glasts = [], [], [], [], [], [], [], []
        for c in range(cb):
            rs = slice(c * DN_CHUNK, (c + 1) * DN_CHUNK)
            gc = gc_ref[rs, :]
            gct = gct_ref[c]
            for h in range(nh):
                gl = d * nh + h
                bl = 2 * nh + d * nh + h
                qs.append(q_ref[h, rs, :])
                ks.append(k_ref[h, rs, :])
                kts.append(kt_ref[h, c])
                vs.append(v_ref[h, rs, :])
                gcols.append(gc[:, gl:gl + 1])
                grows.append(gct[gl:gl + 1, :])
                betas.append(gc[:, bl:bl + 1])
                glasts.append(gc[last_row:last_row + 1, gl:gl + 1])
        st = lambda xs: jnp.stack(xs, axis=0)
        local.append(_dn_local(st(qs), st(ks), st(kts), st(vs), st(gcols), st(grows), st(betas), st(glasts),
                               incl, strict, eye, blks))

    s_all = jnp.concatenate([s_scr[0], s_scr[1]], axis=0)
    for jj in range(cb):
        sel = (slice(jj * nh, (jj + 1) * nh), slice((cb - 1 - jj) * nh, (cb - jj) * nh))
        u, w, qk, qg, kdt, el = (jnp.concatenate([local[0][t][sel[0]], local[1][t][sel[1]]], axis=0) for t in range(6))
        v_new = u - _bmm(w, s_all)
        o = _bmm(qg, s_all) + _bmm(qk, v_new)
        s_all = s_all * el + _bmm(kdt, v_new)
        for d, o_ref in ((0, of_ref), (1, ob_ref)):
            c = jj if d == 0 else cb - 1 - jj
            for h in range(nh):
                o_ref[h, c * DN_CHUNK:(c + 1) * DN_CHUNK, :] = o[d * nh + h]
    s_scr[0] = s_all[:nh]
    s_scr[1] = s_all[nh:]

    @pl.when(i == last)
    def _():
        sout_ref[...] = s_scr[...]


def _dnscan(q, k, kt, v, gc, gct, s0):
    b, nh, l, dk = q.shape
    cb = min(DN_BLOCK_CHUNKS, l // DN_CHUNK)
    tb = cb * DN_CHUNK
    nblk = l // tb
    fwd = lambda i: i
    bwd = lambda i: nblk - 1 - i

    def specs(ix):
        head = pl.BlockSpec((None, nh, tb, dk), lambda bb, i: (bb, 0, ix(i), 0))
        return [head, head,
                pl.BlockSpec((None, nh, cb, dk, DN_CHUNK), lambda bb, i: (bb, 0, ix(i), 0, 0)),
                head,
                pl.BlockSpec((None, tb, LANES), lambda bb, i: (bb, ix(i), 0)),
                pl.BlockSpec((None, cb, 4 * DN_HEADS, DN_CHUNK), lambda bb, i: (bb, ix(i), 0, 0))]

    state_spec = pl.BlockSpec((None, 2, nh, dk, dk), lambda bb, i: (bb, 0, 0, 0, 0))
    o_shape = jax.ShapeDtypeStruct((b, nh, l, dk), F32)
    return pl.pallas_call(
        functools.partial(_dnscan_kernel, cb=cb),
        grid=(b, nblk),
        in_specs=specs(fwd) + specs(bwd) + [state_spec],
        out_specs=[pl.BlockSpec((None, nh, tb, dk), lambda bb, i: (bb, 0, fwd(i), 0)),
                   pl.BlockSpec((None, nh, tb, dk), lambda bb, i: (bb, 0, bwd(i), 0)),
                   state_spec],
        out_shape=[o_shape, o_shape, jax.ShapeDtypeStruct((b, 2, nh, dk, dk), F32)],
        scratch_shapes=[pltpu.VMEM((2, nh, dk, dk), F32)],
        compiler_params=_cparams(("arbitrary", "arbitrary")),
        name="dn_scan",
    )(q, k, kt, v, gc, gct, q, k, kt, v, gc, gct, s0)


def _mix0_out_kernel(x_ref, g1_ref, a_ref, of_ref, ob_ref, z_ref, nw_ref, w_ref, o_ref):
    o = of_ref[...] + ob_ref[...]
    on = o * lax.rsqrt(jnp.mean(o * o, axis=-1, keepdims=True) + EPS) * nw_ref[...]
    ot = jnp.concatenate([on[h] for h in range(DN_HEADS)], axis=1)
    z = z_ref[...]
    bmix = ot * (z * jax.nn.sigmoid(z))
    mix = jnp.concatenate([a_ref[...], bmix.astype(BF16)], axis=1)
    o_ref[...] = x_ref[...] + g1_ref[0] * _dot(mix, w_ref[...])


def _mix0_out(x, mod3, layer, row_fn, attn, o_f, o_b, proj, norm_w, w_out_bf16):
    b, l, d = x.shape
    tm = min(TOKEN_TILE, l)
    aw = Q_HEADS * HEAD_DIM
    zw = DN_HEADS * DN_DK
    head = pl.BlockSpec((None, DN_HEADS, tm, DN_DK), lambda bb, t: (bb, 0, t, 0))
    return pl.pallas_call(
        _mix0_out_kernel,
        grid=(b, l // tm),
        in_specs=[pl.BlockSpec((None, tm, d), lambda bb, t: (bb, t, 0)),
                  pl.BlockSpec((1, 1, d), _mod_spec(layer, row_fn, MOD_G1)),
                  pl.BlockSpec((None, tm, aw), lambda bb, t: (bb, t, 0)),
                  head, head,
                  pl.BlockSpec((None, tm, zw), lambda bb, t: (bb, t, COL_Z // zw)),
                  pl.BlockSpec((1, DN_DK), lambda bb, t: (0, 0)),
                  pl.BlockSpec((aw + zw, d), lambda bb, t: (0, 0))],
        out_specs=pl.BlockSpec((None, tm, d), lambda bb, t: (bb, t, 0)),
        out_shape=jax.ShapeDtypeStruct((b, l, d), F32),
        compiler_params=_cparams(("arbitrary", "arbitrary")),
        name="mix0_out",
    )(x, mod3, attn, o_f, o_b, proj, norm_w.astype(F32).reshape(1, DN_DK), w_out_bf16)


def _s5_kernel(x_ref, sh_ref, sc_ref, wb_ref, wc_ref, lam_ref, x0_ref, y_ref, xfin_ref, bre, bim, st, *, tm, pitch):
    d = pl.program_id(1)
    i = pl.program_id(2)
    nsub = SUBLANES
    nj = bre.shape[0]
    h = _modulated(x_ref, sh_ref, sc_ref).astype(BF16)
    half = nj * LANES
    for s in range(nsub):
        o = _dot(h[:, s * LANES:(s + 1) * LANES], wb_ref[s])
        for j in range(nj):
            bre[j, s * pitch:s * pitch + tm, :] = o[:, j * LANES:(j + 1) * LANES]
            bim[j, s * pitch:s * pitch + tm, :] = o[:, half + j * LANES:half + (j + 1) * LANES]

    @pl.when(i == 0)
    def _():
        st[...] = x0_ref[...]

    lam_re = lam_ref[0]
    lam_im = lam_ref[1]

    def body(k, carry):
        xr, xi = carry
        t = jnp.where(d == 0, k, tm - 1 - k)
        br = jnp.concatenate([bre[j, pl.ds(t, nsub, stride=pitch), :] for j in range(nj)], axis=1)
        bi = jnp.concatenate([bim[j, pl.ds(t, nsub, stride=pitch), :] for j in range(nj)], axis=1)
        nr = lam_re * xr - lam_im * xi + br
        ni = lam_re * xi + lam_im * xr + bi
        for j in range(nj):
            bre[j, pl.ds(t, nsub, stride=pitch), :] = nr[:, j * LANES:(j + 1) * LANES]
            bim[j, pl.ds(t, nsub, stride=pitch), :] = ni[:, j * LANES:(j + 1) * LANES]
        return nr, ni

    xr, xi = lax.fori_loop(0, tm, body, (st[0], st[1]), unroll=8)
    st[0] = xr
    st[1] = xi
    xfin_ref[0] = xr
    xfin_ref[1] = xi
    for s in range(nsub):
        parts = [bre[j, s * pitch:s * pitch + tm, :] for j in range(nj)]
        parts += [bim[j, s * pitch:s * pitch + tm, :] for j in range(nj)]
        xs = jnp.concatenate(parts, axis=1).astype(BF16)
        y_ref[:, s * LANES:(s + 1) * LANES] = _dot(xs, wc_ref[s])


def _s5_scan(x, mod3, layer, row_fn, wb, wc, lam, x0):
    b, l, d = x.shape
    tm = min(TOKEN_TILE, l)
    nt = l // tm
    pitch = tm + S5_PITCH_PAD
    sw = wb.shape[-1] // 2
    nj = sw // LANES
    tile = lambda dd, i: jnp.where(dd == 0, i, nt - 1 - i)
    mspec = lambda which: pl.BlockSpec((1, 1, d), lambda bb, dd, i: _mod_spec(layer, row_fn, which)(bb, i))
    return pl.pallas_call(
        functools.partial(_s5_kernel, tm=tm, pitch=pitch),
        grid=(b, 2, nt),
        in_specs=[pl.BlockSpec((None, tm, d), lambda bb, dd, i: (bb, tile(dd, i), 0)),
                  mspec(MOD_SH1), mspec(MOD_SC1),
                  pl.BlockSpec((None, SUBLANES, LANES, 2 * sw), lambda bb, dd, i: (dd, 0, 0, 0)),
                  pl.BlockSpec((None, SUBLANES, 2 * sw, LANES), lambda bb, dd, i: (dd, 0, 0, 0)),
                  pl.BlockSpec((None, 2, SUBLANES, sw), lambda bb, dd, i: (dd, 0, 0, 0)),
                  pl.BlockSpec((None, None, 2, SUBLANES, sw), lambda bb, dd, i: (bb, dd, 0, 0, 0))],
        out_specs=[pl.BlockSpec((None, None, tm, d), lambda bb, dd, i: (dd, bb, tile(dd, i), 0)),
                   pl.BlockSpec((None, None, 2, SUBLANES, sw), lambda bb, dd, i: (bb, dd, 0, 0, 0))],
        out_shape=[jax.ShapeDtypeStruct((2, b, l, d), F32),
                   jax.ShapeDtypeStruct((b, 2, 2, SUBLANES, sw), F32)],
        scratch_shapes=[pltpu.VMEM((nj, SUBLANES * pitch, LANES), F32),
                        pltpu.VMEM((nj, SUBLANES * pitch, LANES), F32),
                        pltpu.VMEM((2, SUBLANES, sw), F32)],
        compiler_params=_cparams(("arbitrary", "arbitrary", "arbitrary")),
        name="s5_scan",
    )(x, mod3, mod3, wb, wc, lam, x0)


def _s5_weights(lam_re, lam_im, log_step, b_re, b_im, c_re, c_im):
    lam = lax.complex(lam_re.astype(F32), lam_im.astype(F32))
    step = jnp.exp(log_step.astype(F32))[..., None]
    lam_bar = jnp.exp(lam * step)
    b_bar = ((lam_bar - 1.0) / lam)[..., None] * lax.complex(b_re.astype(F32), b_im.astype(F32))
    ng, p = lam.shape[1], lam.shape[2]
    gl = ng // SUBLANES
    eye = jnp.eye(gl, dtype=F32)

    def wb_of(part):
        t = part.reshape(2, SUBLANES, gl, p, S5_GROUP)
        w = jnp.einsum('dsgpi,gh->dsgihp', t, eye)
        return w.reshape(2, SUBLANES, gl * S5_GROUP, gl * p)

    def wc_of(part):
        t = part.reshape(2, SUBLANES, gl, S5_GROUP, p)
        w = jnp.einsum('dsgip,gh->dshpgi', t, eye)
        return w.reshape(2, SUBLANES, gl * p, gl * S5_GROUP)

    wb = jnp.concatenate([wb_of(jnp.real(b_bar)), wb_of(jnp.imag(b_bar))], axis=-1).astype(BF16)
    wc = jnp.concatenate([wc_of(c_re.astype(F32)), -wc_of(c_im.astype(F32))], axis=-2).astype(BF16)
    lam_t = jnp.stack([jnp.real(lam_bar).reshape(2, SUBLANES, gl * p),
                       jnp.imag(lam_bar).reshape(2, SUBLANES, gl * p)], axis=1)
    return wb, wc, lam_t


def _s5_glu_kernel(x_ref, sh_ref, sc_ref, g1_ref, yf_ref, yb_ref, dsk_ref, w1_ref, b1_ref, w2_ref, b2_ref, o_ref):
    x = x_ref[...]
    h = _rms(x) * (1.0 + sc_ref[0]) + sh_ref[0]
    y = dsk_ref[...] * h + yf_ref[...] + yb_ref[...]
    z = jax.nn.gelu(y).astype(BF16)
    o = (_dot(z, w1_ref[...]) + b1_ref[...]) * jax.nn.sigmoid(_dot(z, w2_ref[...]) + b2_ref[...])
    o_ref[...] = x + g1_ref[0] * o


def _s5_glu(x, mod3, layer, row_fn, y2, d_skip, w1, b1, w2, b2):
    b, l, d = x.shape
    tm = min(TOKEN_TILE, l)
    mspec = lambda which: pl.BlockSpec((1, 1, d), _mod_spec(layer, row_fn, which))
    const = lambda shape: pl.BlockSpec(shape, lambda bb, t: (0,) * len(shape))
    return pl.pallas_call(
        _s5_glu_kernel,
        grid=(b, l // tm),
        in_specs=[pl.BlockSpec((None, tm, d), lambda bb, t: (bb, t, 0)),
                  mspec(MOD_SH1), mspec(MOD_SC1), mspec(MOD_G1),
                  pl.BlockSpec((None, None, tm, d), lambda bb, t: (0, bb, t, 0)),
                  pl.BlockSpec((None, None, tm, d), lambda bb, t: (1, bb, t, 0)),
                  const((1, d)), const((d, d)), const((1, d)), const((d, d)), const((1, d))],
        out_specs=pl.BlockSpec((None, tm, d), lambda bb, t: (bb, t, 0)),
        out_shape=jax.ShapeDtypeStruct((b, l, d), F32),
        compiler_params=_cparams(("arbitrary", "arbitrary")),
        name="s5_glu",
    )(x, mod3, mod3, mod3, y2, y2, d_skip.astype(F32).reshape(1, d), w1.astype(BF16), b1.astype(F32).reshape(1, d),
      w2.astype(BF16), b2.astype(F32).reshape(1, d))


def _route_kernel(x_ref, sh_ref, sc_ref, rw_ref, rb_ref, cin_ref, tri_ref, hp_ref, meta_ref, cnt_ref, carry):
    first = (pl.program_id(0) == 0) & (pl.program_id(1) == 0)

    @pl.when(first)
    def _():
        carry[...] = cin_ref[...]

    h = _modulated(x_ref, sh_ref, sc_ref)
    half = h.shape[1] // 2
    lo = pltpu.bitcast(h[:, :half].astype(BF16).astype(F32), U32)
    hi = pltpu.bitcast(h[:, half:].astype(BF16).astype(F32), U32)
    packed = (lo >> 16) | (hi & jnp.uint32(0xFFFF0000))
    pieces = half // LANES
    for j in range(pieces):
        hp_ref[pl.ds(j, h.shape[0], stride=pieces), :] = packed[:, j * LANES:(j + 1) * LANES]

    logits = _dot_hi(h, rw_ref[...]) + rb_ref[...]
    lane = lax.broadcasted_iota(I32, logits.shape, 1)
    cur = jnp.where(lane < N_EXPERTS, logits, -jnp.inf)
    vals, idxs = [], []
    hot = jnp.zeros(logits.shape, F32)
    for _ in range(TOP_K):
        m = jnp.max(cur, axis=-1, keepdims=True)
        idx = jnp.min(jnp.where(cur == m, lane, LANES), axis=-1, keepdims=True)
        sel = lane == idx
        hot = hot + sel.astype(F32)
        cur = jnp.where(sel, -jnp.inf, cur)
        vals.append(m)
        idxs.append(idx)
    es = [jnp.exp(v - vals[0]) for v in vals]
    tot = es[0] + es[1] + es[2] + es[3]
    before = _dot(tri_ref[...], hot.astype(BF16)) + carry[...]
    meta = jnp.zeros(logits.shape, F32)
    for k in range(TOP_K):
        rank = jnp.sum(jnp.where(lane == idxs[k], before, 0.0), axis=-1, keepdims=True)
        meta = jnp.where(lane == k, idxs[k].astype(F32), meta)
        meta = jnp.where(lane == TOP_K + k, es[k] / tot, meta)
        meta = jnp.where(lane == 2 * TOP_K + k, rank, meta)
    meta_ref[...] = meta
    carry[...] = carry[...] + jnp.sum(hot, axis=0, keepdims=True)
    cnt_ref[...] = carry[...]


def _route(x, mod3, layer, row_fn, rw_pad, rb_pad, counts_in):
    b, l, d = x.shape
    tm = min(TOKEN_TILE, l)
    idx = jnp.arange(tm)
    tri = (idx[:, None] > idx[None, :]).astype(BF16)
    mspec = lambda which: pl.BlockSpec((1, 1, d), _mod_spec(layer, row_fn, which))
    const = lambda shape: pl.BlockSpec(shape, lambda bb, t: (0,) * len(shape))
    return pl.pallas_call(
        _route_kernel,
        grid=(b, l // tm),
        in_specs=[pl.BlockSpec((None, tm, d), lambda bb, t: (bb, t, 0)),
                  mspec(MOD_SH2), mspec(MOD_SC2),
                  const((d, LANES)), const((1, LANES)), const((1, LANES)), const((tm, tm))],
        out_specs=[pl.BlockSpec((None, tm * (d // 2 // LANES), LANES), lambda bb, t: (bb, t, 0)),
                   pl.BlockSpec((None, tm, LANES), lambda bb, t: (bb, t, 0)),
                   const((1, LANES))],
        out_shape=[jax.ShapeDtypeStruct((b, l * (d // 2 // LANES), LANES), U32),
                   jax.ShapeDtypeStruct((b, l, LANES), F32),
                   jax.ShapeDtypeStruct((1, LANES), F32)],
        scratch_shapes=[pltpu.VMEM((1, LANES), F32)],
        compiler_params=_cparams(("arbitrary", "arbitrary")),
        name="moe_route",
    )(x, mod3, mod3, rw_pad, rb_pad, counts_in, tri)


def _dest_kernel(meta_ref, ps_ref, o_ref):
    meta = meta_ref[...]
    lane = lax.broadcasted_iota(I32, meta.shape, 1)
    ps = ps_ref[...]
    dst = jnp.zeros(meta.shape, F32)
    for k in range(TOP_K):
        idx = meta[:, k:k + 1].astype(I32)
        base = jnp.sum(jnp.where(lane == idx, ps, 0.0), axis=-1, keepdims=True)
        dst = jnp.where(lane == k, base + meta[:, 2 * TOP_K + k:2 * TOP_K + k + 1], dst)
    dt = dst.T
    o_ref[0] = jnp.concatenate([dt[k:k + 1] for k in range(TOP_K)], axis=1).astype(I32)


def _dest(meta, pad_start_row):
    b, l, _ = meta.shape
    tt = min(TOKEN_TILE, l)
    nt = l // tt
    return pl.pallas_call(
        _dest_kernel,
        grid=(b, nt),
        in_specs=[pl.BlockSpec((None, tt, LANES), lambda bb, t: (bb, t, 0)),
                  pl.BlockSpec((1, LANES), lambda bb, t: (0, 0))],
        out_specs=pl.BlockSpec((1, 1, TOP_K * tt), lambda bb, t: (bb * nt + t, 0, 0)),
        out_shape=jax.ShapeDtypeStruct((b * nt, 1, TOP_K * tt), I32),
        compiler_params=_cparams(("arbitrary", "arbitrary")),
        name="moe_dest",
    )(meta, pad_start_row)


def _dispatch_kernel(dest_hbm, hp_ref, xs_in, xs_out, dest_smem, sem_i, sem, *, tt, pieces):
    del xs_in
    i = pl.program_id(0)
    cp = pltpu.make_async_copy(dest_hbm.at[i, 0], dest_smem, sem_i)
    cp.start()
    cp.wait()

    def row_copy(r, k):
        src = pl.multiple_of(r * pieces, pieces)
        dst = pl.multiple_of(dest_smem[k * tt + r] * pieces, pieces)
        return pltpu.make_async_copy(hp_ref.at[pl.ds(src, pieces)], xs_out.at[pl.ds(dst, pieces)], sem)

    def issue(r, c):
        for k in range(TOP_K):
            row_copy(r, k).start()
        return c

    lax.fori_loop(0, tt, issue, 0, unroll=4)

    def drain(r, c):
        for k in range(TOP_K):
            row_copy(r, k).wait()
        return c

    lax.fori_loop(0, tt, drain, 0, unroll=4)


def _dispatch(dest, hp, xs, pieces):
    n = hp.shape[0] // pieces
    tt = dest.shape[-1] // TOP_K
    return pl.pallas_call(
        functools.partial(_dispatch_kernel, tt=tt, pieces=pieces),
        grid=(n // tt,),
        in_specs=[pl.BlockSpec(memory_space=pl.ANY),
                  pl.BlockSpec((tt * pieces, LANES), lambda i: (i, 0)),
                  pl.BlockSpec(memory_space=pl.ANY)],
        out_specs=pl.BlockSpec(memory_space=pl.ANY),
        out_shape=jax.ShapeDtypeStruct(xs.shape, xs.dtype),
        scratch_shapes=[pltpu.SMEM((tt * TOP_K,), I32), pltpu.SemaphoreType.DMA, pltpu.SemaphoreType.DMA],
        input_output_aliases={2: 0},
        compiler_params=_cparams(("arbitrary",)),
        name="moe_dispatch",
    )(dest, hp, xs)


def _expert_kernel(be_ref, nact_ref, xs_ref, wgu_ref, bgu_ref, wd_ref, bd_ref, y_ref, wgu_bf, wd_bf):
    i = pl.program_id(0)
    e = be_ref[i]
    prev = be_ref[jnp.maximum(i - 1, 0)]

    @pl.when((i == 0) | (e != prev))
    def _():
        wgu_bf[...] = wgu_ref[0].astype(BF16)
        wd_bf[...] = wd_ref[0].astype(BF16)

    rows = MOE_ROWS
    in_pieces = xs_ref.shape[0] // rows
    out_pieces = y_ref.shape[0] // rows

    @pl.when(i < nact_ref[0])
    def _():
        ws = [xs_ref[pl.ds(j, rows, stride=in_pieces), :] for j in range(in_pieces)]
        lo = [pltpu.bitcast(w << 16, F32) for w in ws]
        hi = [pltpu.bitcast(w & jnp.uint32(0xFFFF0000), F32) for w in ws]
        x = jnp.concatenate(lo + hi, axis=1).astype(BF16)
        gu = _dot(x, wgu_bf[...]) + bgu_ref[0]
        de = gu.shape[1] // 2
        gate = jnp.minimum(gu[:, :de], SWIGLU_LIMIT)
        up = jnp.clip(gu[:, de:], -SWIGLU_LIMIT, SWIGLU_LIMIT)
        act = (up + 1.0) * gate * jax.nn.sigmoid(SWIGLU_ALPHA * gate)
        y = _dot(act.astype(BF16), wd_bf[...]) + bd_ref[0]
        for j in range(out_pieces):
            y_ref[pl.ds(j, rows, stride=out_pieces), :] = y[:, j * LANES:(j + 1) * LANES]

    @pl.when(i >= nact_ref[0])
    def _():
        y_ref[...] = jnp.zeros(y_ref.shape, y_ref.dtype)


def _experts(block_e, n_active, xs, w_gu, b_gu, w_d, b_d, layer):
    _, ne, d, de2 = w_gu.shape
    in_pieces = d // 2 // LANES
    out_pieces = d // LANES
    n_rows = xs.shape[0] // in_pieces
    nblk = n_rows // MOE_ROWS
    grid_spec = pltpu.PrefetchScalarGridSpec(
        num_scalar_prefetch=2,
        grid=(nblk,),
        in_specs=[pl.BlockSpec((MOE_ROWS * in_pieces, LANES), lambda i, be, na: (i, 0)),
                  pl.BlockSpec((None, 1, d, de2), lambda i, be, na: (layer, be[i], 0, 0)),
                  pl.BlockSpec((None, 1, 1, de2), lambda i, be, na: (layer, be[i], 0, 0)),
                  pl.BlockSpec((None, 1, de2 // 2, d), lambda i, be, na: (layer, be[i], 0, 0)),
                  pl.BlockSpec((None, 1, 1, d), lambda i, be, na: (layer, be[i], 0, 0))],
        out_specs=pl.BlockSpec((MOE_ROWS * out_pieces, LANES), lambda i, be, na: (i, 0)),
        scratch_shapes=[pltpu.VMEM((d, de2), BF16), pltpu.VMEM((de2 // 2, d), BF16)],
    )
    depth = w_gu.shape[0]
    return pl.pallas_call(
        _expert_kernel,
        grid_spec=grid_spec,
        out_shape=jax.ShapeDtypeStruct((n_rows * out_pieces, LANES), F32),
        compiler_params=_cparams(("arbitrary",)),
        name="moe_experts",
    )(block_e, n_active, xs, w_gu, b_gu.reshape(depth, ne, 1, de2), w_d, b_d.reshape(depth, ne, 1, d))


def _combine_kernel(*refs, tt, final):
    if final:
        dest_hbm, yb_hbm, x_ref, meta_ref, g2_ref, fw_ref, o_ref, dest_smem, buf, sem_i, sem = refs
    else:
        dest_hbm, yb_hbm, x_ref, meta_ref, g2_ref, o_ref, dest_smem, buf, sem_i, sem = refs
    i = pl.program_id(0) * pl.num_programs(1) + pl.program_id(1)
    cp = pltpu.make_async_copy(dest_hbm.at[i, 0], dest_smem, sem_i)
    cp.start()
    cp.wait()

    pieces = buf.shape[1] // tt

    def row_copy(r, k):
        src = pl.multiple_of(dest_smem[k * tt + r] * pieces, pieces)
        dst = pl.multiple_of(r * pieces, pieces)
        return pltpu.make_async_copy(yb_hbm.at[pl.ds(src, pieces)], buf.at[k, pl.ds(dst, pieces)], sem)

    def issue(r, c):
        for k in range(TOP_K):
            row_copy(r, k).start()
        return c

    lax.fori_loop(0, tt, issue, 0, unroll=4)

    def drain(r, c):
        for k in range(TOP_K):
            row_copy(r, k).wait()
        return c

    lax.fori_loop(0, tt, drain, 0, unroll=4)
    meta = meta_ref[...]
    g2 = g2_ref[0]
    cols = []
    for j in range(pieces):
        y = meta[:, TOP_K:TOP_K + 1] * buf[0, pl.ds(j, tt, stride=pieces), :]
        for k in range(1, TOP_K):
            y = y + meta[:, TOP_K + k:TOP_K + k + 1] * buf[k, pl.ds(j, tt, stride=pieces), :]
        sl = slice(j * LANES, (j + 1) * LANES)
        cols.append(x_ref[:, sl] + g2[:, sl] * y)
    xn = jnp.concatenate(cols, axis=1)
    if final:
        xn = _rms(xn) * fw_ref[...]
    o_ref[...] = xn


def _combine(dest, yb, x, meta, mod3, layer, row_fn, final_w=None):
    b, l, d = x.shape
    tt = min(TOKEN_TILE, l)
    nt = l // tt
    final = final_w is not None
    in_specs = [pl.BlockSpec(memory_space=pl.ANY), pl.BlockSpec(memory_space=pl.ANY),
                pl.BlockSpec((None, tt, d), lambda bb, t: (bb, t, 0)),
                pl.BlockSpec((None, tt, LANES), lambda bb, t: (bb, t, 0)),
                pl.BlockSpec((1, 1, d), _mod_spec(layer, row_fn, MOD_G2))]
    args = [dest, yb, x, meta, mod3]
    if final:
        in_specs.append(pl.BlockSpec((1, d), lambda bb, t: (0, 0)))
        args.append(final_w.astype(F32).reshape(1, d))
    return pl.pallas_call(
        functools.partial(_combine_kernel, tt=tt, final=final),
        grid=(b, nt),
        in_specs=in_specs,
        out_specs=pl.BlockSpec((None, tt, d), lambda bb, t: (bb, t, 0)),
        out_shape=jax.ShapeDtypeStruct((b, l, d), F32),
        scratch_shapes=[pltpu.SMEM((tt * TOP_K,), I32), pltpu.VMEM((TOP_K, tt * (d // LANES), LANES), F32),
                        pltpu.SemaphoreType.DMA, pltpu.SemaphoreType.DMA],
        compiler_params=_cparams(("arbitrary", "arbitrary")),
        name="moe_combine",
    )(*args)


def _moe(parts, mod3, layer, router_w, router_b, w_gu, b_gu, w_d, b_d, final_w=None):
    d = router_w.shape[0]
    rw_pad = jnp.zeros((d, LANES), F32).at[:, :N_EXPERTS].set(router_w.astype(F32))
    rb_pad = jnp.zeros((1, LANES), F32).at[0, :N_EXPERTS].set(router_b.astype(F32))
    counts = jnp.zeros((1, LANES), F32)
    routed = []
    for x, row_fn in parts:
        hp, meta, counts = _route(x, mod3, layer, row_fn, rw_pad, rb_pad, counts)
        routed.append((hp, meta))
    n_assign = sum(x.shape[0] * x.shape[1] for x, _ in parts) * TOP_K
    nblk = -(-(n_assign + N_EXPERTS * (MOE_ROWS - 1)) // MOE_ROWS)
    cnt = counts[0, :N_EXPERTS].astype(I32)
    padded = (cnt + MOE_ROWS - 1) // MOE_ROWS * MOE_ROWS
    pad_end = jnp.cumsum(padded)
    pad_start = pad_end - padded
    blk_row = jnp.arange(nblk, dtype=I32)[:, None] * MOE_ROWS
    block_e = jnp.minimum(jnp.sum((pad_end[None, :] <= blk_row).astype(I32), axis=1), N_EXPERTS - 1).astype(I32)
    n_active = (pad_end[-1] // MOE_ROWS).astype(I32).reshape(1)
    ps_row = jnp.zeros((1, LANES), F32).at[0, :N_EXPERTS].set(pad_start.astype(F32))
    pieces = d // 2 // LANES
    xs = jnp.zeros((nblk * MOE_ROWS * pieces, LANES), U32)
    dests = []
    for (x, _), (hp, meta) in zip(parts, routed):
        dest = _dest(meta, ps_row)
        dests.append(dest)
        xs = _dispatch(dest, hp.reshape(-1, LANES), xs, pieces)
    yb = _experts(block_e, n_active, xs, w_gu, b_gu, w_d, b_d, layer)
    outs = []
    for idx, ((x, row_fn), (hp, meta)) in enumerate(zip(parts, routed)):
        fw = final_w if idx == 0 else None
        outs.append(_combine(dests[idx], yb, x, meta, mod3, layer, row_fn, fw))
    return outs


def kernel(x, c, ctx, c_ctx, ada_w, ada_b, ab_w_in, ab_q_gain, ab_k_gain, dn_conv_w, dn_a_log, dn_dt_bias, dn_norm_w, ab_w_out, s5_lambda_re, s5_lambda_im, s5_log_step, s5_b_re, s5_b_im, s5_c_re, s5_c_im, s5_d, s5_glu_w1, s5_glu_b1, s5_glu_w2, s5_glu_b2, moe_router_w, moe_router_b, moe_w_gate_up, moe_b_gate_up, moe_w_down, moe_b_down, final_norm_w):
    b, l, d = x.shape
    lc = ctx.shape[1]
    depth = ada_w.shape[0]
    assert depth == 2 and b < MOD_ROWS
    lat_row = lambda bb: bb
    ctx_row = lambda bb: b

    rows = jnp.zeros((MOD_ROWS, d), F32).at[:b].set(c.astype(F32)).at[b].set(c_ctx.astype(F32))
    mod3 = _ada(rows, ada_w, ada_b).reshape(depth * MOD_ROWS * 6, 1, d)

    w_in = ab_w_in[0]
    aq, akv, bqk = Q_HEADS * HEAD_DIM, KV_HEADS * HEAD_DIM, DN_HEADS * DN_DK
    o_k, o_v, o_dn, o_ab, o_z = aq, aq + akv, aq + 2 * akv, aq + 2 * akv + 3 * bqk, aq + 2 * akv + 3 * bqk + 4 * DN_HEADS
    w_perm = jnp.concatenate([w_in[:, :aq], w_in[:, o_dn:o_ab], w_in[:, o_z:], w_in[:, o_k:o_v], w_in[:, o_v:o_dn],
                              w_in[:, o_ab:o_z], jnp.zeros((d, IN_COLS_PAD - w_in.shape[1]), w_in.dtype)], axis=1).astype(BF16)
    proj_l = _inproj(x, mod3, 0, lat_row, w_perm)
    proj_c = _inproj(ctx, mod3, 0, ctx_row, w_perm)

    cos, sin = _rope_tables(l)
    ones = jnp.ones((lc, KV_HEADS * HEAD_DIM), F32)
    q_l, kt_l, v_l = _qkprep(proj_l, ab_q_gain[0], ab_k_gain[0], cos, sin, min(ATTN_K_TILE, l))
    q_c, kt_c, v_c = _qkprep(proj_c, ab_q_gain[0], ab_k_gain[0], ones, jnp.zeros_like(ones), lc)
    a_l = _attention(q_l, kt_c, v_c, kt_l, v_l)
    a_c = _attention(q_c, kt_c, v_c)

    dq_c, dk_c, dkt_c, dv_c, gc_c, gct_c = _dnprep(proj_c, dn_conv_w[0], dn_a_log[0], dn_dt_bias[0])
    dq_l, dk_l, dkt_l, dv_l, gc_l, gct_l = _dnprep(proj_l, dn_conv_w[0], dn_a_log[0], dn_dt_bias[0])
    s_zero = jnp.zeros((b, 2, DN_HEADS, DN_DK, DN_DK), F32)
    of_c, ob_c, s_ctx = _dnscan(dq_c, dk_c, dkt_c, dv_c, gc_c, gct_c, s_zero)
    of_l, ob_l, _ = _dnscan(dq_l, dk_l, dkt_l, dv_l, gc_l, gct_l, s_ctx)

    w_out = ab_w_out[0].astype(BF16)
    x = _mix0_out(x, mod3, 0, lat_row, a_l, of_l, ob_l, proj_l, dn_norm_w[0], w_out)
    ctx = _mix0_out(ctx, mod3, 0, ctx_row, a_c, of_c, ob_c, proj_c, dn_norm_w[0], w_out)

    x, ctx = _moe([(x, lat_row), (ctx, ctx_row)], mod3, 0, moe_router_w[0], moe_router_b[0],
                  moe_w_gate_up, moe_b_gate_up, moe_w_down, moe_b_down)

    wb, wc, lam_t = _s5_weights(s5_lambda_re[0], s5_lambda_im[0], s5_log_step[0], s5_b_re[0], s5_b_im[0],
                                s5_c_re[0], s5_c_im[0])
    sw = lam_t.shape[-1]
    x0 = jnp.zeros((b, 2, 2, SUBLANES, sw), F32)
    _, x_ctx = _s5_scan(ctx, mod3, 1, ctx_row, wb, wc, lam_t, x0)
    y2, _ = _s5_scan(x, mod3, 1, lat_row, wb, wc, lam_t, x_ctx)
    x = _s5_glu(x, mod3, 1, lat_row, y2, s5_d[0], s5_glu_w1[0], s5_glu_b1[0], s5_glu_w2[0], s5_glu_b2[0])

    (x,) = _moe([(x, lat_row)], mod3, 1, moe_router_w[1], moe_router_b[1], moe_w_gate_up, moe_b_gate_up,
                moe_w_down, moe_b_down, final_w=final_norm_w)
    return x
```

```python
import functools
import math

import jax
import jax.numpy as jnp
from jax import lax
from jax.experimental import pallas as pl
from jax.experimental.pallas import tpu as pltpu

F32 = jnp.float32
BF16 = jnp.bfloat16
U32 = jnp.uint32
I32 = jnp.int32

EPS = 1e-6
GRID_W = 64
HEAD_DIM = 64
Q_HEADS = 8
KV_HEADS = 2
GROUP = Q_HEADS // KV_HEADS
ROPE_THETA = 10000.0
DN_HEADS = 8
DN_DK = 64
DN_CONV = 5
DN_CHUNK = 64
S5_GROUP = 16
S5_STATE = 64
N_EXPERTS = 32
TOP_K = 4
SWIGLU_LIMIT = 7.0
SWIGLU_ALPHA = 1.702

LANES = 128
SUBLANES = 8
TOKEN_TILE = 256
ATTN_Q_TILE = 256
ATTN_K_TILE = 2048
DN_BLOCK_CHUNKS = 4
MOE_ROWS = 512
S5_PITCH_PAD = 4
VMEM_LIMIT = 56 * 1024 * 1024

MOD_SH1, MOD_SC1, MOD_G1, MOD_SH2, MOD_SC2, MOD_G2 = range(6)
MOD_ROWS = 8

COL_Q = 0
COL_DNQ = 512
COL_DNK = 1024
COL_DNV = 1536
COL_Z = 2048
COL_K = 2560
COL_V = 2688
COL_AB = 2816
IN_COLS_PAD = 2944


def _cparams(sem):
    return pltpu.CompilerParams(dimension_semantics=sem, vmem_limit_bytes=VMEM_LIMIT)


def _split3(a):
    a1 = a.astype(BF16)
    r = a - a1.astype(F32)
    a2 = r.astype(BF16)
    a3 = (r - a2.astype(F32)).astype(BF16)
    return a1, a2, a3


def _dot(a, b):
    return jnp.dot(a, b, preferred_element_type=F32)


def _dot_exact_lhs(m_bf16, x):
    x1, x2, x3 = _split3(x)
    return _dot(m_bf16, x1) + _dot(m_bf16, x2) + _dot(m_bf16, x3)


def _dot_exact_rhs(x, m_bf16):
    x1, x2, x3 = _split3(x)
    return _dot(x1, m_bf16) + _dot(x2, m_bf16) + _dot(x3, m_bf16)


def _dot_hi(a, b):
    a1, a2, _ = _split3(a)
    b1, b2, _ = _split3(b)
    return _dot(a1, b1) + _dot(a1, b2) + _dot(a2, b1)


def _rms(x):
    return x * lax.rsqrt(jnp.mean(x * x, axis=-1, keepdims=True) + EPS)


def _modulated(x_ref, sh_ref, sc_ref):
    return _rms(x_ref[...]) * (1.0 + sc_ref[0]) + sh_ref[0]


def _mod_spec(layer, row_fn, which):
    d = None

    def imap(b, t):
        return ((layer * MOD_ROWS + row_fn(b)) * 6 + which, 0, 0)

    return imap


def _ada_kernel(a_ref, w_ref, b_ref, o_ref):
    a = a_ref[...]
    a = a * jax.nn.sigmoid(a)
    o_ref[0] = _dot_hi(a, w_ref[0]) + b_ref[0]


def _ada(rows, ada_w, ada_b):
    depth, d, n = ada_w.shape
    tn = 1536
    return pl.pallas_call(
        _ada_kernel,
        grid=(depth, n // tn),
        in_specs=[pl.BlockSpec((MOD_ROWS, d), lambda l, j: (0, 0)),
                  pl.BlockSpec((1, d, tn), lambda l, j: (l, 0, j)),
                  pl.BlockSpec((1, 1, tn), lambda l, j: (l, 0, j))],
        out_specs=pl.BlockSpec((1, MOD_ROWS, tn), lambda l, j: (l, 0, j)),
        out_shape=jax.ShapeDtypeStruct((depth, MOD_ROWS, n), F32),
        compiler_params=_cparams(("arbitrary", "arbitrary")),
        name="ada_mod",
    )(rows, ada_w, ada_b.reshape(depth, 1, n))


def _inproj_kernel(x_ref, sh_ref, sc_ref, w_ref, o_ref):
    h = _modulated(x_ref, sh_ref, sc_ref)
    o_ref[...] = _dot(h.astype(BF16), w_ref[...])


def _inproj(x, mod3, layer, row_fn, w_bf16):
    b, l, d = x.shape
    tm = min(TOKEN_TILE, l)
    n = w_bf16.shape[1]
    mspec = lambda which: pl.BlockSpec((1, 1, d), _mod_spec(layer, row_fn, which))
    return pl.pallas_call(
        _inproj_kernel,
        grid=(b, l // tm),
        in_specs=[pl.BlockSpec((None, tm, d), lambda bb, t: (bb, t, 0)),
                  mspec(MOD_SH1), mspec(MOD_SC1),
                  pl.BlockSpec((d, n), lambda bb, t: (0, 0))],
        out_specs=pl.BlockSpec((None, tm, n), lambda bb, t: (bb, t, 0)),
        out_shape=jax.ShapeDtypeStruct((b, l, n), F32),
        compiler_params=_cparams(("arbitrary", "arbitrary")),
        name="in_proj",
    )(x, mod3, mod3, w_bf16)


def _head_rot(x, width):
    lane = lax.broadcasted_iota(I32, x.shape, 1)
    first = (lane % HEAD_DIM) < (HEAD_DIM // 2)
    return jnp.where(first, pltpu.roll(x, width - HEAD_DIM // 2, 1), pltpu.roll(x, HEAD_DIM // 2, 1))


def _qkprep_kernel(q_ref, k_ref, v_ref, qg_ref, kg_ref, cos_ref, sin_ref, gq_ref, gk_ref, qo_ref, kt_ref, vo_ref):
    cos = cos_ref[...]
    sin = sin_ref[...]
    q = q_ref[...]
    qn = q * lax.rsqrt(_dot_exact_rhs(q * q, gq_ref[...]) + EPS) * qg_ref[...]
    cos4 = jnp.concatenate([cos] * 4, axis=1)
    sin4 = jnp.concatenate([sin] * 4, axis=1)
    qr = qn * cos4 + _head_rot(qn, Q_HEADS * HEAD_DIM) * sin4
    qo_ref[...] = (qr * (HEAD_DIM ** -0.5 * math.log2(math.e))).astype(BF16)
    k = k_ref[...]
    kn = k * lax.rsqrt(_dot_exact_rhs(k * k, gk_ref[...]) + EPS) * kg_ref[...]
    kr = kn * cos + _head_rot(kn, KV_HEADS * HEAD_DIM) * sin
    kt = kr.T
    kt_ref[0] = kt[:HEAD_DIM].astype(BF16)
    kt_ref[1] = kt[HEAD_DIM:].astype(BF16)
    v = v_ref[...]
    ones = jnp.ones((v.shape[0], HEAD_DIM), F32)
    vo_ref[0] = jnp.concatenate([v[:, :HEAD_DIM], ones], axis=1).astype(BF16)
    vo_ref[1] = jnp.concatenate([v[:, HEAD_DIM:], ones], axis=1).astype(BF16)


def _group_matrix(width, scale):
    g = jnp.arange(width) // HEAD_DIM
    return ((g[:, None] == g[None, :]).astype(F32) * scale).astype(BF16)


def _qkprep(proj, q_gain, k_gain, cos, sin, tk):
    b, l, _ = proj.shape
    qw = Q_HEADS * HEAD_DIM
    kw = KV_HEADS * HEAD_DIM
    qg = jnp.tile(q_gain.astype(F32), Q_HEADS).reshape(1, qw)
    kg = jnp.tile(k_gain.astype(F32), KV_HEADS).reshape(1, kw)
    const = lambda shape: pl.BlockSpec(shape, lambda bb, t: (0,) * len(shape))
    return pl.pallas_call(
        _qkprep_kernel,
        grid=(b, l // tk),
        in_specs=[pl.BlockSpec((None, tk, qw), lambda bb, t: (bb, t, COL_Q // qw)),
                  pl.BlockSpec((None, tk, kw), lambda bb, t: (bb, t, COL_K // kw)),
                  pl.BlockSpec((None, tk, kw), lambda bb, t: (bb, t, COL_V // kw)),
                  const((1, qw)), const((1, kw)),
                  pl.BlockSpec((tk, kw), lambda bb, t: (t, 0)),
                  pl.BlockSpec((tk, kw), lambda bb, t: (t, 0)),
                  const((qw, qw)), const((kw, kw))],
        out_specs=[pl.BlockSpec((None, tk, qw), lambda bb, t: (bb, t, 0)),
                   pl.BlockSpec((None, KV_HEADS, None, HEAD_DIM, tk), lambda bb, t: (bb, 0, t, 0, 0)),
                   pl.BlockSpec((None, KV_HEADS, tk, 2 * HEAD_DIM), lambda bb, t: (bb, 0, t, 0))],
        out_shape=[jax.ShapeDtypeStruct((b, l, qw), BF16),
                   jax.ShapeDtypeStruct((b, KV_HEADS, l // tk, HEAD_DIM, tk), BF16),
                   jax.ShapeDtypeStruct((b, KV_HEADS, l, 2 * HEAD_DIM), BF16)],
        compiler_params=_cparams(("arbitrary", "arbitrary")),
        name="qk_prep",
    )(proj, proj, proj, qg, kg, cos, sin, _group_matrix(qw, 1.0 / HEAD_DIM), _group_matrix(kw, 1.0 / HEAD_DIM))


def _rope_tables(length):
    rows = length // GRID_W
    row = jnp.broadcast_to(jnp.arange(rows, dtype=I32)[:, None], (rows, GRID_W)).reshape(-1).astype(F32)
    col = jnp.broadcast_to(jnp.arange(GRID_W, dtype=I32)[None, :], (rows, GRID_W)).reshape(-1).astype(F32)
    n_axis = HEAD_DIM // 4
    inv_freq = ROPE_THETA ** (-jnp.arange(n_axis, dtype=F32) / n_axis)
    ang = jnp.concatenate([row[:, None] * inv_freq, col[:, None] * inv_freq], axis=-1)
    c, s = jnp.cos(ang), jnp.sin(ang)
    cos_h = jnp.concatenate([c, c], axis=-1)
    sin_h = jnp.concatenate([-s, s], axis=-1)
    return jnp.tile(cos_h, (1, KV_HEADS)), jnp.tile(sin_h, (1, KV_HEADS))


def _attn_kernel(*refs, tq, n_lat_chunks, has_lat):
    if has_lat:
        q_ref, ktc_ref, vc_ref, ktl_ref, vl_ref, o_ref = refs
    else:
        q_ref, ktc_ref, vc_ref, o_ref = refs
    q = q_ref[...]
    qs = jnp.concatenate([q[:, g * HEAD_DIM:(g + 1) * HEAD_DIM] for g in range(GROUP)], axis=0)
    rows = GROUP * tq

    def step(kt, v_ones, m, acc):
        s = _dot(qs, kt)
        m_new = jnp.maximum(m, jnp.max(s, axis=-1, keepdims=True))
        p = jnp.exp2((s - m_new).astype(BF16))
        acc = jnp.exp2(m - m_new) * acc + _dot(p, v_ones)
        return m_new, acc

    m0 = jnp.full((rows, 1), -1e30, F32)
    a0 = jnp.zeros((rows, 2 * HEAD_DIM), F32)
    carry = step(ktc_ref[0], vc_ref[...], m0, a0)
    if has_lat:
        tk = ktl_ref.shape[-1]

        def body(j, c):
            off = pl.multiple_of(j * tk, tk)
            return step(ktl_ref[j], vl_ref[pl.ds(off, tk), :], *c)

        carry = lax.fori_loop(0, n_lat_chunks, body, carry, unroll=2)
    _, acc = carry
    o = acc[:, :HEAD_DIM] / acc[:, HEAD_DIM:HEAD_DIM + 1]
    o_ref[...] = jnp.concatenate([o[g * tq:(g + 1) * tq] for g in range(GROUP)], axis=1).astype(o_ref.dtype)


def _attention(q, ktc, vc, ktl=None, vl=None):
    b, lq, qw = q.shape
    tq = min(ATTN_Q_TILE, lq)
    gw = GROUP * HEAD_DIM
    lc = vc.shape[2]
    has_lat = ktl is not None
    in_specs = [pl.BlockSpec((None, tq, gw), lambda bb, h, i: (bb, i, h)),
                pl.BlockSpec((None, None, 1, HEAD_DIM, lc), lambda bb, h, i: (bb, h, 0, 0, 0)),
                pl.BlockSpec((None, None, lc, 2 * HEAD_DIM), lambda bb, h, i: (bb, h, 0, 0))]
    args = [q, ktc, vc]
    n_chunks = 0
    if has_lat:
        n_chunks, tk = ktl.shape[2], ktl.shape[4]
        ll = vl.shape[2]
        in_specs += [pl.BlockSpec((None, None, n_chunks, HEAD_DIM, tk), lambda bb, h, i: (bb, h, 0, 0, 0)),
                     pl.BlockSpec((None, None, ll, 2 * HEAD_DIM), lambda bb, h, i: (bb, h, 0, 0))]
        args += [ktl, vl]
    return pl.pallas_call(
        functools.partial(_attn_kernel, tq=tq, n_lat_chunks=n_chunks, has_lat=has_lat),
        grid=(b, KV_HEADS, lq // tq),
        in_specs=in_specs,
        out_specs=pl.BlockSpec((None, tq, gw), lambda bb, h, i: (bb, i, h)),
        out_shape=jax.ShapeDtypeStruct((b, lq, qw), BF16),
        compiler_params=_cparams(("arbitrary", "arbitrary", "arbitrary")),
        name="gqa_attention",
    )(*args)


def _dnprep_kernel(qp, qc, qn, kp, kc, kn, vp, vc, vn, ab_ref, cw_ref, alog_ref, dtb_ref, gm_ref, trip_ref, tris_ref,
                   qo_ref, ko_ref, kto_ref, vo_ref, gc_ref, gct_ref, *, tm):
    t = pl.program_id(1)
    nt = pl.num_programs(1)
    mp = (t > 0).astype(F32)
    mn = (t < nt - 1).astype(F32)
    n_ext = tm + 2 * SUBLANES
    pad = DN_CONV // 2

    def conv(p_ref, c_ref, n_ref, w):
        ext = jnp.concatenate([p_ref[...] * mp, c_ref[...], n_ref[...] * mn], axis=0)
        acc = None
        for j in range(DN_CONV):
            s = (pad - j) % n_ext
            r = ext if s == 0 else pltpu.roll(ext, s, 0)
            term = r[SUBLANES:SUBLANES + tm] * w[j:j + 1, :]
            acc = term if acc is None else acc + term
        return acc * jax.nn.sigmoid(acc)

    cw = cw_ref[...]
    hw = DN_HEADS * DN_DK
    gm = gm_ref[...]
    q = conv(qp, qc, qn, cw[:, 0:hw])
    q = q * lax.rsqrt(_dot_exact_rhs(q * q, gm) + EPS) * (DN_DK ** -0.5)
    k = conv(kp, kc, kn, cw[:, hw:2 * hw])
    k = k * lax.rsqrt(_dot_exact_rhs(k * k, gm) + EPS)
    v = conv(vp, vc, vn, cw[:, 2 * hw:3 * hw])
    kt = k.T
    for h in range(DN_HEADS):
        sl = slice(h * DN_DK, (h + 1) * DN_DK)
        qo_ref[h] = q[:, sl]
        ko_ref[h] = k[:, sl]
        vo_ref[h] = v[:, sl]
        for c in range(tm // DN_CHUNK):
            kto_ref[h, c] = kt[sl, c * DN_CHUNK:(c + 1) * DN_CHUNK]

    ab = ab_ref[...]
    lane = lax.broadcasted_iota(I32, ab.shape, 1)
    a = ab + dtb_ref[...]
    sp = jnp.maximum(a, 0.0) + jnp.log(1.0 + jnp.exp(-jnp.abs(a)))
    g = -jnp.exp(alog_ref[...]) * sp
    g = jnp.where(lane < 2 * DN_HEADS, g, 0.0)
    gpre = _dot_exact_lhs(trip_ref[...], g)
    gsuf = _dot_exact_lhs(tris_ref[...], g)
    gc = jnp.where(lane < DN_HEADS, gpre, jnp.where(lane < 2 * DN_HEADS, gsuf, jax.nn.sigmoid(ab)))
    gc_ref[...] = gc
    gt = gc.T
    for c in range(tm // DN_CHUNK):
        gct_ref[c] = gt[:4 * DN_HEADS, c * DN_CHUNK:(c + 1) * DN_CHUNK]


def _dnprep(proj, conv_w, a_log, dt_bias):
    b, l, _ = proj.shape
    tm = min(TOKEN_TILE, l)
    hw = DN_HEADS * DN_DK
    nc = l // DN_CHUNK
    r8 = tm // SUBLANES
    last8 = l // SUBLANES - 1

    def cur(col):
        return pl.BlockSpec((None, tm, hw), lambda bb, t: (bb, t, col // hw))

    def prev(col):
        return pl.BlockSpec((None, SUBLANES, hw), lambda bb, t: (bb, jnp.maximum(t * r8 - 1, 0), col // hw))

    def nxt(col):
        return pl.BlockSpec((None, SUBLANES, hw), lambda bb, t: (bb, jnp.minimum((t + 1) * r8, last8), col // hw))

    const = lambda shape: pl.BlockSpec(shape, lambda bb, t: (0,) * len(shape))
    alog = jnp.zeros((1, LANES), F32).at[0, :2 * DN_HEADS].set(a_log.astype(F32).reshape(-1))
    dtb = jnp.zeros((1, LANES), F32).at[0, :2 * DN_HEADS].set(dt_bias.astype(F32).reshape(-1))
    ch = jnp.arange(tm) // DN_CHUNK
    same = ch[:, None] == ch[None, :]
    idx = jnp.arange(tm)
    trip = (same & (idx[:, None] >= idx[None, :])).astype(BF16)
    tris = (same & (idx[:, None] <= idx[None, :])).astype(BF16)
    gmat = _group_matrix(hw, 1.0)
    in_specs = []
    args = []
    for col in (COL_DNQ, COL_DNK, COL_DNV):
        in_specs += [prev(col), cur(col), nxt(col)]
        args += [proj, proj, proj]
    in_specs += [pl.BlockSpec((None, tm, LANES), lambda bb, t: (bb, t, COL_AB // LANES)),
                 const((DN_CONV, 3 * hw)), const((1, LANES)), const((1, LANES)), const((hw, hw)),
                 const((tm, tm)), const((tm, tm))]
    args += [proj, conv_w.astype(F32), alog, dtb, gmat, trip, tris]
    head_spec = pl.BlockSpec((None, DN_HEADS, tm, DN_DK), lambda bb, t: (bb, 0, t, 0))
    head_shape = jax.ShapeDtypeStruct((b, DN_HEADS, l, DN_DK), F32)
    return pl.pallas_call(
        functools.partial(_dnprep_kernel, tm=tm),
        grid=(b, l // tm),
        in_specs=in_specs,
        out_specs=[head_spec, head_spec,
                   pl.BlockSpec((None, DN_HEADS, tm // DN_CHUNK, DN_DK, DN_CHUNK), lambda bb, t: (bb, 0, t, 0, 0)),
                   head_spec,
                   pl.BlockSpec((None, tm, LANES), lambda bb, t: (bb, t, 0)),
                   pl.BlockSpec((None, tm // DN_CHUNK, 4 * DN_HEADS, DN_CHUNK), lambda bb, t: (bb, t, 0, 0))],
        out_shape=[head_shape, head_shape,
                   jax.ShapeDtypeStruct((b, DN_HEADS, nc, DN_DK, DN_CHUNK), F32),
                   head_shape,
                   jax.ShapeDtypeStruct((b, l, LANES), F32),
                   jax.ShapeDtypeStruct((b, nc, 4 * DN_HEADS, DN_CHUNK), F32)],
        compiler_params=_cparams(("arbitrary", "arbitrary")),
        name="dn_prep",
    )(*args)


def _bmm(a, b):
    return jnp.einsum('nij,njk->nik', a.astype(BF16), b.astype(BF16), preferred_element_type=F32)


def _unit_tri_inverse(a, eye, blks):
    inner = jnp.where(blks[0], a, 0.0)
    x = eye - inner
    p = inner
    for _ in range(2):
        p = _bmm(p, p)
        x = x + _bmm(x, p)
    for level in range(1, len(blks) + 1):
        outer = jnp.where(blks[level], a, 0.0) if level < len(blks) else a
        x = x - _bmm(x, _bmm(outer - inner, x))
        inner = outer
    return x


def _dn_local(q, k, kt, v, gcol, grow, beta, g_last, incl, strict, eye, blks):
    decay = jnp.where(incl, jnp.exp(jnp.where(incl, gcol - grow, 0.0)), 0.0)
    kb = k * beta
    a = jnp.where(strict, _bmm(kb, kt) * decay, 0.0)
    x = _unit_tri_inverse(a, eye, blks)
    u = _bmm(x, v * beta)
    w = _bmm(x, kb * jnp.exp(gcol))
    qk = _bmm(q, kt) * decay
    return u, w, qk, q * jnp.exp(gcol), kt * jnp.exp(g_last - grow), jnp.exp(g_last)


def _dnscan_kernel(qf, kf, ktf, vf, gcf, gctf, qb, kb_, ktb, vb, gcb, gctb, s0_ref, of_ref, ob_ref, sout_ref, s_scr, *, cb):
    i = pl.program_id(1)
    last = pl.num_programs(1) - 1
    nh = DN_HEADS

    @pl.when(i == 0)
    def _():
        s_scr[...] = s0_ref[...]

    row = lax.broadcasted_iota(I32, (DN_CHUNK, DN_CHUNK), 0)
    col = lax.broadcasted_iota(I32, (DN_CHUNK, DN_CHUNK), 1)
    eye = (row == col).astype(F32)
    blks = tuple((row // n) == (col // n) for n in (8, 16, 32))
    dirs = ((0, qf, kf, ktf, vf, gcf, gctf, row >= col, row > col),
            (1, qb, kb_, ktb, vb, gcb, gctb, row <= col, row < col))
    local = []
    for d, q_ref, k_ref, kt_ref, v_ref, gc_ref, gct_ref, incl, strict in dirs:
        last_row = DN_CHUNK - 1 if d == 0 else 0
        qs, ks, kts, vs, gcols, grows, betas, glasts = [], [], [], [], [], [], [], []
        for c in range(cb):
            rs = slice(c * DN_CHUNK, (c + 1) * DN_CHUNK)
            gc = gc_ref[rs, :]
            gct = gct_ref[c]
            for h in range(nh):
                gl = d * nh + h
                bl = 2 * nh + d * nh + h
                qs.append(q_ref[h, rs, :])
                ks.append(k_ref[h, rs, :])
                kts.append(kt_ref[h, c])
                vs.append(v_ref[h, rs, :])
                gcols.append(gc[:, gl:gl + 1])
                grows.append(gct[gl:gl + 1, :])
                betas.append(gc[:, bl:bl + 1])
                glasts.append(gc[last_row:last_row + 1, gl:gl + 1])
        st = lambda xs: jnp.stack(xs, axis=0)
        local.append(_dn_local(st(qs), st(ks), st(kts), st(vs), st(gcols), st(grows), st(betas), st(glasts),
                               incl, strict, eye, blks))

    s_all = jnp.concatenate([s_scr[0], s_scr[1]], axis=0)
    for jj in range(cb):
        sel = (slice(jj * nh, (jj + 1) * nh), slice((cb - 1 - jj) * nh, (cb - jj) * nh))
        u, w, qk, qg, kdt, el = (jnp.concatenate([local[0][t][sel[0]], local[1][t][sel[1]]], axis=0) for t in range(6))
        v_new = u - _bmm(w, s_all)
        o = _bmm(qg, s_all) + _bmm(qk, v_new)
        s_all = s_all * el + _bmm(kdt, v_new)
        for d, o_ref in ((0, of_ref), (1, ob_ref)):
            c = jj if d == 0 else cb - 1 - jj
            for h in range(nh):
                o_ref[h, c * DN_CHUNK:(c + 1) * DN_CHUNK, :] = o[d * nh + h]
    s_scr[0] = s_all[:nh]
    s_scr[1] = s_all[nh:]

    @pl.when(i == last)
    def _():
        sout_ref[...] = s_scr[...]


def _dnscan(q, k, kt, v, gc, gct, s0):
    b, nh, l, dk = q.shape
    cb = min(DN_BLOCK_CHUNKS, l // DN_CHUNK)
    tb = cb * DN_CHUNK
    nblk = l // tb
    fwd = lambda i: i
    bwd = lambda i: nblk - 1 - i

    def specs(ix):
        head = pl.BlockSpec((None, nh, tb, dk), lambda bb, i: (bb, 0, ix(i), 0))
        return [head, head,
                pl.BlockSpec((None, nh, cb, dk, DN_CHUNK), lambda bb, i: (bb, 0, ix(i), 0, 0)),
                head,
                pl.BlockSpec((None, tb, LANES), lambda bb, i: (bb, ix(i), 0)),
                pl.BlockSpec((None, cb, 4 * DN_HEADS, DN_CHUNK), lambda bb, i: (bb, ix(i), 0, 0))]

    state_spec = pl.BlockSpec((None, 2, nh, dk, dk), lambda bb, i: (bb, 0, 0, 0, 0))
    o_shape = jax.ShapeDtypeStruct((b, nh, l, dk), F32)
    return pl.pallas_call(
        functools.partial(_dnscan_kernel, cb=cb),
        grid=(b, nblk),
        in_specs=specs(fwd) + specs(bwd) + [state_spec],
        out_specs=[pl.BlockSpec((None, nh, tb, dk), lambda bb, i: (bb, 0, fwd(i), 0)),
                   pl.BlockSpec((None, nh, tb, dk), lambda bb, i: (bb, 0, bwd(i), 0)),
                   state_spec],
        out_shape=[o_shape, o_shape, jax.ShapeDtypeStruct((b, 2, nh, dk, dk), F32)],
        scratch_shapes=[pltpu.VMEM((2, nh, dk, dk), F32)],
        compiler_params=_cparams(("arbitrary", "arbitrary")),
        name="dn_scan",
    )(q, k, kt, v, gc, gct, q, k, kt, v, gc, gct, s0)


def _mix0_out_kernel(x_ref, g1_ref, a_ref, of_ref, ob_ref, z_ref, nw_ref, w_ref, o_ref):
    o = of_ref[...] + ob_ref[...]
    on = o * lax.rsqrt(jnp.mean(o * o, axis=-1, keepdims=True) + EPS) * nw_ref[...]
    ot = jnp.concatenate([on[h] for h in range(DN_HEADS)], axis=1)
    z = z_ref[...]
    bmix = ot * (z * jax.nn.sigmoid(z))
    mix = jnp.concatenate([a_ref[...], bmix.astype(BF16)], axis=1)
    o_ref[...] = x_ref[...] + g1_ref[0] * _dot(mix, w_ref[...])


def _mix0_out(x, mod3, layer, row_fn, attn, o_f, o_b, proj, norm_w, w_out_bf16):
    b, l, d = x.shape
    tm = min(TOKEN_TILE, l)
    aw = Q_HEADS * HEAD_DIM
    zw = DN_HEADS * DN_DK
    head = pl.BlockSpec((None, DN_HEADS, tm, DN_DK), lambda bb, t: (bb, 0, t, 0))
    return pl.pallas_call(
        _mix0_out_kernel,
        grid=(b, l // tm),
        in_specs=[pl.BlockSpec((None, tm, d), lambda bb, t: (bb, t, 0)),
                  pl.BlockSpec((1, 1, d), _mod_spec(layer, row_fn, MOD_G1)),
                  pl.BlockSpec((None, tm, aw), lambda bb, t: (bb, t, 0)),
                  head, head,
                  pl.BlockSpec((None, tm, zw), lambda bb, t: (bb, t, COL_Z // zw)),
                  pl.BlockSpec((1, DN_DK), lambda bb, t: (0, 0)),
                  pl.BlockSpec((aw + zw, d), lambda bb, t: (0, 0))],
        out_specs=pl.BlockSpec((None, tm, d), lambda bb, t: (bb, t, 0)),
        out_shape=jax.ShapeDtypeStruct((b, l, d), F32),
        compiler_params=_cparams(("arbitrary", "arbitrary")),
        name="mix0_out",
    )(x, mod3, attn, o_f, o_b, proj, norm_w.astype(F32).reshape(1, DN_DK), w_out_bf16)


def _s5_kernel(x_ref, sh_ref, sc_ref, wb_ref, wc_ref, lam_ref, x0_ref, y_ref, xfin_ref, bre, bim, st, *, tm, pitch):
    d = pl.program_id(1)
    i = pl.program_id(2)
    nsub = SUBLANES
    nj = bre.shape[0]
    h = _modulated(x_ref, sh_ref, sc_ref).astype(BF16)
    half = nj * LANES
    for s in range(nsub):
        o = _dot(h[:, s * LANES:(s + 1) * LANES], wb_ref[s])
        for j in range(nj):
            bre[j, s * pitch:s * pitch + tm, :] = o[:, j * LANES:(j + 1) * LANES]
            bim[j, s * pitch:s * pitch + tm, :] = o[:, half + j * LANES:half + (j + 1) * LANES]

    @pl.when(i == 0)
    def _():
        st[...] = x0_ref[...]

    lam_re = lam_ref[0]
    lam_im = lam_ref[1]

    def body(k, carry):
        xr, xi = carry
        t = jnp.where(d == 0, k, tm - 1 - k)
        br = jnp.concatenate([bre[j, pl.ds(t, nsub, stride=pitch), :] for j in range(nj)], axis=1)
        bi = jnp.concatenate([bim[j, pl.ds(t, nsub, stride=pitch), :] for j in range(nj)], axis=1)
        nr = lam_re * xr - lam_im * xi + br
        ni = lam_re * xi + lam_im * xr + bi
        for j in range(nj):
            bre[j, pl.ds(t, nsub, stride=pitch), :] = nr[:, j * LANES:(j + 1) * LANES]
            bim[j, pl.ds(t, nsub, stride=pitch), :] = ni[:, j * LANES:(j + 1) * LANES]
        return nr, ni

    xr, xi = lax.fori_loop(0, tm, body, (st[0], st[1]), unroll=8)
    st[0] = xr
    st[1] = xi
    xfin_ref[0] = xr
    xfin_ref[1] = xi
    for s in range(nsub):
        parts = [bre[j, s * pitch:s * pitch + tm, :] for j in range(nj)]
        parts += [bim[j, s * pitch:s * pitch + tm, :] for j in range(nj)]
        xs = jnp.concatenate(parts, axis=1).astype(BF16)
        y_ref[:, s * LANES:(s + 1) * LANES] = _dot(xs, wc_ref[s])


def _s5_scan(x, mod3, layer, row_fn, wb, wc, lam, x0):
    b, l, d = x.shape
    tm = min(TOKEN_TILE, l)
    nt = l // tm
    pitch = tm + S5_PITCH_PAD
    sw = wb.shape[-1] // 2
    nj = sw // LANES
    tile = lambda dd, i: jnp.where(dd == 0, i, nt - 1 - i)
    mspec = lambda which: pl.BlockSpec((1, 1, d), lambda bb, dd, i: _mod_spec(layer, row_fn, which)(bb, i))
    return pl.pallas_call(
        functools.partial(_s5_kernel, tm=tm, pitch=pitch),
        grid=(b, 2, nt),
        in_specs=[pl.BlockSpec((None, tm, d), lambda bb, dd, i: (bb, tile(dd, i), 0)),
                  mspec(MOD_SH1), mspec(MOD_SC1),
                  pl.BlockSpec((None, SUBLANES, LANES, 2 * sw), lambda bb, dd, i: (dd, 0, 0, 0)),
                  pl.BlockSpec((None, SUBLANES, 2 * sw, LANES), lambda bb, dd, i: (dd, 0, 0, 0)),
                  pl.BlockSpec((None, 2, SUBLANES, sw), lambda bb, dd, i: (dd, 0, 0, 0)),
                  pl.BlockSpec((None, None, 2, SUBLANES, sw), lambda bb, dd, i: (bb, dd, 0, 0, 0))],
        out_specs=[pl.BlockSpec((None, None, tm, d), lambda bb, dd, i: (dd, bb, tile(dd, i), 0)),
                   pl.BlockSpec((None, None, 2, SUBLANES, sw), lambda bb, dd, i: (bb, dd, 0, 0, 0))],
        out_shape=[jax.ShapeDtypeStruct((2, b, l, d), F32),
                   jax.ShapeDtypeStruct((b, 2, 2, SUBLANES, sw), F32)],
        scratch_shapes=[pltpu.VMEM((nj, SUBLANES * pitch, LANES), F32),
                        pltpu.VMEM((nj, SUBLANES * pitch, LANES), F32),
                        pltpu.VMEM((2, SUBLANES, sw), F32)],
        compiler_params=_cparams(("arbitrary", "arbitrary", "arbitrary")),
        name="s5_scan",
    )(x, mod3, mod3, wb, wc, lam, x0)


def _s5_weights(lam_re, lam_im, log_step, b_re, b_im, c_re, c_im):
    lam = lax.complex(lam_re.astype(F32), lam_im.astype(F32))
    step = jnp.exp(log_step.astype(F32))[..., None]
    lam_bar = jnp.exp(lam * step)
    b_bar = ((lam_bar - 1.0) / lam)[..., None] * lax.complex(b_re.astype(F32), b_im.astype(F32))
    ng, p = lam.shape[1], lam.shape[2]
    gl = ng // SUBLANES
    eye = jnp.eye(gl, dtype=F32)

    def wb_of(part):
        t = part.reshape(2, SUBLANES, gl, p, S5_GROUP)
        w = jnp.einsum('dsgpi,gh->dsgihp', t, eye)
        return w.reshape(2, SUBLANES, gl * S5_GROUP, gl * p)

    def wc_of(part):
        t = part.reshape(2, SUBLANES, gl, S5_GROUP, p)
        w = jnp.einsum('dsgip,gh->dshpgi', t, eye)
        return w.reshape(2, SUBLANES, gl * p, gl * S5_GROUP)

    wb = jnp.concatenate([wb_of(jnp.real(b_bar)), wb_of(jnp.imag(b_bar))], axis=-1).astype(BF16)
    wc = jnp.concatenate([wc_of(c_re.astype(F32)), -wc_of(c_im.astype(F32))], axis=-2).astype(BF16)
    lam_t = jnp.stack([jnp.real(lam_bar).reshape(2, SUBLANES, gl * p),
                       jnp.imag(lam_bar).reshape(2, SUBLANES, gl * p)], axis=1)
    return wb, wc, lam_t


def _s5_glu_kernel(x_ref, sh_ref, sc_ref, g1_ref, yf_ref, yb_ref, dsk_ref, w1_ref, b1_ref, w2_ref, b2_ref, o_ref):
    x = x_ref[...]
    h = _rms(x) * (1.0 + sc_ref[0]) + sh_ref[0]
    y = dsk_ref[...] * h + yf_ref[...] + yb_ref[...]
    z = jax.nn.gelu(y).astype(BF16)
    o = (_dot(z, w1_ref[...]) + b1_ref[...]) * jax.nn.sigmoid(_dot(z, w2_ref[...]) + b2_ref[...])
    o_ref[...] = x + g1_ref[0] * o


def _s5_glu(x, mod3, layer, row_fn, y2, d_skip, w1, b1, w2, b2):
    b, l, d = x.shape
    tm = min(TOKEN_TILE, l)
    mspec = lambda which: pl.BlockSpec((1, 1, d), _mod_spec(layer, row_fn, which))
    const = lambda shape: pl.BlockSpec(shape, lambda bb, t: (0,) * len(shape))
    return pl.pallas_call(
        _s5_glu_kernel,
        grid=(b, l // tm),
        in_specs=[pl.BlockSpec((None, tm, d), lambda bb, t: (bb, t, 0)),
                  mspec(MOD_SH1), mspec(MOD_SC1), mspec(MOD_G1),
                  pl.BlockSpec((None, None, tm, d), lambda bb, t: (0, bb, t, 0)),
                  pl.BlockSpec((None, None, tm, d), lambda bb, t: (1, bb, t, 0)),
                  const((1, d)), const((d, d)), const((1, d)), const((d, d)), const((1, d))],
        out_specs=pl.BlockSpec((None, tm, d), lambda bb, t: (bb, t, 0)),
        out_shape=jax.ShapeDtypeStruct((b, l, d), F32),
        compiler_params=_cparams(("arbitrary", "arbitrary")),
        name="s5_glu",
    )(x, mod3, mod3, mod3, y2, y2, d_skip.astype(F32).reshape(1, d), w1.astype(BF16), b1.astype(F32).reshape(1, d),
      w2.astype(BF16), b2.astype(F32).reshape(1, d))


def _route_kernel(x_ref, sh_ref, sc_ref, rw_ref, rb_ref, cin_ref, tri_ref, hp_ref, meta_ref, cnt_ref, carry):
    first = (pl.program_id(0) == 0) & (pl.program_id(1) == 0)

    @pl.when(first)
    def _():
        carry[...] = cin_ref[...]

    h = _modulated(x_ref, sh_ref, sc_ref)
    half = h.shape[1] // 2
    lo = pltpu.bitcast(h[:, :half].astype(BF16).astype(F32), U32)
    hi = pltpu.bitcast(h[:, half:].astype(BF16).astype(F32), U32)
    packed = (lo >> 16) | (hi & jnp.uint32(0xFFFF0000))
    pieces = half // LANES
    for j in range(pieces):
        hp_ref[pl.ds(j, h.shape[0], stride=pieces), :] = packed[:, j * LANES:(j + 1) * LANES]

    logits = _dot_hi(h, rw_ref[...]) + rb_ref[...]
    lane = lax.broadcasted_iota(I32, logits.shape, 1)
    cur = jnp.where(lane < N_EXPERTS, logits, -jnp.inf)
    vals, idxs = [], []
    hot = jnp.zeros(logits.shape, F32)
    for _ in range(TOP_K):
        m = jnp.max(cur, axis=-1, keepdims=True)
        idx = jnp.min(jnp.where(cur == m, lane, LANES), axis=-1, keepdims=True)
        sel = lane == idx
        hot = hot + sel.astype(F32)
        cur = jnp.where(sel, -jnp.inf, cur)
        vals.append(m)
        idxs.append(idx)
    es = [jnp.exp(v - vals[0]) for v in vals]
    tot = es[0] + es[1] + es[2] + es[3]
    before = _dot(tri_ref[...], hot.astype(BF16)) + carry[...]
    meta = jnp.zeros(logits.shape, F32)
    for k in range(TOP_K):
        rank = jnp.sum(jnp.where(lane == idxs[k], before, 0.0), axis=-1, keepdims=True)
        meta = jnp.where(lane == k, idxs[k].astype(F32), meta)
        meta = jnp.where(lane == TOP_K + k, es[k] / tot, meta)
        meta = jnp.where(lane == 2 * TOP_K + k, rank, meta)
    meta_ref[...] = meta
    carry[...] = carry[...] + jnp.sum(hot, axis=0, keepdims=True)
    cnt_ref[...] = carry[...]


def _route(x, mod3, layer, row_fn, rw_pad, rb_pad, counts_in):
    b, l, d = x.shape
    tm = min(TOKEN_TILE, l)
    idx = jnp.arange(tm)
    tri = (idx[:, None] > idx[None, :]).astype(BF16)
    mspec = lambda which: pl.BlockSpec((1, 1, d), _mod_spec(layer, row_fn, which))
    const = lambda shape: pl.BlockSpec(shape, lambda bb, t: (0,) * len(shape))
    return pl.pallas_call(
        _route_kernel,
        grid=(b, l // tm),
        in_specs=[pl.BlockSpec((None, tm, d), lambda bb, t: (bb, t, 0)),
                  mspec(MOD_SH2), mspec(MOD_SC2),
                  const((d, LANES)), const((1, LANES)), const((1, LANES)), const((tm, tm))],
        out_specs=[pl.BlockSpec((None, tm * (d // 2 // LANES), LANES), lambda bb, t: (bb, t, 0)),
                   pl.BlockSpec((None, tm, LANES), lambda bb, t: (bb, t, 0)),
                   const((1, LANES))],
        out_shape=[jax.ShapeDtypeStruct((b, l * (d // 2 // LANES), LANES), U32),
                   jax.ShapeDtypeStruct((b, l, LANES), F32),
                   jax.ShapeDtypeStruct((1, LANES), F32)],
        scratch_shapes=[pltpu.VMEM((1, LANES), F32)],
        compiler_params=_cparams(("arbitrary", "arbitrary")),
        name="moe_route",
    )(x, mod3, mod3, rw_pad, rb_pad, counts_in, tri)


def _dest_kernel(meta_ref, ps_ref, o_ref):
    meta = meta_ref[...]
    lane = lax.broadcasted_iota(I32, meta.shape, 1)
    ps = ps_ref[...]
    dst = jnp.zeros(meta.shape, F32)
    for k in range(TOP_K):
        idx = meta[:, k:k + 1].astype(I32)
        base = jnp.sum(jnp.where(lane == idx, ps, 0.0), axis=-1, keepdims=True)
        dst = jnp.where(lane == k, base + meta[:, 2 * TOP_K + k:2 * TOP_K + k + 1], dst)
    dt = dst.T
    o_ref[0] = jnp.concatenate([dt[k:k + 1] for k in range(TOP_K)], axis=1).astype(I32)


def _dest(meta, pad_start_row):
    b, l, _ = meta.shape
    tt = min(TOKEN_TILE, l)
    nt = l // tt
    return pl.pallas_call(
        _dest_kernel,
        grid=(b, nt),
        in_specs=[pl.BlockSpec((None, tt, LANES), lambda bb, t: (bb, t, 0)),
                  pl.BlockSpec((1, LANES), lambda bb, t: (0, 0))],
        out_specs=pl.BlockSpec((1, 1, TOP_K * tt), lambda bb, t: (bb * nt + t, 0, 0)),
        out_shape=jax.ShapeDtypeStruct((b * nt, 1, TOP_K * tt), I32),
        compiler_params=_cparams(("arbitrary", "arbitrary")),
        name="moe_dest",
    )(meta, pad_start_row)


def _dispatch_kernel(dest_hbm, hp_ref, xs_in, xs_out, dest_smem, sem_i, sem, *, tt, pieces):
    del xs_in
    i = pl.program_id(0)
    cp = pltpu.make_async_copy(dest_hbm.at[i, 0], dest_smem, sem_i)
    cp.start()
    cp.wait()

    def row_copy(r, k):
        src = pl.multiple_of(r * pieces, pieces)
        dst = pl.multiple_of(dest_smem[k * tt + r] * pieces, pieces)
        return pltpu.make_async_copy(hp_ref.at[pl.ds(src, pieces)], xs_out.at[pl.ds(dst, pieces)], sem)

    def issue(r, c):
        for k in range(TOP_K):
            row_copy(r, k).start(priority=k % 2)
        return c

    lax.fori_loop(0, tt, issue, 0, unroll=4)

    def drain(r, c):
        for k in range(TOP_K):
            row_copy(r, k).wait()
        return c

    lax.fori_loop(0, tt, drain, 0, unroll=4)


def _dispatch(dest, hp, xs, pieces):
    n = hp.shape[0] // pieces
    tt = dest.shape[-1] // TOP_K
    return pl.pallas_call(
        functools.partial(_dispatch_kernel, tt=tt, pieces=pieces),
        grid=(n // tt,),
        in_specs=[pl.BlockSpec(memory_space=pl.ANY),
                  pl.BlockSpec((tt * pieces, LANES), lambda i: (i, 0)),
                  pl.BlockSpec(memory_space=pl.ANY)],
        out_specs=pl.BlockSpec(memory_space=pl.ANY),
        out_shape=jax.ShapeDtypeStruct(xs.shape, xs.dtype),
        scratch_shapes=[pltpu.SMEM((tt * TOP_K,), I32), pltpu.SemaphoreType.DMA, pltpu.SemaphoreType.DMA],
        input_output_aliases={2: 0},
        compiler_params=_cparams(("arbitrary",)),
        name="moe_dispatch",
    )(dest, hp, xs)


def _expert_kernel(be_ref, nact_ref, xs_ref, wgu_ref, bgu_ref, wd_ref, bd_ref, y_ref, wgu_bf, wd_bf):
    i = pl.program_id(0)
    e = be_ref[i]
    prev = be_ref[jnp.maximum(i - 1, 0)]

    @pl.when((i == 0) | (e != prev))
    def _():
        wgu_bf[...] = wgu_ref[0].astype(BF16)
        wd_bf[...] = wd_ref[0].astype(BF16)

    rows = MOE_ROWS
    in_pieces = xs_ref.shape[0] // rows
    out_pieces = y_ref.shape[0] // rows

    @pl.when(i < nact_ref[0])
    def _():
        ws = [xs_ref[pl.ds(j, rows, stride=in_pieces), :] for j in range(in_pieces)]
        lo = [pltpu.bitcast(w << 16, F32) for w in ws]
        hi = [pltpu.bitcast(w & jnp.uint32(0xFFFF0000), F32) for w in ws]
        x = jnp.concatenate(lo + hi, axis=1).astype(BF16)
        gu = _dot(x, wgu_bf[...]) + bgu_ref[0]
        de = gu.shape[1] // 2
        gate = jnp.minimum(gu[:, :de], SWIGLU_LIMIT)
        up = jnp.clip(gu[:, de:], -SWIGLU_LIMIT, SWIGLU_LIMIT)
        act = (up + 1.0) * gate * jax.nn.sigmoid(SWIGLU_ALPHA * gate)
        y = _dot(act.astype(BF16), wd_bf[...]) + bd_ref[0]
        for j in range(out_pieces):
            y_ref[pl.ds(j, rows, stride=out_pieces), :] = y[:, j * LANES:(j + 1) * LANES]

    @pl.when(i >= nact_ref[0])
    def _():
        y_ref[...] = jnp.zeros(y_ref.shape, y_ref.dtype)


def _experts(block_e, n_active, xs, w_gu, b_gu, w_d, b_d, layer):
    _, ne, d, de2 = w_gu.shape
    in_pieces = d // 2 // LANES
    out_pieces = d // LANES
    n_rows = xs.shape[0] // in_pieces
    nblk = n_rows // MOE_ROWS
    grid_spec = pltpu.PrefetchScalarGridSpec(
        num_scalar_prefetch=2,
        grid=(nblk,),
        in_specs=[pl.BlockSpec((MOE_ROWS * in_pieces, LANES), lambda i, be, na: (i, 0)),
                  pl.BlockSpec((None, 1, d, de2), lambda i, be, na: (layer, be[i], 0, 0)),
                  pl.BlockSpec((None, 1, 1, de2), lambda i, be, na: (layer, be[i], 0, 0)),
                  pl.BlockSpec((None, 1, de2 // 2, d), lambda i, be, na: (layer, be[i], 0, 0)),
                  pl.BlockSpec((None, 1, 1, d), lambda i, be, na: (layer, be[i], 0, 0))],
        out_specs=pl.BlockSpec((MOE_ROWS * out_pieces, LANES), lambda i, be, na: (i, 0)),
        scratch_shapes=[pltpu.VMEM((d, de2), BF16), pltpu.VMEM((de2 // 2, d), BF16)],
    )
    depth = w_gu.shape[0]
    return pl.pallas_call(
        _expert_kernel,
        grid_spec=grid_spec,
        out_shape=jax.ShapeDtypeStruct((n_rows * out_pieces, LANES), F32),
        compiler_params=_cparams(("arbitrary",)),
        name="moe_experts",
    )(block_e, n_active, xs, w_gu, b_gu.reshape(depth, ne, 1, de2), w_d, b_d.reshape(depth, ne, 1, d))


def _combine_kernel(*refs, tt, final):
    if final:
        dest_hbm, yb_hbm, x_ref, meta_ref, g2_ref, fw_ref, o_ref, dest_smem, buf, sem_i, sem = refs
    else:
        dest_hbm, yb_hbm, x_ref, meta_ref, g2_ref, o_ref, dest_smem, buf, sem_i, sem = refs
    i = pl.program_id(0) * pl.num_programs(1) + pl.program_id(1)
    cp = pltpu.make_async_copy(dest_hbm.at[i, 0], dest_smem, sem_i)
    cp.start()
    cp.wait()

    pieces = buf.shape[1] // tt

    def row_copy(r, k):
        src = pl.multiple_of(dest_smem[k * tt + r] * pieces, pieces)
        dst = pl.multiple_of(r * pieces, pieces)
        return pltpu.make_async_copy(yb_hbm.at[pl.ds(src, pieces)], buf.at[k, pl.ds(dst, pieces)], sem)

    def issue(r, c):
        for k in range(TOP_K):
            row_copy(r, k).start(priority=k % 2)
        return c

    lax.fori_loop(0, tt, issue, 0, unroll=4)

    def drain(r, c):
        for k in range(TOP_K):
            row_copy(r, k).wait()
        return c

    lax.fori_loop(0, tt, drain, 0, unroll=4)
    meta = meta_ref[...]
    g2 = g2_ref[0]
    cols = []
    for j in range(pieces):
        y = meta[:, TOP_K:TOP_K + 1] * buf[0, pl.ds(j, tt, stride=pieces), :]
        for k in range(1, TOP_K):
            y = y + meta[:, TOP_K + k:TOP_K + k + 1] * buf[k, pl.ds(j, tt, stride=pieces), :]
        sl = slice(j * LANES, (j + 1) * LANES)
        cols.append(x_ref[:, sl] + g2[:, sl] * y)
    xn = jnp.concatenate(cols, axis=1)
    if final:
        xn = _rms(xn) * fw_ref[...]
    o_ref[...] = xn


def _combine(dest, yb, x, meta, mod3, layer, row_fn, final_w=None):
    b, l, d = x.shape
    tt = min(TOKEN_TILE, l)
    nt = l // tt
    final = final_w is not None
    in_specs = [pl.BlockSpec(memory_space=pl.ANY), pl.BlockSpec(memory_space=pl.ANY),
                pl.BlockSpec((None, tt, d), lambda bb, t: (bb, t, 0)),
                pl.BlockSpec((None, tt, LANES), lambda bb, t: (bb, t, 0)),
                pl.BlockSpec((1, 1, d), _mod_spec(layer, row_fn, MOD_G2))]
    args = [dest, yb, x, meta, mod3]
    if final:
        in_specs.append(pl.BlockSpec((1, d), lambda bb, t: (0, 0)))
        args.append(final_w.astype(F32).reshape(1, d))
    return pl.pallas_call(
        functools.partial(_combine_kernel, tt=tt, final=final),
        grid=(b, nt),
        in_specs=in_specs,
        out_specs=pl.BlockSpec((None, tt, d), lambda bb, t: (bb, t, 0)),
        out_shape=jax.ShapeDtypeStruct((b, l, d), F32),
        scratch_shapes=[pltpu.SMEM((tt * TOP_K,), I32), pltpu.VMEM((TOP_K, tt * (d // LANES), LANES), F32),
                        pltpu.SemaphoreType.DMA, pltpu.SemaphoreType.DMA],
        compiler_params=_cparams(("arbitrary", "arbitrary")),
        name="moe_combine",
    )(*args)


def _moe(parts, mod3, layer, router_w, router_b, w_gu, b_gu, w_d, b_d, final_w=None):
    d = router_w.shape[0]
    rw_pad = jnp.zeros((d, LANES), F32).at[:, :N_EXPERTS].set(router_w.astype(F32))
    rb_pad = jnp.zeros((1, LANES), F32).at[0, :N_EXPERTS].set(router_b.astype(F32))
    counts = jnp.zeros((1, LANES), F32)
    routed = []
    for x, row_fn in parts:
        hp, meta, counts = _route(x, mod3, layer, row_fn, rw_pad, rb_pad, counts)
        routed.append((hp, meta))
    n_assign = sum(x.shape[0] * x.shape[1] for x, _ in parts) * TOP_K
    nblk = -(-(n_assign + N_EXPERTS * (MOE_ROWS - 1)) // MOE_ROWS)
    cnt = counts[0, :N_EXPERTS].astype(I32)
    padded = (cnt + MOE_ROWS - 1) // MOE_ROWS * MOE_ROWS
    pad_end = jnp.cumsum(padded)
    pad_start = pad_end - padded
    blk_row = jnp.arange(nblk, dtype=I32)[:, None] * MOE_ROWS
    block_e = jnp.minimum(jnp.sum((pad_end[None, :] <= blk_row).astype(I32), axis=1), N_EXPERTS - 1).astype(I32)
    n_active = (pad_end[-1] // MOE_ROWS).astype(I32).reshape(1)
    ps_row = jnp.zeros((1, LANES), F32).at[0, :N_EXPERTS].set(pad_start.astype(F32))
    pieces = d // 2 // LANES
    xs = jnp.zeros((nblk * MOE_ROWS * pieces, LANES), U32)
    dests = []
    for (x, _), (hp, meta) in zip(parts, routed):
        dest = _dest(meta, ps_row)
        dests.append(dest)
        xs = _dispatch(dest, hp.reshape(-1, LANES), xs, pieces)
    yb = _experts(block_e, n_active, xs, w_gu, b_gu, w_d, b_d, layer)
    outs = []
    for idx, ((x, row_fn), (hp, meta)) in enumerate(zip(parts, routed)):
        fw = final_w if idx == 0 else None
        outs.append(_combine(dests[idx], yb, x, meta, mod3, layer, row_fn, fw))
    return outs


def kernel(x, c, ctx, c_ctx, ada_w, ada_b, ab_w_in, ab_q_gain, ab_k_gain, dn_conv_w, dn_a_log, dn_dt_bias, dn_norm_w, ab_w_out, s5_lambda_re, s5_lambda_im, s5_log_step, s5_b_re, s5_b_im, s5_c_re, s5_c_im, s5_d, s5_glu_w1, s5_glu_b1, s5_glu_w2, s5_glu_b2, moe_router_w, moe_router_b, moe_w_gate_up, moe_b_gate_up, moe_w_down, moe_b_down, final_norm_w):
    b, l, d = x.shape
    lc = ctx.shape[1]
    depth = ada_w.shape[0]
    assert depth == 2 and b < MOD_ROWS
    lat_row = lambda bb: bb
    ctx_row = lambda bb: b

    rows = jnp.zeros((MOD_ROWS, d), F32).at[:b].set(c.astype(F32)).at[b].set(c_ctx.astype(F32))
    mod3 = _ada(rows, ada_w, ada_b).reshape(depth * MOD_ROWS * 6, 1, d)

    w_in = ab_w_in[0]
    aq, akv, bqk = Q_HEADS * HEAD_DIM, KV_HEADS * HEAD_DIM, DN_HEADS * DN_DK
    o_k, o_v, o_dn, o_ab, o_z = aq, aq + akv, aq + 2 * akv, aq + 2 * akv + 3 * bqk, aq + 2 * akv + 3 * bqk + 4 * DN_HEADS
    w_perm = jnp.concatenate([w_in[:, :aq], w_in[:, o_dn:o_ab], w_in[:, o_z:], w_in[:, o_k:o_v], w_in[:, o_v:o_dn],
                              w_in[:, o_ab:o_z], jnp.zeros((d, IN_COLS_PAD - w_in.shape[1]), w_in.dtype)], axis=1).astype(BF16)
    proj_l = _inproj(x, mod3, 0, lat_row, w_perm)
    proj_c = _inproj(ctx, mod3, 0, ctx_row, w_perm)

    cos, sin = _rope_tables(l)
    ones = jnp.ones((lc, KV_HEADS * HEAD_DIM), F32)
    q_l, kt_l, v_l = _qkprep(proj_l, ab_q_gain[0], ab_k_gain[0], cos, sin, min(ATTN_K_TILE, l))
    q_c, kt_c, v_c = _qkprep(proj_c, ab_q_gain[0], ab_k_gain[0], ones, jnp.zeros_like(ones), lc)
    a_l = _attention(q_l, kt_c, v_c, kt_l, v_l)
    a_c = _attention(q_c, kt_c, v_c)

    dq_c, dk_c, dkt_c, dv_c, gc_c, gct_c = _dnprep(proj_c, dn_conv_w[0], dn_a_log[0], dn_dt_bias[0])
    dq_l, dk_l, dkt_l, dv_l, gc_l, gct_l = _dnprep(proj_l, dn_conv_w[0], dn_a_log[0], dn_dt_bias[0])
    s_zero = jnp.zeros((b, 2, DN_HEADS, DN_DK, DN_DK), F32)
    of_c, ob_c, s_ctx = _dnscan(dq_c, dk_c, dkt_c, dv_c, gc_c, gct_c, s_zero)
    of_l, ob_l, _ = _dnscan(dq_l, dk_l, dkt_l, dv_l, gc_l, gct_l, s_ctx)

    w_out = ab_w_out[0].astype(BF16)
    x = _mix0_out(x, mod3, 0, lat_row, a_l, of_l, ob_l, proj_l, dn_norm_w[0], w_out)
    ctx = _mix0_out(ctx, mod3, 0, ctx_row, a_c, of_c, ob_c, proj_c, dn_norm_w[0], w_out)

    x, ctx = _moe([(x, lat_row), (ctx, ctx_row)], mod3, 0, moe_router_w[0], moe_router_b[0],
                  moe_w_gate_up, moe_b_gate_up, moe_w_down, moe_b_down)

    wb, wc, lam_t = _s5_weights(s5_lambda_re[0], s5_lambda_im[0], s5_log_step[0], s5_b_re[0], s5_b_im[0],
                                s5_c_re[0], s5_c_im[0])
    sw = lam_t.shape[-1]
    x0 = jnp.zeros((b, 2, 2, SUBLANES, sw), F32)
    _, x_ctx = _s5_scan(ctx, mod3, 1, ctx_row, wb, wc, lam_t, x0)
    y2, _ = _s5_scan(x, mod3, 1, lat_row, wb, wc, lam_t, x_ctx)
    x = _s5_glu(x, mod3, 1, lat_row, y2, s5_d[0], s5_glu_w1[0], s5_glu_b1[0], s5_glu_w2[0], s5_glu_b2[0])

    (x,) = _moe([(x, lat_row)], mod3, 1, moe_router_w[1], moe_router_b[1], moe_w_gate_up, moe_b_gate_up,
                moe_w_down, moe_b_down, final_w=final_norm_w)
    return x
```

```python
import functools
import math

import jax
import jax.numpy as jnp
from jax import lax
from jax.experimental import pallas as pl
from jax.experimental.pallas import tpu as pltpu

F32 = jnp.float32
BF16 = jnp.bfloat16
U32 = jnp.uint32
I32 = jnp.int32

EPS = 1e-6
GRID_W = 64
HEAD_DIM = 64
Q_HEADS = 8
KV_HEADS = 2
GROUP = Q_HEADS // KV_HEADS
ROPE_THETA = 10000.0
DN_HEADS = 8
DN_DK = 64
DN_CONV = 5
DN_CHUNK = 64
S5_GROUP = 16
S5_STATE = 64
N_EXPERTS = 32
TOP_K = 4
SWIGLU_LIMIT = 7.0
SWIGLU_ALPHA = 1.702

LANES = 128
SUBLANES = 8
TOKEN_TILE = 512
MOE_TOKEN_TILE = 1024
ATTN_Q_TILE = 256
ATTN_K_TILE = 2048
DN_BLOCK_CHUNKS = 4
MOE_ROWS = 512
S5_PITCH_PAD = 4
VMEM_LIMIT = 56 * 1024 * 1024

MOD_SH1, MOD_SC1, MOD_G1, MOD_SH2, MOD_SC2, MOD_G2 = range(6)
MOD_ROWS = 8

COL_Q = 0
COL_DNQ = 512
COL_DNK = 1024
COL_DNV = 1536
COL_Z = 2048
COL_K = 2560
COL_V = 2688
COL_AB = 2816
IN_COLS_PAD = 2944


def _cparams(sem):
    return pltpu.CompilerParams(dimension_semantics=sem, vmem_limit_bytes=VMEM_LIMIT)


def _split3(a):
    a1 = a.astype(BF16)
    r = a - a1.astype(F32)
    a2 = r.astype(BF16)
    a3 = (r - a2.astype(F32)).astype(BF16)
    return a1, a2, a3


def _dot(a, b):
    return jnp.dot(a, b, preferred_element_type=F32)


def _dot_exact_lhs(m_bf16, x):
    x1, x2, x3 = _split3(x)
    return _dot(m_bf16, x1) + _dot(m_bf16, x2) + _dot(m_bf16, x3)


def _dot_exact_rhs(x, m_bf16):
    x1, x2, x3 = _split3(x)
    return _dot(x1, m_bf16) + _dot(x2, m_bf16) + _dot(x3, m_bf16)


def _dot_hi(a, b):
    a1, a2, _ = _split3(a)
    b1, b2, _ = _split3(b)
    return _dot(a1, b1) + _dot(a1, b2) + _dot(a2, b1)


def _rms(x):
    return x * lax.rsqrt(jnp.mean(x * x, axis=-1, keepdims=True) + EPS)


def _modulated(x_ref, sh_ref, sc_ref):
    return _rms(x_ref[...]) * (1.0 + sc_ref[0]) + sh_ref[0]


def _mod_spec(layer, row_fn, which):
    d = None

    def imap(b, t):
        return ((layer * MOD_ROWS + row_fn(b)) * 6 + which, 0, 0)

    return imap


def _ada_kernel(a_ref, w_ref, b_ref, o_ref):
    a = a_ref[...]
    a = a * jax.nn.sigmoid(a)
    o_ref[0] = _dot_hi(a, w_ref[0]) + b_ref[0]


def _ada(rows, ada_w, ada_b):
    depth, d, n = ada_w.shape
    tn = 1536
    return pl.pallas_call(
        _ada_kernel,
        grid=(depth, n // tn),
        in_specs=[pl.BlockSpec((MOD_ROWS, d), lambda l, j: (0, 0)),
                  pl.BlockSpec((1, d, tn), lambda l, j: (l, 0, j)),
                  pl.BlockSpec((1, 1, tn), lambda l, j: (l, 0, j))],
        out_specs=pl.BlockSpec((1, MOD_ROWS, tn), lambda l, j: (l, 0, j)),
        out_shape=jax.ShapeDtypeStruct((depth, MOD_ROWS, n), F32),
        compiler_params=_cparams(("arbitrary", "arbitrary")),
        name="ada_mod",
    )(rows, ada_w, ada_b.reshape(depth, 1, n))


def _inproj_kernel(x_ref, sh_ref, sc_ref, w_ref, o_ref):
    h = _modulated(x_ref, sh_ref, sc_ref)
    o_ref[...] = _dot(h.astype(BF16), w_ref[...])


def _inproj(x, mod3, layer, row_fn, w_bf16):
    b, l, d = x.shape
    tm = min(TOKEN_TILE, l)
    n = w_bf16.shape[1]
    mspec = lambda which: pl.BlockSpec((1, 1, d), _mod_spec(layer, row_fn, which))
    return pl.pallas_call(
        _inproj_kernel,
        grid=(b, l // tm),
        in_specs=[pl.BlockSpec((None, tm, d), lambda bb, t: (bb, t, 0)),
                  mspec(MOD_SH1), mspec(MOD_SC1),
                  pl.BlockSpec((d, n), lambda bb, t: (0, 0))],
        out_specs=pl.BlockSpec((None, tm, n), lambda bb, t: (bb, t, 0)),
        out_shape=jax.ShapeDtypeStruct((b, l, n), F32),
        compiler_params=_cparams(("arbitrary", "arbitrary")),
        name="in_proj",
    )(x, mod3, mod3, w_bf16)


def _head_rot(x, width):
    lane = lax.broadcasted_iota(I32, x.shape, 1)
    first = (lane % HEAD_DIM) < (HEAD_DIM // 2)
    return jnp.where(first, pltpu.roll(x, width - HEAD_DIM // 2, 1), pltpu.roll(x, HEAD_DIM // 2, 1))


def _qkprep_kernel(q_ref, k_ref, v_ref, qg_ref, kg_ref, cos_ref, sin_ref, gq_ref, gk_ref, qo_ref, kt_ref, vo_ref):
    cos = cos_ref[...]
    sin = sin_ref[...]
    q = q_ref[...]
    qn = q * lax.rsqrt(_dot_exact_rhs(q * q, gq_ref[...]) + EPS) * qg_ref[...]
    cos4 = jnp.concatenate([cos] * 4, axis=1)
    sin4 = jnp.concatenate([sin] * 4, axis=1)
    qr = qn * cos4 + _head_rot(qn, Q_HEADS * HEAD_DIM) * sin4
    qo_ref[...] = (qr * (HEAD_DIM ** -0.5 * math.log2(math.e))).astype(BF16)
    k = k_ref[...]
    kn = k * lax.rsqrt(_dot_exact_rhs(k * k, gk_ref[...]) + EPS) * kg_ref[...]
    kr = kn * cos + _head_rot(kn, KV_HEADS * HEAD_DIM) * sin
    kt = kr.T
    kt_ref[0] = kt[:HEAD_DIM].astype(BF16)
    kt_ref[1] = kt[HEAD_DIM:].astype(BF16)
    v = v_ref[...]
    ones = jnp.ones((v.shape[0], HEAD_DIM), F32)
    vo_ref[0] = jnp.concatenate([v[:, :HEAD_DIM], ones], axis=1).astype(BF16)
    vo_ref[1] = jnp.concatenate([v[:, HEAD_DIM:], ones], axis=1).astype(BF16)


def _group_matrix(width, scale):
    g = jnp.arange(width) // HEAD_DIM
    return ((g[:, None] == g[None, :]).astype(F32) * scale).astype(BF16)


def _qkprep(proj, q_gain, k_gain, cos, sin, tk):
    b, l, _ = proj.shape
    qw = Q_HEADS * HEAD_DIM
    kw = KV_HEADS * HEAD_DIM
    qg = jnp.tile(q_gain.astype(F32), Q_HEADS).reshape(1, qw)
    kg = jnp.tile(k_gain.astype(F32), KV_HEADS).reshape(1, kw)
    const = lambda shape: pl.BlockSpec(shape, lambda bb, t: (0,) * len(shape))
    return pl.pallas_call(
        _qkprep_kernel,
        grid=(b, l // tk),
        in_specs=[pl.BlockSpec((None, tk, qw), lambda bb, t: (bb, t, COL_Q // qw)),
                  pl.BlockSpec((None, tk, kw), lambda bb, t: (bb, t, COL_K // kw)),
                  pl.BlockSpec((None, tk, kw), lambda bb, t: (bb, t, COL_V // kw)),
                  const((1, qw)), const((1, kw)),
                  pl.BlockSpec((tk, kw), lambda bb, t: (t, 0)),
                  pl.BlockSpec((tk, kw), lambda bb, t: (t, 0)),
                  const((qw, qw)), const((kw, kw))],
        out_specs=[pl.BlockSpec((None, tk, qw), lambda bb, t: (bb, t, 0)),
                   pl.BlockSpec((None, KV_HEADS, None, HEAD_DIM, tk), lambda bb, t: (bb, 0, t, 0, 0)),
                   pl.BlockSpec((None, KV_HEADS, tk, 2 * HEAD_DIM), lambda bb, t: (bb, 0, t, 0))],
        out_shape=[jax.ShapeDtypeStruct((b, l, qw), BF16),
                   jax.ShapeDtypeStruct((b, KV_HEADS, l // tk, HEAD_DIM, tk), BF16),
                   jax.ShapeDtypeStruct((b, KV_HEADS, l, 2 * HEAD_DIM), BF16)],
        compiler_params=_cparams(("arbitrary", "arbitrary")),
        name="qk_prep",
    )(proj, proj, proj, qg, kg, cos, sin, _group_matrix(qw, 1.0 / HEAD_DIM), _group_matrix(kw, 1.0 / HEAD_DIM))


def _rope_tables(length):
    rows = length // GRID_W
    row = jnp.broadcast_to(jnp.arange(rows, dtype=I32)[:, None], (rows, GRID_W)).reshape(-1).astype(F32)
    col = jnp.broadcast_to(jnp.arange(GRID_W, dtype=I32)[None, :], (rows, GRID_W)).reshape(-1).astype(F32)
    n_axis = HEAD_DIM // 4
    inv_freq = ROPE_THETA ** (-jnp.arange(n_axis, dtype=F32) / n_axis)
    ang = jnp.concatenate([row[:, None] * inv_freq, col[:, None] * inv_freq], axis=-1)
    c, s = jnp.cos(ang), jnp.sin(ang)
    cos_h = jnp.concatenate([c, c], axis=-1)
    sin_h = jnp.concatenate([-s, s], axis=-1)
    return jnp.tile(cos_h, (1, KV_HEADS)), jnp.tile(sin_h, (1, KV_HEADS))


def _attn_kernel(*refs, tq, n_lat_chunks, has_lat):
    if has_lat:
        q_ref, ktc_ref, vc_ref, ktl_ref, vl_ref, o_ref = refs
    else:
        q_ref, ktc_ref, vc_ref, o_ref = refs
    q = q_ref[...]
    qs = jnp.concatenate([q[:, g * HEAD_DIM:(g + 1) * HEAD_DIM] for g in range(GROUP)], axis=0)
    rows = GROUP * tq

    def step(kt, v_ones, m, acc):
        s = _dot(qs, kt)
        m_new = jnp.maximum(m, jnp.max(s, axis=-1, keepdims=True))
        p = jnp.exp2((s - m_new).astype(BF16))
        acc = jnp.exp2(m - m_new) * acc + _dot(p, v_ones)
        return m_new, acc

    m0 = jnp.full((rows, 1), -1e30, F32)
    a0 = jnp.zeros((rows, 2 * HEAD_DIM), F32)
    carry = step(ktc_ref[0], vc_ref[...], m0, a0)
    if has_lat:
        tk = ktl_ref.shape[-1]

        def body(j, c):
            off = pl.multiple_of(j * tk, tk)
            return step(ktl_ref[j], vl_ref[pl.ds(off, tk), :], *c)

        carry = lax.fori_loop(0, n_lat_chunks, body, carry, unroll=2)
    _, acc = carry
    o = acc[:, :HEAD_DIM] / acc[:, HEAD_DIM:HEAD_DIM + 1]
    o_ref[...] = jnp.concatenate([o[g * tq:(g + 1) * tq] for g in range(GROUP)], axis=1).astype(o_ref.dtype)


def _attention(q, ktc, vc, ktl=None, vl=None):
    b, lq, qw = q.shape
    tq = min(ATTN_Q_TILE, lq)
    gw = GROUP * HEAD_DIM
    lc = vc.shape[2]
    has_lat = ktl is not None
    in_specs = [pl.BlockSpec((None, tq, gw), lambda bb, h, i: (bb, i, h)),
                pl.BlockSpec((None, None, 1, HEAD_DIM, lc), lambda bb, h, i: (bb, h, 0, 0, 0)),
                pl.BlockSpec((None, None, lc, 2 * HEAD_DIM), lambda bb, h, i: (bb, h, 0, 0))]
    args = [q, ktc, vc]
    n_chunks = 0
    if has_lat:
        n_chunks, tk = ktl.shape[2], ktl.shape[4]
        ll = vl.shape[2]
        in_specs += [pl.BlockSpec((None, None, n_chunks, HEAD_DIM, tk), lambda bb, h, i: (bb, h, 0, 0, 0)),
                     pl.BlockSpec((None, None, ll, 2 * HEAD_DIM), lambda bb, h, i: (bb, h, 0, 0))]
        args += [ktl, vl]
    return pl.pallas_call(
        functools.partial(_attn_kernel, tq=tq, n_lat_chunks=n_chunks, has_lat=has_lat),
        grid=(b, KV_HEADS, lq // tq),
        in_specs=in_specs,
        out_specs=pl.BlockSpec((None, tq, gw), lambda bb, h, i: (bb, i, h)),
        out_shape=jax.ShapeDtypeStruct((b, lq, qw), BF16),
        compiler_params=_cparams(("arbitrary", "arbitrary", "arbitrary")),
        name="gqa_attention",
    )(*args)


def _dnprep_kernel(qp, qc, qn, kp, kc, kn, vp, vc, vn, ab_ref, cw_ref, alog_ref, dtb_ref, gm_ref, trip_ref, tris_ref,
                   qo_ref, ko_ref, kto_ref, vo_ref, gc_ref, gct_ref, *, tm):
    t = pl.program_id(1)
    nt = pl.num_programs(1)
    mp = (t > 0).astype(F32)
    mn = (t < nt - 1).astype(F32)
    n_ext = tm + 2 * SUBLANES
    pad = DN_CONV // 2

    def conv(p_ref, c_ref, n_ref, w):
        ext = jnp.concatenate([p_ref[...] * mp, c_ref[...], n_ref[...] * mn], axis=0)
        acc = None
        for j in range(DN_CONV):
            s = (pad - j) % n_ext
            r = ext if s == 0 else pltpu.roll(ext, s, 0)
            term = r[SUBLANES:SUBLANES + tm] * w[j:j + 1, :]
            acc = term if acc is None else acc + term
        return acc * jax.nn.sigmoid(acc)

    cw = cw_ref[...]
    hw = DN_HEADS * DN_DK
    gm = gm_ref[...]
    q = conv(qp, qc, qn, cw[:, 0:hw])
    q = q * lax.rsqrt(_dot_exact_rhs(q * q, gm) + EPS) * (DN_DK ** -0.5)
    k = conv(kp, kc, kn, cw[:, hw:2 * hw])
    k = k * lax.rsqrt(_dot_exact_rhs(k * k, gm) + EPS)
    v = conv(vp, vc, vn, cw[:, 2 * hw:3 * hw])
    kt = k.T
    for h in range(DN_HEADS):
        sl = slice(h * DN_DK, (h + 1) * DN_DK)
        qo_ref[h] = q[:, sl]
        ko_ref[h] = k[:, sl]
        vo_ref[h] = v[:, sl]
        for c in range(tm // DN_CHUNK):
            kto_ref[h, c] = kt[sl, c * DN_CHUNK:(c + 1) * DN_CHUNK]

    ab = ab_ref[...]
    lane = lax.broadcasted_iota(I32, ab.shape, 1)
    a = ab + dtb_ref[...]
    sp = jnp.maximum(a, 0.0) + jnp.log(1.0 + jnp.exp(-jnp.abs(a)))
    g = -jnp.exp(alog_ref[...]) * sp
    g = jnp.where(lane < 2 * DN_HEADS, g, 0.0)
    gpre = _dot_exact_lhs(trip_ref[...], g)
    gsuf = _dot_exact_lhs(tris_ref[...], g)
    gc = jnp.where(lane < DN_HEADS, gpre, jnp.where(lane < 2 * DN_HEADS, gsuf, jax.nn.sigmoid(ab)))
    gc_ref[...] = gc
    gt = gc.T
    for c in range(tm // DN_CHUNK):
        gct_ref[c] = gt[:4 * DN_HEADS, c * DN_CHUNK:(c + 1) * DN_CHUNK]


def _dnprep(proj, conv_w, a_log, dt_bias):
    b, l, _ = proj.shape
    tm = min(TOKEN_TILE, l)
    hw = DN_HEADS * DN_DK
    nc = l // DN_CHUNK
    r8 = tm // SUBLANES
    last8 = l // SUBLANES - 1

    def cur(col):
        return pl.BlockSpec((None, tm, hw), lambda bb, t: (bb, t, col // hw))

    def prev(col):
        return pl.BlockSpec((None, SUBLANES, hw), lambda bb, t: (bb, jnp.maximum(t * r8 - 1, 0), col // hw))

    def nxt(col):
        return pl.BlockSpec((None, SUBLANES, hw), lambda bb, t: (bb, jnp.minimum((t + 1) * r8, last8), col // hw))

    const = lambda shape: pl.BlockSpec(shape, lambda bb, t: (0,) * len(shape))
    alog = jnp.zeros((1, LANES), F32).at[0, :2 * DN_HEADS].set(a_log.astype(F32).reshape(-1))
    dtb = jnp.zeros((1, LANES), F32).at[0, :2 * DN_HEADS].set(dt_bias.astype(F32).reshape(-1))
    ch = jnp.arange(tm) // DN_CHUNK
    same = ch[:, None] == ch[None, :]
    idx = jnp.arange(tm)
    trip = (same & (idx[:, None] >= idx[None, :])).astype(BF16)
    tris = (same & (idx[:, None] <= idx[None, :])).astype(BF16)
    gmat = _group_matrix(hw, 1.0)
    in_specs = []
    args = []
    for col in (COL_DNQ, COL_DNK, COL_DNV):
        in_specs += [prev(col), cur(col), nxt(col)]
        args += [proj, proj, proj]
    in_specs += [pl.BlockSpec((None, tm, LANES), lambda bb, t: (bb, t, COL_AB // LANES)),
                 const((DN_CONV, 3 * hw)), const((1, LANES)), const((1, LANES)), const((hw, hw)),
                 const((tm, tm)), const((tm, tm))]
    args += [proj, conv_w.astype(F32), alog, dtb, gmat, trip, tris]
    head_spec = pl.BlockSpec((None, DN_HEADS, tm, DN_DK), lambda bb, t: (bb, 0, t, 0))
    head_shape = jax.ShapeDtypeStruct((b, DN_HEADS, l, DN_DK), F32)
    return pl.pallas_call(
        functools.partial(_dnprep_kernel, tm=tm),
        grid=(b, l // tm),
        in_specs=in_specs,
        out_specs=[head_spec, head_spec,
                   pl.BlockSpec((None, DN_HEADS, tm // DN_CHUNK, DN_DK, DN_CHUNK), lambda bb, t: (bb, 0, t, 0, 0)),
                   head_spec,
                   pl.BlockSpec((None, tm, LANES), lambda bb, t: (bb, t, 0)),
                   pl.BlockSpec((None, tm // DN_CHUNK, 4 * DN_HEADS, DN_CHUNK), lambda bb, t: (bb, t, 0, 0))],
        out_shape=[head_shape, head_shape,
                   jax.ShapeDtypeStruct((b, DN_HEADS, nc, DN_DK, DN_CHUNK), F32),
                   head_shape,
                   jax.ShapeDtypeStruct((b, l, LANES), F32),
                   jax.ShapeDtypeStruct((b, nc, 4 * DN_HEADS, DN_CHUNK), F32)],
        compiler_params=_cparams(("arbitrary", "arbitrary")),
        name="dn_prep",
    )(*args)


def _bmm(a, b):
    return jnp.einsum('nij,njk->nik', a.astype(BF16), b.astype(BF16), preferred_element_type=F32)


def _unit_tri_inverse(a, eye, blks):
    inner = jnp.where(blks[0], a, 0.0)
    x = eye - inner
    p = inner
    for _ in range(2):
        p = _bmm(p, p)
        x = x + _bmm(x, p)
    for level in range(1, len(blks) + 1):
        outer = jnp.where(blks[level], a, 0.0) if level < len(blks) else a
        x = x - _bmm(x, _bmm(outer - inner, x))
        inner = outer
    return x


def _dn_local(q, k, kt, v, gcol, grow, beta, g_last, incl, strict, eye, blks):
    decay = jnp.where(incl, jnp.exp(jnp.where(incl, gcol - grow, 0.0)), 0.0)
    kb = k * beta
    a = jnp.where(strict, _bmm(kb, kt) * decay, 0.0)
    x = _unit_tri_inverse(a, eye, blks)
    u = _bmm(x, v * beta)
    w = _bmm(x, kb * jnp.exp(gcol))
    qk = _bmm(q, kt) * decay
    return u, w, qk, q * jnp.exp(gcol), kt * jnp.exp(g_last - grow), jnp.exp(g_last)


def _dnscan_kernel(qf, kf, ktf, vf, gcf, gctf, qb, kb_, ktb, vb, gcb, gctb, s0_ref, of_ref, ob_ref, sout_ref, s_scr, *, cb):
    i = pl.program_id(1)
    last = pl.num_programs(1) - 1
    nh = DN_HEADS

    @pl.when(i == 0)
    def _():
        s_scr[...] = s0_ref[...]

    row = lax.broadcasted_iota(I32, (DN_CHUNK, DN_CHUNK), 0)
    col = lax.broadcasted_iota(I32, (DN_CHUNK, DN_CHUNK), 1)
    eye = (row == col).astype(F32)
    blks = tuple((row // n) == (col // n) for n in (8, 16, 32))
    dirs = ((0, qf, kf, ktf, vf, gcf, gctf, row >= col, row > col),
            (1, qb, kb_, ktb, vb, gcb, gctb, row <= col, row < col))
    local = []
    for d, q_ref, k_ref, kt_ref, v_ref, gc_ref, gct_ref, incl, strict in dirs:
        last_row = DN_CHUNK - 1 if d == 0 else 0
        qs, ks, kts, vs, gcols, grows, betas, glasts = [], [], [], [], [], [], [], []
        for c in range(cb):
            rs = slice(c * DN_CHUNK, (c + 1) * DN_CHUNK)
            gc = gc_ref[rs, :]
            gct = gct_ref[c]
            for h in range(nh):
                gl = d * nh + h
                bl = 2 * nh + d * nh + h
                qs.append(q_ref[h, rs, :])
                ks.append(k_ref[h, rs, :])
                kts.append(kt_ref[h, c])
                vs.append(v_ref[h, rs, :])
                gcols.append(gc[:, gl:gl + 1])
                grows.append(gct[gl:gl + 1, :])
                betas.append(gc[:, bl:bl + 1])
                glasts.append(gc[last_row:last_row + 1, gl:gl + 1])
        st = lambda xs: jnp.stack(xs, axis=0)
        local.append(_dn_local(st(qs), st(ks), st(kts), st(vs), st(gcols), st(grows), st(betas), st(glasts),
                               incl, strict, eye, blks))

    s_all = jnp.concatenate([s_scr[0], s_scr[1]], axis=0)
    for jj in range(cb):
        sel = (slice(jj * nh, (jj + 1) * nh), slice((cb - 1 - jj) * nh, (cb - jj) * nh))
        u, w, qk, qg, kdt, el = (jnp.concatenate([local[0][t][sel[0]], local[1][t][sel[1]]], axis=0) for t in range(6))
        v_new = u - _bmm(w, s_all)
        o = _bmm(qg, s_all) + _bmm(qk, v_new)
        s_all = s_all * el + _bmm(kdt, v_new)
        for d, o_ref in ((0, of_ref), (1, ob_ref)):
            c = jj if d == 0 else cb - 1 - jj
            for h in range(nh):
                o_ref[h, c * DN_CHUNK:(c + 1) * DN_CHUNK, :] = o[d * nh + h]
    s_scr[0] = s_all[:nh]
    s_scr[1] = s_all[nh:]

    @pl.when(i == last)
    def _():
        sout_ref[...] = s_scr[...]


def _dnscan(q, k, kt, v, gc, gct, s0):
    b, nh, l, dk = q.shape
    cb = min(DN_BLOCK_CHUNKS, l // DN_CHUNK)
    tb = cb * DN_CHUNK
    nblk = l // tb
    fwd = lambda i: i
    bwd = lambda i: nblk - 1 - i

    def specs(ix):
        head = pl.BlockSpec((None, nh, tb, dk), lambda bb, i: (bb, 0, ix(i), 0))
        return [head, head,
                pl.BlockSpec((None, nh, cb, dk, DN_CHUNK), lambda bb, i: (bb, 0, ix(i), 0, 0)),
                head,
                pl.BlockSpec((None, tb, LANES), lambda bb, i: (bb, ix(i), 0)),
                pl.BlockSpec((None, cb, 4 * DN_HEADS, DN_CHUNK), lambda bb, i: (bb, ix(i), 0, 0))]

    state_spec = pl.BlockSpec((None, 2, nh, dk, dk), lambda bb, i: (bb, 0, 0, 0, 0))
    o_shape = jax.ShapeDtypeStruct((b, nh, l, dk), F32)
    return pl.pallas_call(
        functools.partial(_dnscan_kernel, cb=cb),
        grid=(b, nblk),
        in_specs=specs(fwd) + specs(bwd) + [state_spec],
        out_specs=[pl.BlockSpec((None, nh, tb, dk), lambda bb, i: (bb, 0, fwd(i), 0)),
                   pl.BlockSpec((None, nh, tb, dk), lambda bb, i: (bb, 0, bwd(i), 0)),
                   state_spec],
        out_shape=[o_shape, o_shape, jax.ShapeDtypeStruct((b, 2, nh, dk, dk), F32)],
        scratch_shapes=[pltpu.VMEM((2, nh, dk, dk), F32)],
        compiler_params=_cparams(("arbitrary", "arbitrary")),
        name="dn_scan",
    )(q, k, kt, v, gc, gct, q, k, kt, v, gc, gct, s0)


def _mix0_out_kernel(x_ref, g1_ref, a_ref, of_ref, ob_ref, z_ref, nw_ref, w_ref, o_ref):
    o = of_ref[...] + ob_ref[...]
    on = o * lax.rsqrt(jnp.mean(o * o, axis=-1, keepdims=True) + EPS) * nw_ref[...]
    ot = jnp.concatenate([on[h] for h in range(DN_HEADS)], axis=1)
    z = z_ref[...]
    bmix = ot * (z * jax.nn.sigmoid(z))
    mix = jnp.concatenate([a_ref[...], bmix.astype(BF16)], axis=1)
    o_ref[...] = x_ref[...] + g1_ref[0] * _dot(mix, w_ref[...])


def _mix0_out(x, mod3, layer, row_fn, attn, o_f, o_b, proj, norm_w, w_out_bf16):
    b, l, d = x.shape
    tm = min(TOKEN_TILE, l)
    aw = Q_HEADS * HEAD_DIM
    zw = DN_HEADS * DN_DK
    head = pl.BlockSpec((None, DN_HEADS, tm, DN_DK), lambda bb, t: (bb, 0, t, 0))
    return pl.pallas_call(
        _mix0_out_kernel,
        grid=(b, l // tm),
        in_specs=[pl.BlockSpec((None, tm, d), lambda bb, t: (bb, t, 0)),
                  pl.BlockSpec((1, 1, d), _mod_spec(layer, row_fn, MOD_G1)),
                  pl.BlockSpec((None, tm, aw), lambda bb, t: (bb, t, 0)),
                  head, head,
                  pl.BlockSpec((None, tm, zw), lambda bb, t: (bb, t, COL_Z // zw)),
                  pl.BlockSpec((1, DN_DK), lambda bb, t: (0, 0)),
                  pl.BlockSpec((aw + zw, d), lambda bb, t: (0, 0))],
        out_specs=pl.BlockSpec((None, tm, d), lambda bb, t: (bb, t, 0)),
        out_shape=jax.ShapeDtypeStruct((b, l, d), F32),
        compiler_params=_cparams(("arbitrary", "arbitrary")),
        name="mix0_out",
    )(x, mod3, attn, o_f, o_b, proj, norm_w.astype(F32).reshape(1, DN_DK), w_out_bf16)


def _s5_kernel(x_ref, sh_ref, sc_ref, wb_ref, wc_ref, lam_ref, x0_ref, y_ref, xfin_ref, bre, bim, st, *, tm, pitch):
    d = pl.program_id(1)
    i = pl.program_id(2)
    nsub = SUBLANES
    nj = bre.shape[0]
    h = _modulated(x_ref, sh_ref, sc_ref).astype(BF16)
    half = nj * LANES
    for s in range(nsub):
        o = _dot(h[:, s * LANES:(s + 1) * LANES], wb_ref[s])
        for j in range(nj):
            bre[j, s * pitch:s * pitch + tm, :] = o[:, j * LANES:(j + 1) * LANES]
            bim[j, s * pitch:s * pitch + tm, :] = o[:, half + j * LANES:half + (j + 1) * LANES]

    @pl.when(i == 0)
    def _():
        st[...] = x0_ref[...]

    lam_re = lam_ref[0]
    lam_im = lam_ref[1]

    def body(k, carry):
        xr, xi = carry
        t = jnp.where(d == 0, k, tm - 1 - k)
        br = jnp.concatenate([bre[j, pl.ds(t, nsub, stride=pitch), :] for j in range(nj)], axis=1)
        bi = jnp.concatenate([bim[j, pl.ds(t, nsub, stride=pitch), :] for j in range(nj)], axis=1)
        nr = lam_re * xr - lam_im * xi + br
        ni = lam_re * xi + lam_im * xr + bi
        for j in range(nj):
            bre[j, pl.ds(t, nsub, stride=pitch), :] = nr[:, j * LANES:(j + 1) * LANES]
            bim[j, pl.ds(t, nsub, stride=pitch), :] = ni[:, j * LANES:(j + 1) * LANES]
        return nr, ni

    xr, xi = lax.fori_loop(0, tm, body, (st[0], st[1]), unroll=8)
    st[0] = xr
    st[1] = xi
    xfin_ref[0] = xr
    xfin_ref[1] = xi
    for s in range(nsub):
        parts = [bre[j, s * pitch:s * pitch + tm, :] for j in range(nj)]
        parts += [bim[j, s * pitch:s * pitch + tm, :] for j in range(nj)]
        xs = jnp.concatenate(parts, axis=1).astype(BF16)
        y_ref[:, s * LANES:(s + 1) * LANES] = _dot(xs, wc_ref[s])


def _s5_scan(x, mod3, layer, row_fn, wb, wc, lam, x0):
    b, l, d = x.shape
    tm = min(TOKEN_TILE, l)
    nt = l // tm
    pitch = tm + S5_PITCH_PAD
    sw = wb.shape[-1] // 2
    nj = sw // LANES
    tile = lambda dd, i: jnp.where(dd == 0, i, nt - 1 - i)
    mspec = lambda which: pl.BlockSpec((1, 1, d), lambda bb, dd, i: _mod_spec(layer, row_fn, which)(bb, i))
    return pl.pallas_call(
        functools.partial(_s5_kernel, tm=tm, pitch=pitch),
        grid=(b, 2, nt),
        in_specs=[pl.BlockSpec((None, tm, d), lambda bb, dd, i: (bb, tile(dd, i), 0)),
                  mspec(MOD_SH1), mspec(MOD_SC1),
                  pl.BlockSpec((None, SUBLANES, LANES, 2 * sw), lambda bb, dd, i: (dd, 0, 0, 0)),
                  pl.BlockSpec((None, SUBLANES, 2 * sw, LANES), lambda bb, dd, i: (dd, 0, 0, 0)),
                  pl.BlockSpec((None, 2, SUBLANES, sw), lambda bb, dd, i: (dd, 0, 0, 0)),
                  pl.BlockSpec((None, None, 2, SUBLANES, sw), lambda bb, dd, i: (bb, dd, 0, 0, 0))],
        out_specs=[pl.BlockSpec((None, None, tm, d), lambda bb, dd, i: (dd, bb, tile(dd, i), 0)),
                   pl.BlockSpec((None, None, 2, SUBLANES, sw), lambda bb, dd, i: (bb, dd, 0, 0, 0))],
        out_shape=[jax.ShapeDtypeStruct((2, b, l, d), F32),
                   jax.ShapeDtypeStruct((b, 2, 2, SUBLANES, sw), F32)],
        scratch_shapes=[pltpu.VMEM((nj, SUBLANES * pitch, LANES), F32),
                        pltpu.VMEM((nj, SUBLANES * pitch, LANES), F32),
                        pltpu.VMEM((2, SUBLANES, sw), F32)],
        compiler_params=_cparams(("arbitrary", "arbitrary", "arbitrary")),
        name="s5_scan",
    )(x, mod3, mod3, wb, wc, lam, x0)


def _s5_weights(lam_re, lam_im, log_step, b_re, b_im, c_re, c_im):
    lam = lax.complex(lam_re.astype(F32), lam_im.astype(F32))
    step = jnp.exp(log_step.astype(F32))[..., None]
    lam_bar = jnp.exp(lam * step)
    b_bar = ((lam_bar - 1.0) / lam)[..., None] * lax.complex(b_re.astype(F32), b_im.astype(F32))
    ng, p = lam.shape[1], lam.shape[2]
    gl = ng // SUBLANES
    eye = jnp.eye(gl, dtype=F32)

    def wb_of(part):
        t = part.reshape(2, SUBLANES, gl, p, S5_GROUP)
        w = jnp.einsum('dsgpi,gh->dsgihp', t, eye)
        return w.reshape(2, SUBLANES, gl * S5_GROUP, gl * p)

    def wc_of(part):
        t = part.reshape(2, SUBLANES, gl, S5_GROUP, p)
        w = jnp.einsum('dsgip,gh->dshpgi', t, eye)
        return w.reshape(2, SUBLANES, gl * p, gl * S5_GROUP)

    wb = jnp.concatenate([wb_of(jnp.real(b_bar)), wb_of(jnp.imag(b_bar))], axis=-1).astype(BF16)
    wc = jnp.concatenate([wc_of(c_re.astype(F32)), -wc_of(c_im.astype(F32))], axis=-2).astype(BF16)
    lam_t = jnp.stack([jnp.real(lam_bar).reshape(2, SUBLANES, gl * p),
                       jnp.imag(lam_bar).reshape(2, SUBLANES, gl * p)], axis=1)
    return wb, wc, lam_t


def _s5_glu_kernel(x_ref, sh_ref, sc_ref, g1_ref, yf_ref, yb_ref, dsk_ref, w1_ref, b1_ref, w2_ref, b2_ref, o_ref):
    x = x_ref[...]
    h = _rms(x) * (1.0 + sc_ref[0]) + sh_ref[0]
    y = dsk_ref[...] * h + yf_ref[...] + yb_ref[...]
    z = jax.nn.gelu(y).astype(BF16)
    o = (_dot(z, w1_ref[...]) + b1_ref[...]) * jax.nn.sigmoid(_dot(z, w2_ref[...]) + b2_ref[...])
    o_ref[...] = x + g1_ref[0] * o


def _s5_glu(x, mod3, layer, row_fn, y2, d_skip, w1, b1, w2, b2):
    b, l, d = x.shape
    tm = min(TOKEN_TILE, l)
    mspec = lambda which: pl.BlockSpec((1, 1, d), _mod_spec(layer, row_fn, which))
    const = lambda shape: pl.BlockSpec(shape, lambda bb, t: (0,) * len(shape))
    return pl.pallas_call(
        _s5_glu_kernel,
        grid=(b, l // tm),
        in_specs=[pl.BlockSpec((None, tm, d), lambda bb, t: (bb, t, 0)),
                  mspec(MOD_SH1), mspec(MOD_SC1), mspec(MOD_G1),
                  pl.BlockSpec((None, None, tm, d), lambda bb, t: (0, bb, t, 0)),
                  pl.BlockSpec((None, None, tm, d), lambda bb, t: (1, bb, t, 0)),
                  const((1, d)), const((d, d)), const((1, d)), const((d, d)), const((1, d))],
        out_specs=pl.BlockSpec((None, tm, d), lambda bb, t: (bb, t, 0)),
        out_shape=jax.ShapeDtypeStruct((b, l, d), F32),
        compiler_params=_cparams(("arbitrary", "arbitrary")),
        name="s5_glu",
    )(x, mod3, mod3, mod3, y2, y2, d_skip.astype(F32).reshape(1, d), w1.astype(BF16), b1.astype(F32).reshape(1, d),
      w2.astype(BF16), b2.astype(F32).reshape(1, d))


def _route_kernel(x_ref, sh_ref, sc_ref, rw_ref, rb_ref, cin_ref, tri_ref, hp_ref, meta_ref, cnt_ref, carry):
    first = (pl.program_id(0) == 0) & (pl.program_id(1) == 0)

    @pl.when(first)
    def _():
        carry[...] = cin_ref[...]

    h = _modulated(x_ref, sh_ref, sc_ref)
    half = h.shape[1] // 2
    lo = pltpu.bitcast(h[:, :half].astype(BF16).astype(F32), U32)
    hi = pltpu.bitcast(h[:, half:].astype(BF16).astype(F32), U32)
    packed = (lo >> 16) | (hi & jnp.uint32(0xFFFF0000))
    pieces = half // LANES
    for j in range(pieces):
        hp_ref[pl.ds(j, h.shape[0], stride=pieces), :] = packed[:, j * LANES:(j + 1) * LANES]

    logits = _dot_hi(h, rw_ref[...]) + rb_ref[...]
    lane = lax.broadcasted_iota(I32, logits.shape, 1)
    cur = jnp.where(lane < N_EXPERTS, logits, -jnp.inf)
    vals, idxs = [], []
    hot = jnp.zeros(logits.shape, F32)
    for _ in range(TOP_K):
        m = jnp.max(cur, axis=-1, keepdims=True)
        idx = jnp.min(jnp.where(cur == m, lane, LANES), axis=-1, keepdims=True)
        sel = lane == idx
        hot = hot + sel.astype(F32)
        cur = jnp.where(sel, -jnp.inf, cur)
        vals.append(m)
        idxs.append(idx)
    es = [jnp.exp(v - vals[0]) for v in vals]
    tot = es[0] + es[1] + es[2] + es[3]
    before = _dot(tri_ref[...], hot.astype(BF16)) + carry[...]
    meta = jnp.zeros(logits.shape, F32)
    for k in range(TOP_K):
        rank = jnp.sum(jnp.where(lane == idxs[k], before, 0.0), axis=-1, keepdims=True)
        meta = jnp.where(lane == k, idxs[k].astype(F32), meta)
        meta = jnp.where(lane == TOP_K + k, es[k] / tot, meta)
        meta = jnp.where(lane == 2 * TOP_K + k, rank, meta)
    meta_ref[...] = meta
    carry[...] = carry[...] + jnp.sum(hot, axis=0, keepdims=True)
    cnt_ref[...] = carry[...]


def _route(x, mod3, layer, row_fn, rw_pad, rb_pad, counts_in):
    b, l, d = x.shape
    tm = min(MOE_TOKEN_TILE, l)
    idx = jnp.arange(tm)
    tri = (idx[:, None] > idx[None, :]).astype(BF16)
    mspec = lambda which: pl.BlockSpec((1, 1, d), _mod_spec(layer, row_fn, which))
    const = lambda shape: pl.BlockSpec(shape, lambda bb, t: (0,) * len(shape))
    return pl.pallas_call(
        _route_kernel,
        grid=(b, l // tm),
        in_specs=[pl.BlockSpec((None, tm, d), lambda bb, t: (bb, t, 0)),
                  mspec(MOD_SH2), mspec(MOD_SC2),
                  const((d, LANES)), const((1, LANES)), const((1, LANES)), const((tm, tm))],
        out_specs=[pl.BlockSpec((None, tm * (d // 2 // LANES), LANES), lambda bb, t: (bb, t, 0)),
                   pl.BlockSpec((None, tm, LANES), lambda bb, t: (bb, t, 0)),
                   const((1, LANES))],
        out_shape=[jax.ShapeDtypeStruct((b, l * (d // 2 // LANES), LANES), U32),
                   jax.ShapeDtypeStruct((b, l, LANES), F32),
                   jax.ShapeDtypeStruct((1, LANES), F32)],
        scratch_shapes=[pltpu.VMEM((1, LANES), F32)],
        compiler_params=_cparams(("arbitrary", "arbitrary")),
        name="moe_route",
    )(x, mod3, mod3, rw_pad, rb_pad, counts_in, tri)


def _dest_kernel(meta_ref, ps_ref, o_ref):
    meta = meta_ref[...]
    lane = lax.broadcasted_iota(I32, meta.shape, 1)
    ps = ps_ref[...]
    dst = jnp.zeros(meta.shape, F32)
    for k in range(TOP_K):
        idx = meta[:, k:k + 1].astype(I32)
        base = jnp.sum(jnp.where(lane == idx, ps, 0.0), axis=-1, keepdims=True)
        dst = jnp.where(lane == k, base + meta[:, 2 * TOP_K + k:2 * TOP_K + k + 1], dst)
    dt = dst.T
    o_ref[0] = jnp.concatenate([dt[k:k + 1] for k in range(TOP_K)], axis=1).astype(I32)


def _dest(meta, pad_start_row):
    b, l, _ = meta.shape
    tt = min(MOE_TOKEN_TILE, l)
    nt = l // tt
    return pl.pallas_call(
        _dest_kernel,
        grid=(b, nt),
        in_specs=[pl.BlockSpec((None, tt, LANES), lambda bb, t: (bb, t, 0)),
                  pl.BlockSpec((1, LANES), lambda bb, t: (0, 0))],
        out_specs=pl.BlockSpec((1, 1, TOP_K * tt), lambda bb, t: (bb * nt + t, 0, 0)),
        out_shape=jax.ShapeDtypeStruct((b * nt, 1, TOP_K * tt), I32),
        compiler_params=_cparams(("arbitrary", "arbitrary")),
        name="moe_dest",
    )(meta, pad_start_row)


def _dispatch_kernel(dest_hbm, hp_ref, xs_in, xs_out, dest_smem, sem_i, sem, *, tt, pieces):
    del xs_in
    i = pl.program_id(0)
    cp = pltpu.make_async_copy(dest_hbm.at[i, 0], dest_smem, sem_i)
    cp.start()
    cp.wait()

    def row_copy(r, k):
        src = pl.multiple_of(r * pieces, pieces)
        dst = pl.multiple_of(dest_smem[k * tt + r] * pieces, pieces)
        return pltpu.make_async_copy(hp_ref.at[pl.ds(src, pieces)], xs_out.at[pl.ds(dst, pieces)], sem)

    def issue(r, c):
        for k in range(TOP_K):
            row_copy(r, k).start(priority=k % 2)
        return c

    lax.fori_loop(0, tt, issue, 0, unroll=4)

    def drain(r, c):
        for k in range(TOP_K):
            row_copy(r, k).wait()
        return c

    lax.fori_loop(0, tt, drain, 0, unroll=4)


def _dispatch(dest, hp, xs, pieces):
    n = hp.shape[0] // pieces
    tt = dest.shape[-1] // TOP_K
    return pl.pallas_call(
        functools.partial(_dispatch_kernel, tt=tt, pieces=pieces),
        grid=(n // tt,),
        in_specs=[pl.BlockSpec(memory_space=pl.ANY),
                  pl.BlockSpec((tt * pieces, LANES), lambda i: (i, 0)),
                  pl.BlockSpec(memory_space=pl.ANY)],
        out_specs=pl.BlockSpec(memory_space=pl.ANY),
        out_shape=jax.ShapeDtypeStruct(xs.shape, xs.dtype),
        scratch_shapes=[pltpu.SMEM((tt * TOP_K,), I32), pltpu.SemaphoreType.DMA, pltpu.SemaphoreType.DMA],
        input_output_aliases={2: 0},
        compiler_params=_cparams(("arbitrary",)),
        name="moe_dispatch",
    )(dest, hp, xs)


def _expert_kernel(be_ref, nact_ref, xs_ref, wgu_ref, bgu_ref, wd_ref, bd_ref, y_ref, wgu_bf, wd_bf):
    i = pl.program_id(0)
    e = be_ref[i]
    prev = be_ref[jnp.maximum(i - 1, 0)]

    @pl.when((i == 0) | (e != prev))
    def _():
        wgu_bf[...] = wgu_ref[0].astype(BF16)
        wd_bf[...] = wd_ref[0].astype(BF16)

    rows = MOE_ROWS
    in_pieces = xs_ref.shape[0] // rows
    out_pieces = y_ref.shape[0] // rows

    @pl.when(i < nact_ref[0])
    def _():
        ws = [xs_ref[pl.ds(j, rows, stride=in_pieces), :] for j in range(in_pieces)]
        lo = [pltpu.bitcast(w << 16, F32) for w in ws]
        hi = [pltpu.bitcast(w & jnp.uint32(0xFFFF0000), F32) for w in ws]
        x = jnp.concatenate(lo + hi, axis=1).astype(BF16)
        gu = _dot(x, wgu_bf[...]) + bgu_ref[0]
        de = gu.shape[1] // 2
        gate = jnp.minimum(gu[:, :de], SWIGLU_LIMIT)
        up = jnp.clip(gu[:, de:], -SWIGLU_LIMIT, SWIGLU_LIMIT)
        act = (up + 1.0) * gate * jax.nn.sigmoid(SWIGLU_ALPHA * gate)
        y = _dot(act.astype(BF16), wd_bf[...]) + bd_ref[0]
        for j in range(out_pieces):
            y_ref[pl.ds(j, rows, stride=out_pieces), :] = y[:, j * LANES:(j + 1) * LANES]

    @pl.when(i >= nact_ref[0])
    def _():
        y_ref[...] = jnp.zeros(y_ref.shape, y_ref.dtype)


def _experts(block_e, n_active, xs, w_gu, b_gu, w_d, b_d, layer):
    _, ne, d, de2 = w_gu.shape
    in_pieces = d // 2 // LANES
    out_pieces = d // LANES
    n_rows = xs.shape[0] // in_pieces
    nblk = n_rows // MOE_ROWS
    grid_spec = pltpu.PrefetchScalarGridSpec(
        num_scalar_prefetch=2,
        grid=(nblk,),
        in_specs=[pl.BlockSpec((MOE_ROWS * in_pieces, LANES), lambda i, be, na: (i, 0)),
                  pl.BlockSpec((None, 1, d, de2), lambda i, be, na: (layer, be[i], 0, 0)),
                  pl.BlockSpec((None, 1, 1, de2), lambda i, be, na: (layer, be[i], 0, 0)),
                  pl.BlockSpec((None, 1, de2 // 2, d), lambda i, be, na: (layer, be[i], 0, 0)),
                  pl.BlockSpec((None, 1, 1, d), lambda i, be, na: (layer, be[i], 0, 0))],
        out_specs=pl.BlockSpec((MOE_ROWS * out_pieces, LANES), lambda i, be, na: (i, 0)),
        scratch_shapes=[pltpu.VMEM((d, de2), BF16), pltpu.VMEM((de2 // 2, d), BF16)],
    )
    depth = w_gu.shape[0]
    return pl.pallas_call(
        _expert_kernel,
        grid_spec=grid_spec,
        out_shape=jax.ShapeDtypeStruct((n_rows * out_pieces, LANES), F32),
        compiler_params=_cparams(("arbitrary",)),
        name="moe_experts",
    )(block_e, n_active, xs, w_gu, b_gu.reshape(depth, ne, 1, de2), w_d, b_d.reshape(depth, ne, 1, d))


def _combine_kernel(*refs, tt, final):
    if final:
        dest_hbm, yb_hbm, x_ref, meta_ref, g2_ref, fw_ref, o_ref, dest_smem, buf, sem_i, sem = refs
    else:
        dest_hbm, yb_hbm, x_ref, meta_ref, g2_ref, o_ref, dest_smem, buf, sem_i, sem = refs
    i = pl.program_id(0) * pl.num_programs(1) + pl.program_id(1)
    cp = pltpu.make_async_copy(dest_hbm.at[i, 0], dest_smem, sem_i)
    cp.start()
    cp.wait()

    pieces = buf.shape[1] // tt

    def row_copy(r, k):
        src = pl.multiple_of(dest_smem[k * tt + r] * pieces, pieces)
        dst = pl.multiple_of(r * pieces, pieces)
        return pltpu.make_async_copy(yb_hbm.at[pl.ds(src, pieces)], buf.at[k, pl.ds(dst, pieces)], sem)

    def issue(r, c):
        for k in range(TOP_K):
            row_copy(r, k).start(priority=k % 2)
        return c

    lax.fori_loop(0, tt, issue, 0, unroll=4)

    def drain(r, c):
        for k in range(TOP_K):
            row_copy(r, k).wait()
        return c

    lax.fori_loop(0, tt, drain, 0, unroll=4)
    meta = meta_ref[...]
    g2 = g2_ref[0]
    cols = []
    for j in range(pieces):
        y = meta[:, TOP_K:TOP_K + 1] * buf[0, pl.ds(j, tt, stride=pieces), :]
        for k in range(1, TOP_K):
            y = y + meta[:, TOP_K + k:TOP_K + k + 1] * buf[k, pl.ds(j, tt, stride=pieces), :]
        sl = slice(j * LANES, (j + 1) * LANES)
        cols.append(x_ref[:, sl] + g2[:, sl] * y)
    xn = jnp.concatenate(cols, axis=1)
    if final:
        xn = _rms(xn) * fw_ref[...]
    o_ref[...] = xn


def _combine(dest, yb, x, meta, mod3, layer, row_fn, final_w=None):
    b, l, d = x.shape
    tt = min(MOE_TOKEN_TILE, l)
    nt = l // tt
    final = final_w is not None
    in_specs = [pl.BlockSpec(memory_space=pl.ANY), pl.BlockSpec(memory_space=pl.ANY),
                pl.BlockSpec((None, tt, d), lambda bb, t: (bb, t, 0)),
                pl.BlockSpec((None, tt, LANES), lambda bb, t: (bb, t, 0)),
                pl.BlockSpec((1, 1, d), _mod_spec(layer, row_fn, MOD_G2))]
    args = [dest, yb, x, meta, mod3]
    if final:
        in_specs.append(pl.BlockSpec((1, d), lambda bb, t: (0, 0)))
        args.append(final_w.astype(F32).reshape(1, d))
    return pl.pallas_call(
        functools.partial(_combine_kernel, tt=tt, final=final),
        grid=(b, nt),
        in_specs=in_specs,
        out_specs=pl.BlockSpec((None, tt, d), lambda bb, t: (bb, t, 0)),
        out_shape=jax.ShapeDtypeStruct((b, l, d), F32),
        scratch_shapes=[pltpu.SMEM((tt * TOP_K,), I32), pltpu.VMEM((TOP_K, tt * (d // LANES), LANES), F32),
                        pltpu.SemaphoreType.DMA, pltpu.SemaphoreType.DMA],
        compiler_params=_cparams(("arbitrary", "arbitrary")),
        name="moe_combine",
    )(*args)


def _moe(parts, mod3, layer, router_w, router_b, w_gu, b_gu, w_d, b_d, final_w=None):
    d = router_w.shape[0]
    rw_pad = jnp.zeros((d, LANES), F32).at[:, :N_EXPERTS].set(router_w.astype(F32))
    rb_pad = jnp.zeros((1, LANES), F32).at[0, :N_EXPERTS].set(router_b.astype(F32))
    counts = jnp.zeros((1, LANES), F32)
    routed = []
    for x, row_fn in parts:
        hp, meta, counts = _route(x, mod3, layer, row_fn, rw_pad, rb_pad, counts)
        routed.append((hp, meta))
    n_assign = sum(x.shape[0] * x.shape[1] for x, _ in parts) * TOP_K
    nblk = -(-(n_assign + N_EXPERTS * (MOE_ROWS - 1)) // MOE_ROWS)
    cnt = counts[0, :N_EXPERTS].astype(I32)
    padded = (cnt + MOE_ROWS - 1) // MOE_ROWS * MOE_ROWS
    pad_end = jnp.cumsum(padded)
    pad_start = pad_end - padded
    blk_row = jnp.arange(nblk, dtype=I32)[:, None] * MOE_ROWS
    block_e = jnp.minimum(jnp.sum((pad_end[None, :] <= blk_row).astype(I32), axis=1), N_EXPERTS - 1).astype(I32)
    n_active = (pad_end[-1] // MOE_ROWS).astype(I32).reshape(1)
    ps_row = jnp.zeros((1, LANES), F32).at[0, :N_EXPERTS].set(pad_start.astype(F32))
    pieces = d // 2 // LANES
    xs = jnp.zeros((nblk * MOE_ROWS * pieces, LANES), U32)
    dests = []
    for (x, _), (hp, meta) in zip(parts, routed):
        dest = _dest(meta, ps_row)
        dests.append(dest)
        xs = _dispatch(dest, hp.reshape(-1, LANES), xs, pieces)
    yb = _experts(block_e, n_active, xs, w_gu, b_gu, w_d, b_d, layer)
    outs = []
    for idx, ((x, row_fn), (hp, meta)) in enumerate(zip(parts, routed)):
        fw = final_w if idx == 0 else None
        outs.append(_combine(dests[idx], yb, x, meta, mod3, layer, row_fn, fw))
    return outs


def kernel(x, c, ctx, c_ctx, ada_w, ada_b, ab_w_in, ab_q_gain, ab_k_gain, dn_conv_w, dn_a_log, dn_dt_bias, dn_norm_w, ab_w_out, s5_lambda_re, s5_lambda_im, s5_log_step, s5_b_re, s5_b_im, s5_c_re, s5_c_im, s5_d, s5_glu_w1, s5_glu_b1, s5_glu_w2, s5_glu_b2, moe_router_w, moe_router_b, moe_w_gate_up, moe_b_gate_up, moe_w_down, moe_b_down, final_norm_w):
    b, l, d = x.shape
    lc = ctx.shape[1]
    depth = ada_w.shape[0]
    assert depth == 2 and b < MOD_ROWS
    lat_row = lambda bb: bb
    ctx_row = lambda bb: b

    rows = jnp.zeros((MOD_ROWS, d), F32).at[:b].set(c.astype(F32)).at[b].set(c_ctx.astype(F32))
    mod3 = _ada(rows, ada_w, ada_b).reshape(depth * MOD_ROWS * 6, 1, d)

    w_in = ab_w_in[0]
    aq, akv, bqk = Q_HEADS * HEAD_DIM, KV_HEADS * HEAD_DIM, DN_HEADS * DN_DK
    o_k, o_v, o_dn, o_ab, o_z = aq, aq + akv, aq + 2 * akv, aq + 2 * akv + 3 * bqk, aq + 2 * akv + 3 * bqk + 4 * DN_HEADS
    w_perm = jnp.concatenate([w_in[:, :aq], w_in[:, o_dn:o_ab], w_in[:, o_z:], w_in[:, o_k:o_v], w_in[:, o_v:o_dn],
                              w_in[:, o_ab:o_z], jnp.zeros((d, IN_COLS_PAD - w_in.shape[1]), w_in.dtype)], axis=1).astype(BF16)
    proj_l = _inproj(x, mod3, 0, lat_row, w_perm)
    proj_c = _inproj(ctx, mod3, 0, ctx_row, w_perm)

    cos, sin = _rope_tables(l)
    ones = jnp.ones((lc, KV_HEADS * HEAD_DIM), F32)
    q_l, kt_l, v_l = _qkprep(proj_l, ab_q_gain[0], ab_k_gain[0], cos, sin, min(ATTN_K_TILE, l))
    q_c, kt_c, v_c = _qkprep(proj_c, ab_q_gain[0], ab_k_gain[0], ones, jnp.zeros_like(ones), lc)
    a_l = _attention(q_l, kt_c, v_c, kt_l, v_l)
    a_c = _attention(q_c, kt_c, v_c)

    dq_c, dk_c, dkt_c, dv_c, gc_c, gct_c = _dnprep(proj_c, dn_conv_w[0], dn_a_log[0], dn_dt_bias[0])
    dq_l, dk_l, dkt_l, dv_l, gc_l, gct_l = _dnprep(proj_l, dn_conv_w[0], dn_a_log[0], dn_dt_bias[0])
    s_zero = jnp.zeros((b, 2, DN_HEADS, DN_DK, DN_DK), F32)
    of_c, ob_c, s_ctx = _dnscan(dq_c, dk_c, dkt_c, dv_c, gc_c, gct_c, s_zero)
    of_l, ob_l, _ = _dnscan(dq_l, dk_l, dkt_l, dv_l, gc_l, gct_l, s_ctx)

    w_out = ab_w_out[0].astype(BF16)
    x = _mix0_out(x, mod3, 0, lat_row, a_l, of_l, ob_l, proj_l, dn_norm_w[0], w_out)
    ctx = _mix0_out(ctx, mod3, 0, ctx_row, a_c, of_c, ob_c, proj_c, dn_norm_w[0], w_out)

    x, ctx = _moe([(x, lat_row), (ctx, ctx_row)], mod3, 0, moe_router_w[0], moe_router_b[0],
                  moe_w_gate_up, moe_b_gate_up, moe_w_down, moe_b_down)

    wb, wc, lam_t = _s5_weights(s5_lambda_re[0], s5_lambda_im[0], s5_log_step[0], s5_b_re[0], s5_b_im[0],
                                s5_c_re[0], s5_c_im[0])
    sw = lam_t.shape[-1]
    x0 = jnp.zeros((b, 2, 2, SUBLANES, sw), F32)
    _, x_ctx = _s5_scan(ctx, mod3, 1, ctx_row, wb, wc, lam_t, x0)
    y2, _ = _s5_scan(x, mod3, 1, lat_row, wb, wc, lam_t, x_ctx)
    x = _s5_glu(x, mod3, 1, lat_row, y2, s5_d[0], s5_glu_w1[0], s5_glu_b1[0], s5_glu_w2[0], s5_glu_b2[0])

    (x,) = _moe([(x, lat_row)], mod3, 1, moe_router_w[1], moe_router_b[1], moe_w_gate_up, moe_b_gate_up,
                moe_w_down, moe_b_down, final_w=final_norm_w)
    return x
```

```python
import functools
import math

import jax
import jax.numpy as jnp
from jax import lax
from jax.experimental import pallas as pl
from jax.experimental.pallas import tpu as pltpu

F32 = jnp.float32
BF16 = jnp.bfloat16
U32 = jnp.uint32
I32 = jnp.int32

EPS = 1e-6
GRID_W = 64
HEAD_DIM = 64
Q_HEADS = 8
KV_HEADS = 2
GROUP = Q_HEADS // KV_HEADS
ROPE_THETA = 10000.0
DN_HEADS = 8
DN_DK = 64
DN_CONV = 5
DN_CHUNK = 64
S5_GROUP = 16
S5_STATE = 64
N_EXPERTS = 32
TOP_K = 4
SWIGLU_LIMIT = 7.0
SWIGLU_ALPHA = 1.702

LANES = 128
SUBLANES = 8
TOKEN_TILE = 512
MOE_TOKEN_TILE = 1024
ATTN_Q_TILE = 256
ATTN_K_TILE = 2048
DN_BLOCK_CHUNKS = 4
MOE_ROWS = 512
S5_PITCH_PAD = 4
VMEM_LIMIT = 56 * 1024 * 1024

MOD_SH1, MOD_SC1, MOD_G1, MOD_SH2, MOD_SC2, MOD_G2 = range(6)
MOD_ROWS = 8

COL_Q = 0
COL_DNQ = 512
COL_DNK = 1024
COL_DNV = 1536
COL_Z = 2048
COL_K = 2560
COL_V = 2688
COL_AB = 2816
IN_COLS_PAD = 2944


def _cparams(sem):
    return pltpu.CompilerParams(dimension_semantics=sem, vmem_limit_bytes=VMEM_LIMIT)


def _split3(a):
    a1 = a.astype(BF16)
    r = a - a1.astype(F32)
    a2 = r.astype(BF16)
    a3 = (r - a2.astype(F32)).astype(BF16)
    return a1, a2, a3


def _dot(a, b):
    return jnp.dot(a, b, preferred_element_type=F32)


def _dot_exact_lhs(m_bf16, x):
    x1, x2, x3 = _split3(x)
    return _dot(m_bf16, x1) + _dot(m_bf16, x2) + _dot(m_bf16, x3)


def _dot_exact_rhs(x, m_bf16):
    x1, x2, x3 = _split3(x)
    return _dot(x1, m_bf16) + _dot(x2, m_bf16) + _dot(x3, m_bf16)


def _dot_hi(a, b):
    a1, a2, _ = _split3(a)
    b1, b2, _ = _split3(b)
    return _dot(a1, b1) + _dot(a1, b2) + _dot(a2, b1)


def _rms(x):
    return x * lax.rsqrt(jnp.mean(x * x, axis=-1, keepdims=True) + EPS)


def _modulated(x_ref, sh_ref, sc_ref):
    return _rms(x_ref[...]) * (1.0 + sc_ref[0]) + sh_ref[0]


def _mod_spec(layer, row_fn, which):
    d = None

    def imap(b, t):
        return ((layer * MOD_ROWS + row_fn(b)) * 6 + which, 0, 0)

    return imap


def _ada_kernel(a_ref, w_ref, b_ref, o_ref):
    a = a_ref[...]
    a = a * jax.nn.sigmoid(a)
    o_ref[0] = _dot_hi(a, w_ref[0]) + b_ref[0]


def _ada(rows, ada_w, ada_b):
    depth, d, n = ada_w.shape
    tn = 1536
    return pl.pallas_call(
        _ada_kernel,
        grid=(depth, n // tn),
        in_specs=[pl.BlockSpec((MOD_ROWS, d), lambda l, j: (0, 0)),
                  pl.BlockSpec((1, d, tn), lambda l, j: (l, 0, j)),
                  pl.BlockSpec((1, 1, tn), lambda l, j: (l, 0, j))],
        out_specs=pl.BlockSpec((1, MOD_ROWS, tn), lambda l, j: (l, 0, j)),
        out_shape=jax.ShapeDtypeStruct((depth, MOD_ROWS, n), F32),
        compiler_params=_cparams(("arbitrary", "arbitrary")),
        name="ada_mod",
    )(rows, ada_w, ada_b.reshape(depth, 1, n))


def _inproj_kernel(x_ref, sh_ref, sc_ref, w_ref, o_ref):
    h = _modulated(x_ref, sh_ref, sc_ref)
    o_ref[...] = _dot(h.astype(BF16), w_ref[...])


def _inproj(x, mod3, layer, row_fn, w_bf16):
    b, l, d = x.shape
    tm = min(TOKEN_TILE, l)
    n = w_bf16.shape[1]
    mspec = lambda which: pl.BlockSpec((1, 1, d), _mod_spec(layer, row_fn, which))
    return pl.pallas_call(
        _inproj_kernel,
        grid=(b, l // tm),
        in_specs=[pl.BlockSpec((None, tm, d), lambda bb, t: (bb, t, 0)),
                  mspec(MOD_SH1), mspec(MOD_SC1),
                  pl.BlockSpec((d, n), lambda bb, t: (0, 0))],
        out_specs=pl.BlockSpec((None, tm, n), lambda bb, t: (bb, t, 0)),
        out_shape=jax.ShapeDtypeStruct((b, l, n), F32),
        compiler_params=_cparams(("arbitrary", "arbitrary")),
        name="in_proj",
    )(x, mod3, mod3, w_bf16)


def _head_rot(x, width):
    lane = lax.broadcasted_iota(I32, x.shape, 1)
    first = (lane % HEAD_DIM) < (HEAD_DIM // 2)
    return jnp.where(first, pltpu.roll(x, width - HEAD_DIM // 2, 1), pltpu.roll(x, HEAD_DIM // 2, 1))


def _qkprep_kernel(q_ref, k_ref, v_ref, qg_ref, kg_ref, cos_ref, sin_ref, gq_ref, gk_ref, qo_ref, kt_ref, vo_ref):
    cos = cos_ref[...]
    sin = sin_ref[...]
    q = q_ref[...]
    qn = q * lax.rsqrt(_dot_exact_rhs(q * q, gq_ref[...]) + EPS) * qg_ref[...]
    cos4 = jnp.concatenate([cos] * 4, axis=1)
    sin4 = jnp.concatenate([sin] * 4, axis=1)
    qr = qn * cos4 + _head_rot(qn, Q_HEADS * HEAD_DIM) * sin4
    qo_ref[...] = (qr * (HEAD_DIM ** -0.5 * math.log2(math.e))).astype(BF16)
    k = k_ref[...]
    kn = k * lax.rsqrt(_dot_exact_rhs(k * k, gk_ref[...]) + EPS) * kg_ref[...]
    kr = kn * cos + _head_rot(kn, KV_HEADS * HEAD_DIM) * sin
    kt = kr.T
    kt_ref[0] = kt[:HEAD_DIM].astype(BF16)
    kt_ref[1] = kt[HEAD_DIM:].astype(BF16)
    v = v_ref[...]
    ones = jnp.ones((v.shape[0], HEAD_DIM), F32)
    vo_ref[0] = jnp.concatenate([v[:, :HEAD_DIM], ones], axis=1).astype(BF16)
    vo_ref[1] = jnp.concatenate([v[:, HEAD_DIM:], ones], axis=1).astype(BF16)


def _group_matrix(width, scale):
    g = jnp.arange(width) // HEAD_DIM
    return ((g[:, None] == g[None, :]).astype(F32) * scale).astype(BF16)


def _qkprep(proj, q_gain, k_gain, cos, sin, tk):
    b, l, _ = proj.shape
    qw = Q_HEADS * HEAD_DIM
    kw = KV_HEADS * HEAD_DIM
    qg = jnp.tile(q_gain.astype(F32), Q_HEADS).reshape(1, qw)
    kg = jnp.tile(k_gain.astype(F32), KV_HEADS).reshape(1, kw)
    const = lambda shape: pl.BlockSpec(shape, lambda bb, t: (0,) * len(shape))
    return pl.pallas_call(
        _qkprep_kernel,
        grid=(b, l // tk),
        in_specs=[pl.BlockSpec((None, tk, qw), lambda bb, t: (bb, t, COL_Q // qw)),
                  pl.BlockSpec((None, tk, kw), lambda bb, t: (bb, t, COL_K // kw)),
                  pl.BlockSpec((None, tk, kw), lambda bb, t: (bb, t, COL_V // kw)),
                  const((1, qw)), const((1, kw)),
                  pl.BlockSpec((tk, kw), lambda bb, t: (t, 0)),
                  pl.BlockSpec((tk, kw), lambda bb, t: (t, 0)),
                  const((qw, qw)), const((kw, kw))],
        out_specs=[pl.BlockSpec((None, tk, qw), lambda bb, t: (bb, t, 0)),
                   pl.BlockSpec((None, KV_HEADS, None, HEAD_DIM, tk), lambda bb, t: (bb, 0, t, 0, 0)),
                   pl.BlockSpec((None, KV_HEADS, tk, 2 * HEAD_DIM), lambda bb, t: (bb, 0, t, 0))],
        out_shape=[jax.ShapeDtypeStruct((b, l, qw), BF16),
                   jax.ShapeDtypeStruct((b, KV_HEADS, l // tk, HEAD_DIM, tk), BF16),
                   jax.ShapeDtypeStruct((b, KV_HEADS, l, 2 * HEAD_DIM), BF16)],
        compiler_params=_cparams(("arbitrary", "arbitrary")),
        name="qk_prep",
    )(proj, proj, proj, qg, kg, cos, sin, _group_matrix(qw, 1.0 / HEAD_DIM), _group_matrix(kw, 1.0 / HEAD_DIM))


def _rope_tables(length):
    rows = length // GRID_W
    row = jnp.broadcast_to(jnp.arange(rows, dtype=I32)[:, None], (rows, GRID_W)).reshape(-1).astype(F32)
    col = jnp.broadcast_to(jnp.arange(GRID_W, dtype=I32)[None, :], (rows, GRID_W)).reshape(-1).astype(F32)
    n_axis = HEAD_DIM // 4
    inv_freq = ROPE_THETA ** (-jnp.arange(n_axis, dtype=F32) / n_axis)
    ang = jnp.concatenate([row[:, None] * inv_freq, col[:, None] * inv_freq], axis=-1)
    c, s = jnp.cos(ang), jnp.sin(ang)
    cos_h = jnp.concatenate([c, c], axis=-1)
    sin_h = jnp.concatenate([-s, s], axis=-1)
    return jnp.tile(cos_h, (1, KV_HEADS)), jnp.tile(sin_h, (1, KV_HEADS))


def _attn_kernel(*refs, tq, n_lat_chunks, has_lat):
    if has_lat:
        q_ref, ktc_ref, vc_ref, ktl_ref, vl_ref, o_ref = refs
    else:
        q_ref, ktc_ref, vc_ref, o_ref = refs
    q = q_ref[...]
    qs = jnp.concatenate([q[:, g * HEAD_DIM:(g + 1) * HEAD_DIM] for g in range(GROUP)], axis=0)
    rows = GROUP * tq

    def step(kt, v_ones, m, acc):
        s = _dot(qs, kt)
        m_new = jnp.maximum(m, jnp.max(s, axis=-1, keepdims=True))
        p = jnp.exp2((s - m_new).astype(BF16))
        acc = jnp.exp2(m - m_new) * acc + _dot(p, v_ones)
        return m_new, acc

    m0 = jnp.full((rows, 1), -1e30, F32)
    a0 = jnp.zeros((rows, 2 * HEAD_DIM), F32)
    carry = step(ktc_ref[0], vc_ref[...], m0, a0)
    if has_lat:
        tk = ktl_ref.shape[-1]

        def body(j, c):
            off = pl.multiple_of(j * tk, tk)
            return step(ktl_ref[j], vl_ref[pl.ds(off, tk), :], *c)

        carry = lax.fori_loop(0, n_lat_chunks, body, carry, unroll=True)
    _, acc = carry
    o = acc[:, :HEAD_DIM] / acc[:, HEAD_DIM:HEAD_DIM + 1]
    o_ref[...] = jnp.concatenate([o[g * tq:(g + 1) * tq] for g in range(GROUP)], axis=1).astype(o_ref.dtype)


def _attention(q, ktc, vc, ktl=None, vl=None):
    b, lq, qw = q.shape
    tq = min(ATTN_Q_TILE, lq)
    gw = GROUP * HEAD_DIM
    lc = vc.shape[2]
    has_lat = ktl is not None
    in_specs = [pl.BlockSpec((None, tq, gw), lambda bb, h, i: (bb, i, h)),
                pl.BlockSpec((None, None, 1, HEAD_DIM, lc), lambda bb, h, i: (bb, h, 0, 0, 0)),
                pl.BlockSpec((None, None, lc, 2 * HEAD_DIM), lambda bb, h, i: (bb, h, 0, 0))]
    args = [q, ktc, vc]
    n_chunks = 0
    if has_lat:
        n_chunks, tk = ktl.shape[2], ktl.shape[4]
        ll = vl.shape[2]
        in_specs += [pl.BlockSpec((None, None, n_chunks, HEAD_DIM, tk), lambda bb, h, i: (bb, h, 0, 0, 0)),
                     pl.BlockSpec((None, None, ll, 2 * HEAD_DIM), lambda bb, h, i: (bb, h, 0, 0))]
        args += [ktl, vl]
    return pl.pallas_call(
        functools.partial(_attn_kernel, tq=tq, n_lat_chunks=n_chunks, has_lat=has_lat),
        grid=(b, KV_HEADS, lq // tq),
        in_specs=in_specs,
        out_specs=pl.BlockSpec((None, tq, gw), lambda bb, h, i: (bb, i, h)),
        out_shape=jax.ShapeDtypeStruct((b, lq, qw), BF16),
        compiler_params=_cparams(("arbitrary", "arbitrary", "arbitrary")),
        name="gqa_attention",
    )(*args)


def _dnprep_kernel(qp, qc, qn, kp, kc, kn, vp, vc, vn, ab_ref, cw_ref, alog_ref, dtb_ref, gm_ref, trip_ref, tris_ref,
                   qo_ref, ko_ref, kto_ref, vo_ref, gc_ref, gct_ref, *, tm):
    t = pl.program_id(1)
    nt = pl.num_programs(1)
    mp = (t > 0).astype(F32)
    mn = (t < nt - 1).astype(F32)
    n_ext = tm + 2 * SUBLANES
    pad = DN_CONV // 2

    def conv(p_ref, c_ref, n_ref, w):
        ext = jnp.concatenate([p_ref[...] * mp, c_ref[...], n_ref[...] * mn], axis=0)
        acc = None
        for j in range(DN_CONV):
            s = (pad - j) % n_ext
            r = ext if s == 0 else pltpu.roll(ext, s, 0)
            term = r[SUBLANES:SUBLANES + tm] * w[j:j + 1, :]
            acc = term if acc is None else acc + term
        return acc * jax.nn.sigmoid(acc)

    cw = cw_ref[...]
    hw = DN_HEADS * DN_DK
    gm = gm_ref[...]
    q = conv(qp, qc, qn, cw[:, 0:hw])
    q = q * lax.rsqrt(_dot_exact_rhs(q * q, gm) + EPS) * (DN_DK ** -0.5)
    k = conv(kp, kc, kn, cw[:, hw:2 * hw])
    k = k * lax.rsqrt(_dot_exact_rhs(k * k, gm) + EPS)
    v = conv(vp, vc, vn, cw[:, 2 * hw:3 * hw])
    kt = k.T
    for h in range(DN_HEADS):
        sl = slice(h * DN_DK, (h + 1) * DN_DK)
        qo_ref[h] = q[:, sl]
        ko_ref[h] = k[:, sl]
        vo_ref[h] = v[:, sl]
        for c in range(tm // DN_CHUNK):
            kto_ref[h, c] = kt[sl, c * DN_CHUNK:(c + 1) * DN_CHUNK]

    ab = ab_ref[...]
    lane = lax.broadcasted_iota(I32, ab.shape, 1)
    a = ab + dtb_ref[...]
    sp = jnp.maximum(a, 0.0) + jnp.log(1.0 + jnp.exp(-jnp.abs(a)))
    g = -jnp.exp(alog_ref[...]) * sp
    g = jnp.where(lane < 2 * DN_HEADS, g, 0.0)
    gpre = _dot_exact_lhs(trip_ref[...], g)
    gsuf = _dot_exact_lhs(tris_ref[...], g)
    gc = jnp.where(lane < DN_HEADS, gpre, jnp.where(lane < 2 * DN_HEADS, gsuf, jax.nn.sigmoid(ab)))
    gc_ref[...] = gc
    gt = gc.T
    for c in range(tm // DN_CHUNK):
        gct_ref[c] = gt[:4 * DN_HEADS, c * DN_CHUNK:(c + 1) * DN_CHUNK]


def _dnprep(proj, conv_w, a_log, dt_bias):
    b, l, _ = proj.shape
    tm = min(TOKEN_TILE, l)
    hw = DN_HEADS * DN_DK
    nc = l // DN_CHUNK
    r8 = tm // SUBLANES
    last8 = l // SUBLANES - 1

    def cur(col):
        return pl.BlockSpec((None, tm, hw), lambda bb, t: (bb, t, col // hw))

    def prev(col):
        return pl.BlockSpec((None, SUBLANES, hw), lambda bb, t: (bb, jnp.maximum(t * r8 - 1, 0), col // hw))

    def nxt(col):
        return pl.BlockSpec((None, SUBLANES, hw), lambda bb, t: (bb, jnp.minimum((t + 1) * r8, last8), col // hw))

    const = lambda shape: pl.BlockSpec(shape, lambda bb, t: (0,) * len(shape))
    alog = jnp.zeros((1, LANES), F32).at[0, :2 * DN_HEADS].set(a_log.astype(F32).reshape(-1))
    dtb = jnp.zeros((1, LANES), F32).at[0, :2 * DN_HEADS].set(dt_bias.astype(F32).reshape(-1))
    ch = jnp.arange(tm) // DN_CHUNK
    same = ch[:, None] == ch[None, :]
    idx = jnp.arange(tm)
    trip = (same & (idx[:, None] >= idx[None, :])).astype(BF16)
    tris = (same & (idx[:, None] <= idx[None, :])).astype(BF16)
    gmat = _group_matrix(hw, 1.0)
    in_specs = []
    args = []
    for col in (COL_DNQ, COL_DNK, COL_DNV):
        in_specs += [prev(col), cur(col), nxt(col)]
        args += [proj, proj, proj]
    in_specs += [pl.BlockSpec((None, tm, LANES), lambda bb, t: (bb, t, COL_AB // LANES)),
                 const((DN_CONV, 3 * hw)), const((1, LANES)), const((1, LANES)), const((hw, hw)),
                 const((tm, tm)), const((tm, tm))]
    args += [proj, conv_w.astype(F32), alog, dtb, gmat, trip, tris]
    head_spec = pl.BlockSpec((None, DN_HEADS, tm, DN_DK), lambda bb, t: (bb, 0, t, 0))
    head_shape = jax.ShapeDtypeStruct((b, DN_HEADS, l, DN_DK), F32)
    return pl.pallas_call(
        functools.partial(_dnprep_kernel, tm=tm),
        grid=(b, l // tm),
        in_specs=in_specs,
        out_specs=[head_spec, head_spec,
                   pl.BlockSpec((None, DN_HEADS, tm // DN_CHUNK, DN_DK, DN_CHUNK), lambda bb, t: (bb, 0, t, 0, 0)),
                   head_spec,
                   pl.BlockSpec((None, tm, LANES), lambda bb, t: (bb, t, 0)),
                   pl.BlockSpec((None, tm // DN_CHUNK, 4 * DN_HEADS, DN_CHUNK), lambda bb, t: (bb, t, 0, 0))],
        out_shape=[head_shape, head_shape,
                   jax.ShapeDtypeStruct((b, DN_HEADS, nc, DN_DK, DN_CHUNK), F32),
                   head_shape,
                   jax.ShapeDtypeStruct((b, l, LANES), F32),
                   jax.ShapeDtypeStruct((b, nc, 4 * DN_HEADS, DN_CHUNK), F32)],
        compiler_params=_cparams(("arbitrary", "arbitrary")),
        name="dn_prep",
    )(*args)


def _bmm(a, b):
    return jnp.einsum('nij,njk->nik', a.astype(BF16), b.astype(BF16), preferred_element_type=F32)


def _unit_tri_inverse(a, eye, blks):
    inner = jnp.where(blks[0], a, 0.0)
    x = eye - inner
    p = inner
    for _ in range(2):
        p = _bmm(p, p)
        x = x + _bmm(x, p)
    for level in range(1, len(blks) + 1):
        outer = jnp.where(blks[level], a, 0.0) if level < len(blks) else a
        x = x - _bmm(x, _bmm(outer - inner, x))
        inner = outer
    return x


def _dn_local(q, k, kt, v, gcol, grow, beta, g_last, incl, strict, eye, blks):
    decay = jnp.where(incl, jnp.exp(jnp.where(incl, gcol - grow, 0.0)), 0.0)
    kb = k * beta
    a = jnp.where(strict, _bmm(kb, kt) * decay, 0.0)
    x = _unit_tri_inverse(a, eye, blks)
    u = _bmm(x, v * beta)
    w = _bmm(x, kb * jnp.exp(gcol))
    qk = _bmm(q, kt) * decay
    return u, w, qk, q * jnp.exp(gcol), kt * jnp.exp(g_last - grow), jnp.exp(g_last)


def _dnscan_kernel(qf, kf, ktf, vf, gcf, gctf, qb, kb_, ktb, vb, gcb, gctb, s0_ref, of_ref, ob_ref, sout_ref, s_scr, *, cb):
    i = pl.program_id(1)
    last = pl.num_programs(1) - 1
    nh = DN_HEADS

    @pl.when(i == 0)
    def _():
        s_scr[...] = s0_ref[...]

    row = lax.broadcasted_iota(I32, (DN_CHUNK, DN_CHUNK), 0)
    col = lax.broadcasted_iota(I32, (DN_CHUNK, DN_CHUNK), 1)
    eye = (row == col).astype(F32)
    blks = tuple((row // n) == (col // n) for n in (8, 16, 32))
    dirs = ((0, qf, kf, ktf, vf, gcf, gctf, row >= col, row > col),
            (1, qb, kb_, ktb, vb, gcb, gctb, row <= col, row < col))
    local = []
    for d, q_ref, k_ref, kt_ref, v_ref, gc_ref, gct_ref, incl, strict in dirs:
        last_row = DN_CHUNK - 1 if d == 0 else 0
        qs, ks, kts, vs, gcols, grows, betas, glasts = [], [], [], [], [], [], [], []
        for c in range(cb):
            rs = slice(c * DN_CHUNK, (c + 1) * DN_CHUNK)
            gc = gc_ref[rs, :]
            gct = gct_ref[c]
            for h in range(nh):
                gl = d * nh + h
                bl = 2 * nh + d * nh + h
                qs.append(q_ref[h, rs, :])
                ks.append(k_ref[h, rs, :])
                kts.append(kt_ref[h, c])
                vs.append(v_ref[h, rs, :])
                gcols.append(gc[:, gl:gl + 1])
                grows.append(gct[gl:gl + 1, :])
                betas.append(gc[:, bl:bl + 1])
                glasts.append(gc[last_row:last_row + 1, gl:gl + 1])
        st = lambda xs: jnp.stack(xs, axis=0)
        local.append(_dn_local(st(qs), st(ks), st(kts), st(vs), st(gcols), st(grows), st(betas), st(glasts),
                               incl, strict, eye, blks))

    s_all = jnp.concatenate([s_scr[0], s_scr[1]], axis=0)
    for jj in range(cb):
        sel = (slice(jj * nh, (jj + 1) * nh), slice((cb - 1 - jj) * nh, (cb - jj) * nh))
        u, w, qk, qg, kdt, el = (jnp.concatenate([local[0][t][sel[0]], local[1][t][sel[1]]], axis=0) for t in range(6))
        v_new = u - _bmm(w, s_all)
        o = _bmm(qg, s_all) + _bmm(qk, v_new)
        s_all = s_all * el + _bmm(kdt, v_new)
        for d, o_ref in ((0, of_ref), (1, ob_ref)):
            c = jj if d == 0 else cb - 1 - jj
            for h in range(nh):
                o_ref[h, c * DN_CHUNK:(c + 1) * DN_CHUNK, :] = o[d * nh + h]
    s_scr[0] = s_all[:nh]
    s_scr[1] = s_all[nh:]

    @pl.when(i == last)
    def _():
        sout_ref[...] = s_scr[...]


def _dnscan(q, k, kt, v, gc, gct, s0):
    b, nh, l, dk = q.shape
    cb = min(DN_BLOCK_CHUNKS, l // DN_CHUNK)
    tb = cb * DN_CHUNK
    nblk = l // tb
    fwd = lambda i: i
    bwd = lambda i: nblk - 1 - i

    def specs(ix):
        head = pl.BlockSpec((None, nh, tb, dk), lambda bb, i: (bb, 0, ix(i), 0))
        return [head, head,
                pl.BlockSpec((None, nh, cb, dk, DN_CHUNK), lambda bb, i: (bb, 0, ix(i), 0, 0)),
                head,
                pl.BlockSpec((None, tb, LANES), lambda bb, i: (bb, ix(i), 0)),
                pl.BlockSpec((None, cb, 4 * DN_HEADS, DN_CHUNK), lambda bb, i: (bb, ix(i), 0, 0))]

    state_spec = pl.BlockSpec((None, 2, nh, dk, dk), lambda bb, i: (bb, 0, 0, 0, 0))
    o_shape = jax.ShapeDtypeStruct((b, nh, l, dk), F32)
    return pl.pallas_call(
        functools.partial(_dnscan_kernel, cb=cb),
        grid=(b, nblk),
        in_specs=specs(fwd) + specs(bwd) + [state_spec],
        out_specs=[pl.BlockSpec((None, nh, tb, dk), lambda bb, i: (bb, 0, fwd(i), 0)),
                   pl.BlockSpec((None, nh, tb, dk), lambda bb, i: (bb, 0, bwd(i), 0)),
                   state_spec],
        out_shape=[o_shape, o_shape, jax.ShapeDtypeStruct((b, 2, nh, dk, dk), F32)],
        scratch_shapes=[pltpu.VMEM((2, nh, dk, dk), F32)],
        compiler_params=_cparams(("arbitrary", "arbitrary")),
        name="dn_scan",
    )(q, k, kt, v, gc, gct, q, k, kt, v, gc, gct, s0)


def _mix0_out_kernel(x_ref, g1_ref, a_ref, of_ref, ob_ref, z_ref, nw_ref, w_ref, o_ref):
    o = of_ref[...] + ob_ref[...]
    on = o * lax.rsqrt(jnp.mean(o * o, axis=-1, keepdims=True) + EPS) * nw_ref[...]
    ot = jnp.concatenate([on[h] for h in range(DN_HEADS)], axis=1)
    z = z_ref[...]
    bmix = ot * (z * jax.nn.sigmoid(z))
    mix = jnp.concatenate([a_ref[...], bmix.astype(BF16)], axis=1)
    o_ref[...] = x_ref[...] + g1_ref[0] * _dot(mix, w_ref[...])


def _mix0_out(x, mod3, layer, row_fn, attn, o_f, o_b, proj, norm_w, w_out_bf16):
    b, l, d = x.shape
    tm = min(TOKEN_TILE, l)
    aw = Q_HEADS * HEAD_DIM
    zw = DN_HEADS * DN_DK
    head = pl.BlockSpec((None, DN_HEADS, tm, DN_DK), lambda bb, t: (bb, 0, t, 0))
    return pl.pallas_call(
        _mix0_out_kernel,
        grid=(b, l // tm),
        in_specs=[pl.BlockSpec((None, tm, d), lambda bb, t: (bb, t, 0)),
                  pl.BlockSpec((1, 1, d), _mod_spec(layer, row_fn, MOD_G1)),
                  pl.BlockSpec((None, tm, aw), lambda bb, t: (bb, t, 0)),
                  head, head,
                  pl.BlockSpec((None, tm, zw), lambda bb, t: (bb, t, COL_Z // zw)),
                  pl.BlockSpec((1, DN_DK), lambda bb, t: (0, 0)),
                  pl.BlockSpec((aw + zw, d), lambda bb, t: (0, 0))],
        out_specs=pl.BlockSpec((None, tm, d), lambda bb, t: (bb, t, 0)),
        out_shape=jax.ShapeDtypeStruct((b, l, d), F32),
        compiler_params=_cparams(("arbitrary", "arbitrary")),
        name="mix0_out",
    )(x, mod3, attn, o_f, o_b, proj, norm_w.astype(F32).reshape(1, DN_DK), w_out_bf16)


def _s5_kernel(x_ref, sh_ref, sc_ref, wb_ref, wc_ref, lam_ref, x0_ref, y_ref, xfin_ref, bre, bim, st, *, tm, pitch):
    d = pl.program_id(1)
    i = pl.program_id(2)
    nsub = SUBLANES
    nj = bre.shape[0]
    h = _modulated(x_ref, sh_ref, sc_ref).astype(BF16)
    half = nj * LANES
    for s in range(nsub):
        o = _dot(h[:, s * LANES:(s + 1) * LANES], wb_ref[s])
        for j in range(nj):
            bre[j, s * pitch:s * pitch + tm, :] = o[:, j * LANES:(j + 1) * LANES]
            bim[j, s * pitch:s * pitch + tm, :] = o[:, half + j * LANES:half + (j + 1) * LANES]

    @pl.when(i == 0)
    def _():
        st[...] = x0_ref[...]

    lam_re = lam_ref[0]
    lam_im = lam_ref[1]

    def body(k, carry):
        xr, xi = carry
        t = jnp.where(d == 0, k, tm - 1 - k)
        br = jnp.concatenate([bre[j, pl.ds(t, nsub, stride=pitch), :] for j in range(nj)], axis=1)
        bi = jnp.concatenate([bim[j, pl.ds(t, nsub, stride=pitch), :] for j in range(nj)], axis=1)
        nr = lam_re * xr - lam_im * xi + br
        ni = lam_re * xi + lam_im * xr + bi
        for j in range(nj):
            bre[j, pl.ds(t, nsub, stride=pitch), :] = nr[:, j * LANES:(j + 1) * LANES]
            bim[j, pl.ds(t, nsub, stride=pitch), :] = ni[:, j * LANES:(j + 1) * LANES]
        return nr, ni

    xr, xi = lax.fori_loop(0, tm, body, (st[0], st[1]), unroll=8)
    st[0] = xr
    st[1] = xi
    xfin_ref[0] = xr
    xfin_ref[1] = xi
    for s in range(nsub):
        parts = [bre[j, s * pitch:s * pitch + tm, :] for j in range(nj)]
        parts += [bim[j, s * pitch:s * pitch + tm, :] for j in range(nj)]
        xs = jnp.concatenate(parts, axis=1).astype(BF16)
        y_ref[:, s * LANES:(s + 1) * LANES] = _dot(xs, wc_ref[s])


def _s5_scan(x, mod3, layer, row_fn, wb, wc, lam, x0):
    b, l, d = x.shape
    tm = min(TOKEN_TILE, l)
    nt = l // tm
    pitch = tm + S5_PITCH_PAD
    sw = wb.shape[-1] // 2
    nj = sw // LANES
    tile = lambda dd, i: jnp.where(dd == 0, i, nt - 1 - i)
    mspec = lambda which: pl.BlockSpec((1, 1, d), lambda bb, dd, i: _mod_spec(layer, row_fn, which)(bb, i))
    return pl.pallas_call(
        functools.partial(_s5_kernel, tm=tm, pitch=pitch),
        grid=(b, 2, nt),
        in_specs=[pl.BlockSpec((None, tm, d), lambda bb, dd, i: (bb, tile(dd, i), 0)),
                  mspec(MOD_SH1), mspec(MOD_SC1),
                  pl.BlockSpec((None, SUBLANES, LANES, 2 * sw), lambda bb, dd, i: (dd, 0, 0, 0)),
                  pl.BlockSpec((None, SUBLANES, 2 * sw, LANES), lambda bb, dd, i: (dd, 0, 0, 0)),
                  pl.BlockSpec((None, 2, SUBLANES, sw), lambda bb, dd, i: (dd, 0, 0, 0)),
                  pl.BlockSpec((None, None, 2, SUBLANES, sw), lambda bb, dd, i: (bb, dd, 0, 0, 0))],
        out_specs=[pl.BlockSpec((None, None, tm, d), lambda bb, dd, i: (dd, bb, tile(dd, i), 0)),
                   pl.BlockSpec((None, None, 2, SUBLANES, sw), lambda bb, dd, i: (bb, dd, 0, 0, 0))],
        out_shape=[jax.ShapeDtypeStruct((2, b, l, d), F32),
                   jax.ShapeDtypeStruct((b, 2, 2, SUBLANES, sw), F32)],
        scratch_shapes=[pltpu.VMEM((nj, SUBLANES * pitch, LANES), F32),
                        pltpu.VMEM((nj, SUBLANES * pitch, LANES), F32),
                        pltpu.VMEM((2, SUBLANES, sw), F32)],
        compiler_params=_cparams(("arbitrary", "arbitrary", "arbitrary")),
        name="s5_scan",
    )(x, mod3, mod3, wb, wc, lam, x0)


def _s5_weights(lam_re, lam_im, log_step, b_re, b_im, c_re, c_im):
    lam = lax.complex(lam_re.astype(F32), lam_im.astype(F32))
    step = jnp.exp(log_step.astype(F32))[..., None]
    lam_bar = jnp.exp(lam * step)
    b_bar = ((lam_bar - 1.0) / lam)[..., None] * lax.complex(b_re.astype(F32), b_im.astype(F32))
    ng, p = lam.shape[1], lam.shape[2]
    gl = ng // SUBLANES
    eye = jnp.eye(gl, dtype=F32)

    def wb_of(part):
        t = part.reshape(2, SUBLANES, gl, p, S5_GROUP)
        w = jnp.einsum('dsgpi,gh->dsgihp', t, eye)
        return w.reshape(2, SUBLANES, gl * S5_GROUP, gl * p)

    def wc_of(part):
        t = part.reshape(2, SUBLANES, gl, S5_GROUP, p)
        w = jnp.einsum('dsgip,gh->dshpgi', t, eye)
        return w.reshape(2, SUBLANES, gl * p, gl * S5_GROUP)

    wb = jnp.concatenate([wb_of(jnp.real(b_bar)), wb_of(jnp.imag(b_bar))], axis=-1).astype(BF16)
    wc = jnp.concatenate([wc_of(c_re.astype(F32)), -wc_of(c_im.astype(F32))], axis=-2).astype(BF16)
    lam_t = jnp.stack([jnp.real(lam_bar).reshape(2, SUBLANES, gl * p),
                       jnp.imag(lam_bar).reshape(2, SUBLANES, gl * p)], axis=1)
    return wb, wc, lam_t


def _s5_glu_kernel(x_ref, sh_ref, sc_ref, g1_ref, yf_ref, yb_ref, dsk_ref, w1_ref, b1_ref, w2_ref, b2_ref, o_ref):
    x = x_ref[...]
    h = _rms(x) * (1.0 + sc_ref[0]) + sh_ref[0]
    y = dsk_ref[...] * h + yf_ref[...] + yb_ref[...]
    z = jax.nn.gelu(y).astype(BF16)
    o = (_dot(z, w1_ref[...]) + b1_ref[...]) * jax.nn.sigmoid(_dot(z, w2_ref[...]) + b2_ref[...])
    o_ref[...] = x + g1_ref[0] * o


def _s5_glu(x, mod3, layer, row_fn, y2, d_skip, w1, b1, w2, b2):
    b, l, d = x.shape
    tm = min(TOKEN_TILE, l)
    mspec = lambda which: pl.BlockSpec((1, 1, d), _mod_spec(layer, row_fn, which))
    const = lambda shape: pl.BlockSpec(shape, lambda bb, t: (0,) * len(shape))
    return pl.pallas_call(
        _s5_glu_kernel,
        grid=(b, l // tm),
        in_specs=[pl.BlockSpec((None, tm, d), lambda bb, t: (bb, t, 0)),
                  mspec(MOD_SH1), mspec(MOD_SC1), mspec(MOD_G1),
                  pl.BlockSpec((None, None, tm, d), lambda bb, t: (0, bb, t, 0)),
                  pl.BlockSpec((None, None, tm, d), lambda bb, t: (1, bb, t, 0)),
                  const((1, d)), const((d, d)), const((1, d)), const((d, d)), const((1, d))],
        out_specs=pl.BlockSpec((None, tm, d), lambda bb, t: (bb, t, 0)),
        out_shape=jax.ShapeDtypeStruct((b, l, d), F32),
        compiler_params=_cparams(("arbitrary", "arbitrary")),
        name="s5_glu",
    )(x, mod3, mod3, mod3, y2, y2, d_skip.astype(F32).reshape(1, d), w1.astype(BF16), b1.astype(F32).reshape(1, d),
      w2.astype(BF16), b2.astype(F32).reshape(1, d))


def _route_kernel(x_ref, sh_ref, sc_ref, rw_ref, rb_ref, cin_ref, tri_ref, hp_ref, meta_ref, cnt_ref, carry):
    first = (pl.program_id(0) == 0) & (pl.program_id(1) == 0)

    @pl.when(first)
    def _():
        carry[...] = cin_ref[...]

    h = _modulated(x_ref, sh_ref, sc_ref)
    half = h.shape[1] // 2
    lo = pltpu.bitcast(h[:, :half].astype(BF16).astype(F32), U32)
    hi = pltpu.bitcast(h[:, half:].astype(BF16).astype(F32), U32)
    packed = (lo >> 16) | (hi & jnp.uint32(0xFFFF0000))
    pieces = half // LANES
    for j in range(pieces):
        hp_ref[pl.ds(j, h.shape[0], stride=pieces), :] = packed[:, j * LANES:(j + 1) * LANES]

    logits = _dot_hi(h, rw_ref[...]) + rb_ref[...]
    lane = lax.broadcasted_iota(I32, logits.shape, 1)
    cur = jnp.where(lane < N_EXPERTS, logits, -jnp.inf)
    vals, idxs = [], []
    hot = jnp.zeros(logits.shape, F32)
    for _ in range(TOP_K):
        m = jnp.max(cur, axis=-1, keepdims=True)
        idx = jnp.min(jnp.where(cur == m, lane, LANES), axis=-1, keepdims=True)
        sel = lane == idx
        hot = hot + sel.astype(F32)
        cur = jnp.where(sel, -jnp.inf, cur)
        vals.append(m)
        idxs.append(idx)
    es = [jnp.exp(v - vals[0]) for v in vals]
    tot = es[0] + es[1] + es[2] + es[3]
    before = _dot(tri_ref[...], hot.astype(BF16)) + carry[...]
    meta = jnp.zeros(logits.shape, F32)
    for k in range(TOP_K):
        rank = jnp.sum(jnp.where(lane == idxs[k], before, 0.0), axis=-1, keepdims=True)
        meta = jnp.where(lane == k, idxs[k].astype(F32), meta)
        meta = jnp.where(lane == TOP_K + k, es[k] / tot, meta)
        meta = jnp.where(lane == 2 * TOP_K + k, rank, meta)
    meta_ref[...] = meta
    carry[...] = carry[...] + jnp.sum(hot, axis=0, keepdims=True)
    cnt_ref[...] = carry[...]


def _route(x, mod3, layer, row_fn, rw_pad, rb_pad, counts_in):
    b, l, d = x.shape
    tm = min(MOE_TOKEN_TILE, l)
    idx = jnp.arange(tm)
    tri = (idx[:, None] > idx[None, :]).astype(BF16)
    mspec = lambda which: pl.BlockSpec((1, 1, d), _mod_spec(layer, row_fn, which))
    const = lambda shape: pl.BlockSpec(shape, lambda bb, t: (0,) * len(shape))
    return pl.pallas_call(
        _route_kernel,
        grid=(b, l // tm),
        in_specs=[pl.BlockSpec((None, tm, d), lambda bb, t: (bb, t, 0)),
                  mspec(MOD_SH2), mspec(MOD_SC2),
                  const((d, LANES)), const((1, LANES)), const((1, LANES)), const((tm, tm))],
        out_specs=[pl.BlockSpec((None, tm * (d // 2 // LANES), LANES), lambda bb, t: (bb, t, 0)),
                   pl.BlockSpec((None, tm, LANES), lambda bb, t: (bb, t, 0)),
                   const((1, LANES))],
        out_shape=[jax.ShapeDtypeStruct((b, l * (d // 2 // LANES), LANES), U32),
                   jax.ShapeDtypeStruct((b, l, LANES), F32),
                   jax.ShapeDtypeStruct((1, LANES), F32)],
        scratch_shapes=[pltpu.VMEM((1, LANES), F32)],
        compiler_params=_cparams(("arbitrary", "arbitrary")),
        name="moe_route",
    )(x, mod3, mod3, rw_pad, rb_pad, counts_in, tri)


def _dest_kernel(meta_ref, ps_ref, o_ref):
    meta = meta_ref[...]
    lane = lax.broadcasted_iota(I32, meta.shape, 1)
    ps = ps_ref[...]
    dst = jnp.zeros(meta.shape, F32)
    for k in range(TOP_K):
        idx = meta[:, k:k + 1].astype(I32)
        base = jnp.sum(jnp.where(lane == idx, ps, 0.0), axis=-1, keepdims=True)
        dst = jnp.where(lane == k, base + meta[:, 2 * TOP_K + k:2 * TOP_K + k + 1], dst)
    dt = dst.T
    o_ref[0] = jnp.concatenate([dt[k:k + 1] for k in range(TOP_K)], axis=1).astype(I32)


def _dest(meta, pad_start_row):
    b, l, _ = meta.shape
    tt = min(MOE_TOKEN_TILE, l)
    nt = l // tt
    return pl.pallas_call(
        _dest_kernel,
        grid=(b, nt),
        in_specs=[pl.BlockSpec((None, tt, LANES), lambda bb, t: (bb, t, 0)),
                  pl.BlockSpec((1, LANES), lambda bb, t: (0, 0))],
        out_specs=pl.BlockSpec((1, 1, TOP_K * tt), lambda bb, t: (bb * nt + t, 0, 0)),
        out_shape=jax.ShapeDtypeStruct((b * nt, 1, TOP_K * tt), I32),
        compiler_params=_cparams(("arbitrary", "arbitrary")),
        name="moe_dest",
    )(meta, pad_start_row)


def _dispatch_kernel(dest_hbm, hp_ref, xs_in, xs_out, dest_smem, sem_i, sem, *, tt, pieces):
    del xs_in
    i = pl.program_id(0)
    cp = pltpu.make_async_copy(dest_hbm.at[i, 0], dest_smem, sem_i)
    cp.start()
    cp.wait()

    def row_copy(r, k):
        src = pl.multiple_of(r * pieces, pieces)
        dst = pl.multiple_of(dest_smem[k * tt + r] * pieces, pieces)
        return pltpu.make_async_copy(hp_ref.at[pl.ds(src, pieces)], xs_out.at[pl.ds(dst, pieces)], sem)

    def issue(r, c):
        for k in range(TOP_K):
            row_copy(r, k).start(priority=k % 2)
        return c

    lax.fori_loop(0, tt, issue, 0, unroll=4)

    def drain(r, c):
        for k in range(TOP_K):
            row_copy(r, k).wait()
        return c

    lax.fori_loop(0, tt, drain, 0, unroll=4)


def _dispatch(dest, hp, xs, pieces):
    n = hp.shape[0] // pieces
    tt = dest.shape[-1] // TOP_K
    return pl.pallas_call(
        functools.partial(_dispatch_kernel, tt=tt, pieces=pieces),
        grid=(n // tt,),
        in_specs=[pl.BlockSpec(memory_space=pl.ANY),
                  pl.BlockSpec((tt * pieces, LANES), lambda i: (i, 0)),
                  pl.BlockSpec(memory_space=pl.ANY)],
        out_specs=pl.BlockSpec(memory_space=pl.ANY),
        out_shape=jax.ShapeDtypeStruct(xs.shape, xs.dtype),
        scratch_shapes=[pltpu.SMEM((tt * TOP_K,), I32), pltpu.SemaphoreType.DMA, pltpu.SemaphoreType.DMA],
        input_output_aliases={2: 0},
        compiler_params=_cparams(("arbitrary",)),
        name="moe_dispatch",
    )(dest, hp, xs)


def _expert_kernel(be_ref, nact_ref, xs_ref, wgu_ref, bgu_ref, wd_ref, bd_ref, y_ref, wgu_bf, wd_bf):
    i = pl.program_id(0)
    e = be_ref[i]
    prev = be_ref[jnp.maximum(i - 1, 0)]

    @pl.when((i == 0) | (e != prev))
    def _():
        wgu_bf[...] = wgu_ref[0].astype(BF16)
        wd_bf[...] = wd_ref[0].astype(BF16)

    rows = MOE_ROWS
    in_pieces = xs_ref.shape[0] // rows
    out_pieces = y_ref.shape[0] // rows

    @pl.when(i < nact_ref[0])
    def _():
        ws = [xs_ref[pl.ds(j, rows, stride=in_pieces), :] for j in range(in_pieces)]
        lo = [pltpu.bitcast(w << 16, F32) for w in ws]
        hi = [pltpu.bitcast(w & jnp.uint32(0xFFFF0000), F32) for w in ws]
        x = jnp.concatenate(lo + hi, axis=1).astype(BF16)
        gu = _dot(x, wgu_bf[...]) + bgu_ref[0]
        de = gu.shape[1] // 2
        gate = jnp.minimum(gu[:, :de], SWIGLU_LIMIT)
        up = jnp.clip(gu[:, de:], -SWIGLU_LIMIT, SWIGLU_LIMIT)
        act = (up + 1.0) * gate * jax.nn.sigmoid(SWIGLU_ALPHA * gate)
        y = _dot(act.astype(BF16), wd_bf[...]) + bd_ref[0]
        for j in range(out_pieces):
            y_ref[pl.ds(j, rows, stride=out_pieces), :] = y[:, j * LANES:(j + 1) * LANES]

    @pl.when(i >= nact_ref[0])
    def _():
        y_ref[...] = jnp.zeros(y_ref.shape, y_ref.dtype)


def _experts(block_e, n_active, xs, w_gu, b_gu, w_d, b_d, layer):
    _, ne, d, de2 = w_gu.shape
    in_pieces = d // 2 // LANES
    out_pieces = d // LANES
    n_rows = xs.shape[0] // in_pieces
    nblk = n_rows // MOE_ROWS
    grid_spec = pltpu.PrefetchScalarGridSpec(
        num_scalar_prefetch=2,
        grid=(nblk,),
        in_specs=[pl.BlockSpec((MOE_ROWS * in_pieces, LANES), lambda i, be, na: (i, 0)),
                  pl.BlockSpec((None, 1, d, de2), lambda i, be, na: (layer, be[i], 0, 0)),
                  pl.BlockSpec((None, 1, 1, de2), lambda i, be, na: (layer, be[i], 0, 0)),
                  pl.BlockSpec((None, 1, de2 // 2, d), lambda i, be, na: (layer, be[i], 0, 0)),
                  pl.BlockSpec((None, 1, 1, d), lambda i, be, na: (layer, be[i], 0, 0))],
        out_specs=pl.BlockSpec((MOE_ROWS * out_pieces, LANES), lambda i, be, na: (i, 0)),
        scratch_shapes=[pltpu.VMEM((d, de2), BF16), pltpu.VMEM((de2 // 2, d), BF16)],
    )
    depth = w_gu.shape[0]
    return pl.pallas_call(
        _expert_kernel,
        grid_spec=grid_spec,
        out_shape=jax.ShapeDtypeStruct((n_rows * out_pieces, LANES), F32),
        compiler_params=_cparams(("arbitrary",)),
        name="moe_experts",
    )(block_e, n_active, xs, w_gu, b_gu.reshape(depth, ne, 1, de2), w_d, b_d.reshape(depth, ne, 1, d))


def _combine_kernel(*refs, tt, final):
    if final:
        dest_hbm, yb_hbm, x_ref, meta_ref, g2_ref, fw_ref, o_ref, dest_smem, buf, sem_i, sem = refs
    else:
        dest_hbm, yb_hbm, x_ref, meta_ref, g2_ref, o_ref, dest_smem, buf, sem_i, sem = refs
    i = pl.program_id(0) * pl.num_programs(1) + pl.program_id(1)
    cp = pltpu.make_async_copy(dest_hbm.at[i, 0], dest_smem, sem_i)
    cp.start()
    cp.wait()

    pieces = buf.shape[1] // tt

    def row_copy(r, k):
        src = pl.multiple_of(dest_smem[k * tt + r] * pieces, pieces)
        dst = pl.multiple_of(r * pieces, pieces)
        return pltpu.make_async_copy(yb_hbm.at[pl.ds(src, pieces)], buf.at[k, pl.ds(dst, pieces)], sem)

    def issue(r, c):
        for k in range(TOP_K):
            row_copy(r, k).start(priority=k % 2)
        return c

    lax.fori_loop(0, tt, issue, 0, unroll=4)

    def drain(r, c):
        for k in range(TOP_K):
            row_copy(r, k).wait()
        return c

    lax.fori_loop(0, tt, drain, 0, unroll=4)
    meta = meta_ref[...]
    g2 = g2_ref[0]
    cols = []
    for j in range(pieces):
        y = meta[:, TOP_K:TOP_K + 1] * buf[0, pl.ds(j, tt, stride=pieces), :]
        for k in range(1, TOP_K):
            y = y + meta[:, TOP_K + k:TOP_K + k + 1] * buf[k, pl.ds(j, tt, stride=pieces), :]
        sl = slice(j * LANES, (j + 1) * LANES)
        cols.append(x_ref[:, sl] + g2[:, sl] * y)
    xn = jnp.concatenate(cols, axis=1)
    if final:
        xn = _rms(xn) * fw_ref[...]
    o_ref[...] = xn


def _combine(dest, yb, x, meta, mod3, layer, row_fn, final_w=None):
    b, l, d = x.shape
    tt = min(MOE_TOKEN_TILE, l)
    nt = l // tt
    final = final_w is not None
    in_specs = [pl.BlockSpec(memory_space=pl.ANY), pl.BlockSpec(memory_space=pl.ANY),
                pl.BlockSpec((None, tt, d), lambda bb, t: (bb, t, 0)),
                pl.BlockSpec((None, tt, LANES), lambda bb, t: (bb, t, 0)),
                pl.BlockSpec((1, 1, d), _mod_spec(layer, row_fn, MOD_G2))]
    args = [dest, yb, x, meta, mod3]
    if final:
        in_specs.append(pl.BlockSpec((1, d), lambda bb, t: (0, 0)))
        args.append(final_w.astype(F32).reshape(1, d))
    return pl.pallas_call(
        functools.partial(_combine_kernel, tt=tt, final=final),
        grid=(b, nt),
        in_specs=in_specs,
        out_specs=pl.BlockSpec((None, tt, d), lambda bb, t: (bb, t, 0)),
        out_shape=jax.ShapeDtypeStruct((b, l, d), F32),
        scratch_shapes=[pltpu.SMEM((tt * TOP_K,), I32), pltpu.VMEM((TOP_K, tt * (d // LANES), LANES), F32),
                        pltpu.SemaphoreType.DMA, pltpu.SemaphoreType.DMA],
        compiler_params=_cparams(("arbitrary", "arbitrary")),
        name="moe_combine",
    )(*args)


def _moe(parts, mod3, layer, router_w, router_b, w_gu, b_gu, w_d, b_d, final_w=None):
    d = router_w.shape[0]
    rw_pad = jnp.zeros((d, LANES), F32).at[:, :N_EXPERTS].set(router_w.astype(F32))
    rb_pad = jnp.zeros((1, LANES), F32).at[0, :N_EXPERTS].set(router_b.astype(F32))
    counts = jnp.zeros((1, LANES), F32)
    routed = []
    for x, row_fn in parts:
        hp, meta, counts = _route(x, mod3, layer, row_fn, rw_pad, rb_pad, counts)
        routed.append((hp, meta))
    n_assign = sum(x.shape[0] * x.shape[1] for x, _ in parts) * TOP_K
    nblk = -(-(n_assign + N_EXPERTS * (MOE_ROWS - 1)) // MOE_ROWS)
    cnt = counts[0, :N_EXPERTS].astype(I32)
    padded = (cnt + MOE_ROWS - 1) // MOE_ROWS * MOE_ROWS
    pad_end = jnp.cumsum(padded)
    pad_start = pad_end - padded
    blk_row = jnp.arange(nblk, dtype=I32)[:, None] * MOE_ROWS
    block_e = jnp.minimum(jnp.sum((pad_end[None, :] <= blk_row).astype(I32), axis=1), N_EXPERTS - 1).astype(I32)
    n_active = (pad_end[-1] // MOE_ROWS).astype(I32).reshape(1)
    ps_row = jnp.zeros((1, LANES), F32).at[0, :N_EXPERTS].set(pad_start.astype(F32))
    pieces = d // 2 // LANES
    xs = jnp.zeros((nblk * MOE_ROWS * pieces, LANES), U32)
    dests = []
    for (x, _), (hp, meta) in zip(parts, routed):
        dest = _dest(meta, ps_row)
        dests.append(dest)
        xs = _dispatch(dest, hp.reshape(-1, LANES), xs, pieces)
    yb = _experts(block_e, n_active, xs, w_gu, b_gu, w_d, b_d, layer)
    outs = []
    for idx, ((x, row_fn), (hp, meta)) in enumerate(zip(parts, routed)):
        fw = final_w if idx == 0 else None
        outs.append(_combine(dests[idx], yb, x, meta, mod3, layer, row_fn, fw))
    return outs


def kernel(x, c, ctx, c_ctx, ada_w, ada_b, ab_w_in, ab_q_gain, ab_k_gain, dn_conv_w, dn_a_log, dn_dt_bias, dn_norm_w, ab_w_out, s5_lambda_re, s5_lambda_im, s5_log_step, s5_b_re, s5_b_im, s5_c_re, s5_c_im, s5_d, s5_glu_w1, s5_glu_b1, s5_glu_w2, s5_glu_b2, moe_router_w, moe_router_b, moe_w_gate_up, moe_b_gate_up, moe_w_down, moe_b_down, final_norm_w):
    b, l, d = x.shape
    lc = ctx.shape[1]
    depth = ada_w.shape[0]
    assert depth == 2 and b < MOD_ROWS
    lat_row = lambda bb: bb
    ctx_row = lambda bb: b

    rows = jnp.zeros((MOD_ROWS, d), F32).at[:b].set(c.astype(F32)).at[b].set(c_ctx.astype(F32))
    mod3 = _ada(rows, ada_w, ada_b).reshape(depth * MOD_ROWS * 6, 1, d)

    w_in = ab_w_in[0]
    aq, akv, bqk = Q_HEADS * HEAD_DIM, KV_HEADS * HEAD_DIM, DN_HEADS * DN_DK
    o_k, o_v, o_dn, o_ab, o_z = aq, aq + akv, aq + 2 * akv, aq + 2 * akv + 3 * bqk, aq + 2 * akv + 3 * bqk + 4 * DN_HEADS
    w_perm = jnp.concatenate([w_in[:, :aq], w_in[:, o_dn:o_ab], w_in[:, o_z:], w_in[:, o_k:o_v], w_in[:, o_v:o_dn],
                              w_in[:, o_ab:o_z], jnp.zeros((d, IN_COLS_PAD - w_in.shape[1]), w_in.dtype)], axis=1).astype(BF16)
    proj_l = _inproj(x, mod3, 0, lat_row, w_perm)
    proj_c = _inproj(ctx, mod3, 0, ctx_row, w_perm)

    cos, sin = _rope_tables(l)
    ones = jnp.ones((lc, KV_HEADS * HEAD_DIM), F32)
    q_l, kt_l, v_l = _qkprep(proj_l, ab_q_gain[0], ab_k_gain[0], cos, sin, min(ATTN_K_TILE, l))
    q_c, kt_c, v_c = _qkprep(proj_c, ab_q_gain[0], ab_k_gain[0], ones, jnp.zeros_like(ones), lc)
    a_l = _attention(q_l, kt_c, v_c, kt_l, v_l)
    a_c = _attention(q_c, kt_c, v_c)

    dq_c, dk_c, dkt_c, dv_c, gc_c, gct_c = _dnprep(proj_c, dn_conv_w[0], dn_a_log[0], dn_dt_bias[0])
    dq_l, dk_l, dkt_l, dv_l, gc_l, gct_l = _dnprep(proj_l, dn_conv_w[0], dn_a_log[0], dn_dt_bias[0])
    s_zero = jnp.zeros((b, 2, DN_HEADS, DN_DK, DN_DK), F32)
    of_c, ob_c, s_ctx = _dnscan(dq_c, dk_c, dkt_c, dv_c, gc_c, gct_c, s_zero)
    of_l, ob_l, _ = _dnscan(dq_l, dk_l, dkt_l, dv_l, gc_l, gct_l, s_ctx)

    w_out = ab_w_out[0].astype(BF16)
    x = _mix0_out(x, mod3, 0, lat_row, a_l, of_l, ob_l, proj_l, dn_norm_w[0], w_out)
    ctx = _mix0_out(ctx, mod3, 0, ctx_row, a_c, of_c, ob_c, proj_c, dn_norm_w[0], w_out)

    x, ctx = _moe([(x, lat_row), (ctx, ctx_row)], mod3, 0, moe_router_w[0], moe_router_b[0],
                  moe_w_gate_up, moe_b_gate_up, moe_w_down, moe_b_down)

    wb, wc, lam_t = _s5_weights(s5_lambda_re[0], s5_lambda_im[0], s5_log_step[0], s5_b_re[0], s5_b_im[0],
                                s5_c_re[0], s5_c_im[0])
    sw = lam_t.shape[-1]
    x0 = jnp.zeros((b, 2, 2, SUBLANES, sw), F32)
    _, x_ctx = _s5_scan(ctx, mod3, 1, ctx_row, wb, wc, lam_t, x0)
    y2, _ = _s5_scan(x, mod3, 1, lat_row, wb, wc, lam_t, x_ctx)
    x = _s5_glu(x, mod3, 1, lat_row, y2, s5_d[0], s5_glu_w1[0], s5_glu_b1[0], s5_glu_w2[0], s5_glu_b2[0])

    (x,) = _moe([(x, lat_row)], mod3, 1, moe_router_w[1], moe_router_b[1], moe_w_gate_up, moe_b_gate_up,
                moe_w_down, moe_b_down, final_w=final_norm_w)
    return x
```

```python
import functools
import math

import jax
import jax.numpy as jnp
from jax import lax
from jax.experimental import pallas as pl
from jax.experimental.pallas import tpu as pltpu

F32 = jnp.float32
BF16 = jnp.bfloat16
U32 = jnp.uint32
I32 = jnp.int32

EPS = 1e-6
GRID_W = 64
HEAD_DIM = 64
Q_HEADS = 8
KV_HEADS = 2
GROUP = Q_HEADS // KV_HEADS
ROPE_THETA = 10000.0
DN_HEADS = 8
DN_DK = 64
DN_CONV = 5
DN_CHUNK = 64
S5_GROUP = 16
S5_STATE = 64
N_EXPERTS = 32
TOP_K = 4
SWIGLU_LIMIT = 7.0
SWIGLU_ALPHA = 1.702

LANES = 128
SUBLANES = 8
TOKEN_TILE = 512
MOE_TOKEN_TILE = 1024
ATTN_Q_TILE = 256
ATTN_K_TILE = 2048
DN_BLOCK_CHUNKS = 4
MOE_ROWS = 896
S5_TILE = 256
S5_PITCH_PAD = 4
VMEM_LIMIT = 56 * 1024 * 1024

MOD_SH1, MOD_SC1, MOD_G1, MOD_SH2, MOD_SC2, MOD_G2 = range(6)
MOD_ROWS = 8

COL_Q = 0
COL_DNQ = 512
COL_DNK = 1024
COL_DNV = 1536
COL_Z = 2048
COL_K = 2560
COL_V = 2688
COL_AB = 2816
IN_COLS_PAD = 2944


def _cparams(sem):
    return pltpu.CompilerParams(dimension_semantics=sem, vmem_limit_bytes=VMEM_LIMIT)


def _split3(a):
    a1 = a.astype(BF16)
    r = a - a1.astype(F32)
    a2 = r.astype(BF16)
    a3 = (r - a2.astype(F32)).astype(BF16)
    return a1, a2, a3


def _dot(a, b):
    return jnp.dot(a, b, preferred_element_type=F32)


def _dot_exact_lhs(m_bf16, x):
    x1, x2, x3 = _split3(x)
    return _dot(m_bf16, x1) + _dot(m_bf16, x2) + _dot(m_bf16, x3)


def _dot_exact_rhs(x, m_bf16):
    x1, x2, x3 = _split3(x)
    return _dot(x1, m_bf16) + _dot(x2, m_bf16) + _dot(x3, m_bf16)


def _dot_hi(a, b):
    a1, a2, _ = _split3(a)
    b1, b2, _ = _split3(b)
    return _dot(a1, b1) + _dot(a1, b2) + _dot(a2, b1)


def _rms(x):
    return x * lax.rsqrt(jnp.mean(x * x, axis=-1, keepdims=True) + EPS)


def _modulated(x_ref, sh_ref, sc_ref):
    return _rms(x_ref[...]) * (1.0 + sc_ref[0]) + sh_ref[0]


def _mod_spec(layer, row_fn, which):
    d = None

    def imap(b, t):
        return ((layer * MOD_ROWS + row_fn(b)) * 6 + which, 0, 0)

    return imap


def _ada_kernel(a_ref, w_ref, b_ref, o_ref):
    a = a_ref[...]
    a = a * jax.nn.sigmoid(a)
    o_ref[0] = _dot_hi(a, w_ref[0]) + b_ref[0]


def _ada(rows, ada_w, ada_b):
    depth, d, n = ada_w.shape
    tn = 1536
    return pl.pallas_call(
        _ada_kernel,
        grid=(depth, n // tn),
        in_specs=[pl.BlockSpec((MOD_ROWS, d), lambda l, j: (0, 0)),
                  pl.BlockSpec((1, d, tn), lambda l, j: (l, 0, j)),
                  pl.BlockSpec((1, 1, tn), lambda l, j: (l, 0, j))],
        out_specs=pl.BlockSpec((1, MOD_ROWS, tn), lambda l, j: (l, 0, j)),
        out_shape=jax.ShapeDtypeStruct((depth, MOD_ROWS, n), F32),
        compiler_params=_cparams(("arbitrary", "arbitrary")),
        name="ada_mod",
    )(rows, ada_w, ada_b.reshape(depth, 1, n))


def _inproj_kernel(x_ref, sh_ref, sc_ref, w_ref, o_ref):
    h = _modulated(x_ref, sh_ref, sc_ref)
    o_ref[...] = _dot(h.astype(BF16), w_ref[...])


def _inproj(x, mod3, layer, row_fn, w_bf16):
    b, l, d = x.shape
    tm = min(TOKEN_TILE, l)
    n = w_bf16.shape[1]
    mspec = lambda which: pl.BlockSpec((1, 1, d), _mod_spec(layer, row_fn, which))
    return pl.pallas_call(
        _inproj_kernel,
        grid=(b, l // tm),
        in_specs=[pl.BlockSpec((None, tm, d), lambda bb, t: (bb, t, 0)),
                  mspec(MOD_SH1), mspec(MOD_SC1),
                  pl.BlockSpec((d, n), lambda bb, t: (0, 0))],
        out_specs=pl.BlockSpec((None, tm, n), lambda bb, t: (bb, t, 0)),
        out_shape=jax.ShapeDtypeStruct((b, l, n), F32),
        compiler_params=_cparams(("arbitrary", "arbitrary")),
        name="in_proj",
    )(x, mod3, mod3, w_bf16)


def _head_rot(x, width):
    lane = lax.broadcasted_iota(I32, x.shape, 1)
    first = (lane % HEAD_DIM) < (HEAD_DIM // 2)
    return jnp.where(first, pltpu.roll(x, width - HEAD_DIM // 2, 1), pltpu.roll(x, HEAD_DIM // 2, 1))


def _qkprep_kernel(q_ref, k_ref, v_ref, qg_ref, kg_ref, cos_ref, sin_ref, gq_ref, gk_ref, qo_ref, kt_ref, vo_ref):
    cos = cos_ref[...]
    sin = sin_ref[...]
    q = q_ref[...]
    qn = q * lax.rsqrt(_dot_exact_rhs(q * q, gq_ref[...]) + EPS) * qg_ref[...]
    cos4 = jnp.concatenate([cos] * 4, axis=1)
    sin4 = jnp.concatenate([sin] * 4, axis=1)
    qr = qn * cos4 + _head_rot(qn, Q_HEADS * HEAD_DIM) * sin4
    qo_ref[...] = (qr * (HEAD_DIM ** -0.5 * math.log2(math.e))).astype(BF16)
    k = k_ref[...]
    kn = k * lax.rsqrt(_dot_exact_rhs(k * k, gk_ref[...]) + EPS) * kg_ref[...]
    kr = kn * cos + _head_rot(kn, KV_HEADS * HEAD_DIM) * sin
    kt = kr.T
    kt_ref[0] = kt[:HEAD_DIM].astype(BF16)
    kt_ref[1] = kt[HEAD_DIM:].astype(BF16)
    v = v_ref[...]
    ones = jnp.ones((v.shape[0], HEAD_DIM), F32)
    vo_ref[0] = jnp.concatenate([v[:, :HEAD_DIM], ones], axis=1).astype(BF16)
    vo_ref[1] = jnp.concatenate([v[:, HEAD_DIM:], ones], axis=1).astype(BF16)


def _group_matrix(width, scale):
    g = jnp.arange(width) // HEAD_DIM
    return ((g[:, None] == g[None, :]).astype(F32) * scale).astype(BF16)


def _qkprep(proj, q_gain, k_gain, cos, sin, tk):
    b, l, _ = proj.shape
    qw = Q_HEADS * HEAD_DIM
    kw = KV_HEADS * HEAD_DIM
    qg = jnp.tile(q_gain.astype(F32), Q_HEADS).reshape(1, qw)
    kg = jnp.tile(k_gain.astype(F32), KV_HEADS).reshape(1, kw)
    const = lambda shape: pl.BlockSpec(shape, lambda bb, t: (0,) * len(shape))
    return pl.pallas_call(
        _qkprep_kernel,
        grid=(b, l // tk),
        in_specs=[pl.BlockSpec((None, tk, qw), lambda bb, t: (bb, t, COL_Q // qw)),
                  pl.BlockSpec((None, tk, kw), lambda bb, t: (bb, t, COL_K // kw)),
                  pl.BlockSpec((None, tk, kw), lambda bb, t: (bb, t, COL_V // kw)),
                  const((1, qw)), const((1, kw)),
                  pl.BlockSpec((tk, kw), lambda bb, t: (t, 0)),
                  pl.BlockSpec((tk, kw), lambda bb, t: (t, 0)),
                  const((qw, qw)), const((kw, kw))],
        out_specs=[pl.BlockSpec((None, tk, qw), lambda bb, t: (bb, t, 0)),
                   pl.BlockSpec((None, KV_HEADS, None, HEAD_DIM, tk), lambda bb, t: (bb, 0, t, 0, 0)),
                   pl.BlockSpec((None, KV_HEADS, tk, 2 * HEAD_DIM), lambda bb, t: (bb, 0, t, 0))],
        out_shape=[jax.ShapeDtypeStruct((b, l, qw), BF16),
                   jax.ShapeDtypeStruct((b, KV_HEADS, l // tk, HEAD_DIM, tk), BF16),
                   jax.ShapeDtypeStruct((b, KV_HEADS, l, 2 * HEAD_DIM), BF16)],
        compiler_params=_cparams(("arbitrary", "arbitrary")),
        name="qk_prep",
    )(proj, proj, proj, qg, kg, cos, sin, _group_matrix(qw, 1.0 / HEAD_DIM), _group_matrix(kw, 1.0 / HEAD_DIM))


def _rope_tables(length):
    rows = length // GRID_W
    row = jnp.broadcast_to(jnp.arange(rows, dtype=I32)[:, None], (rows, GRID_W)).reshape(-1).astype(F32)
    col = jnp.broadcast_to(jnp.arange(GRID_W, dtype=I32)[None, :], (rows, GRID_W)).reshape(-1).astype(F32)
    n_axis = HEAD_DIM // 4
    inv_freq = ROPE_THETA ** (-jnp.arange(n_axis, dtype=F32) / n_axis)
    ang = jnp.concatenate([row[:, None] * inv_freq, col[:, None] * inv_freq], axis=-1)
    c, s = jnp.cos(ang), jnp.sin(ang)
    cos_h = jnp.concatenate([c, c], axis=-1)
    sin_h = jnp.concatenate([-s, s], axis=-1)
    return jnp.tile(cos_h, (1, KV_HEADS)), jnp.tile(sin_h, (1, KV_HEADS))


def _attn_kernel(*refs, tq, n_lat_chunks, has_lat):
    if has_lat:
        q_ref, ktc_ref, vc_ref, ktl_ref, vl_ref, o_ref = refs
    else:
        q_ref, ktc_ref, vc_ref, o_ref = refs
    q = q_ref[...]
    qs = jnp.concatenate([q[:, g * HEAD_DIM:(g + 1) * HEAD_DIM] for g in range(GROUP)], axis=0)
    rows = GROUP * tq

    def step(kt, v_ones, m, acc):
        s = _dot(qs, kt)
        m_new = jnp.maximum(m, jnp.max(s, axis=-1, keepdims=True))
        p = jnp.exp2((s - m_new).astype(BF16))
        acc = jnp.exp2(m - m_new) * acc + _dot(p, v_ones)
        return m_new, acc

    m0 = jnp.full((rows, 1), -1e30, F32)
    a0 = jnp.zeros((rows, 2 * HEAD_DIM), F32)
    carry = step(ktc_ref[0], vc_ref[...], m0, a0)
    if has_lat:
        tk = ktl_ref.shape[-1]

        def body(j, c):
            off = pl.multiple_of(j * tk, tk)
            return step(ktl_ref[j], vl_ref[pl.ds(off, tk), :], *c)

        carry = lax.fori_loop(0, n_lat_chunks, body, carry, unroll=True)
    _, acc = carry
    o = acc[:, :HEAD_DIM] / acc[:, HEAD_DIM:HEAD_DIM + 1]
    o_ref[...] = jnp.concatenate([o[g * tq:(g + 1) * tq] for g in range(GROUP)], axis=1).astype(o_ref.dtype)


def _attention(q, ktc, vc, ktl=None, vl=None):
    b, lq, qw = q.shape
    tq = min(ATTN_Q_TILE, lq)
    gw = GROUP * HEAD_DIM
    lc = vc.shape[2]
    has_lat = ktl is not None
    in_specs = [pl.BlockSpec((None, tq, gw), lambda bb, h, i: (bb, i, h)),
                pl.BlockSpec((None, None, 1, HEAD_DIM, lc), lambda bb, h, i: (bb, h, 0, 0, 0)),
                pl.BlockSpec((None, None, lc, 2 * HEAD_DIM), lambda bb, h, i: (bb, h, 0, 0))]
    args = [q, ktc, vc]
    n_chunks = 0
    if has_lat:
        n_chunks, tk = ktl.shape[2], ktl.shape[4]
        ll = vl.shape[2]
        in_specs += [pl.BlockSpec((None, None, n_chunks, HEAD_DIM, tk), lambda bb, h, i: (bb, h, 0, 0, 0)),
                     pl.BlockSpec((None, None, ll, 2 * HEAD_DIM), lambda bb, h, i: (bb, h, 0, 0))]
        args += [ktl, vl]
    return pl.pallas_call(
        functools.partial(_attn_kernel, tq=tq, n_lat_chunks=n_chunks, has_lat=has_lat),
        grid=(b, KV_HEADS, lq // tq),
        in_specs=in_specs,
        out_specs=pl.BlockSpec((None, tq, gw), lambda bb, h, i: (bb, i, h)),
        out_shape=jax.ShapeDtypeStruct((b, lq, qw), BF16),
        compiler_params=_cparams(("arbitrary", "arbitrary", "arbitrary")),
        name="gqa_attention",
    )(*args)


def _dnprep_kernel(qp, qc, qn, kp, kc, kn, vp, vc, vn, ab_ref, cw_ref, alog_ref, dtb_ref, gm_ref, trip_ref, tris_ref,
                   qo_ref, ko_ref, kto_ref, vo_ref, gc_ref, gct_ref, *, tm):
    t = pl.program_id(1)
    nt = pl.num_programs(1)
    mp = (t > 0).astype(F32)
    mn = (t < nt - 1).astype(F32)
    n_ext = tm + 2 * SUBLANES
    pad = DN_CONV // 2

    def conv(p_ref, c_ref, n_ref, w):
        ext = jnp.concatenate([p_ref[...] * mp, c_ref[...], n_ref[...] * mn], axis=0)
        acc = None
        for j in range(DN_CONV):
            s = (pad - j) % n_ext
            r = ext if s == 0 else pltpu.roll(ext, s, 0)
            term = r[SUBLANES:SUBLANES + tm] * w[j:j + 1, :]
            acc = term if acc is None else acc + term
        return acc * jax.nn.sigmoid(acc)

    cw = cw_ref[...]
    hw = DN_HEADS * DN_DK
    gm = gm_ref[...]
    q = conv(qp, qc, qn, cw[:, 0:hw])
    q = q * lax.rsqrt(_dot_exact_rhs(q * q, gm) + EPS) * (DN_DK ** -0.5)
    k = conv(kp, kc, kn, cw[:, hw:2 * hw])
    k = k * lax.rsqrt(_dot_exact_rhs(k * k, gm) + EPS)
    v = conv(vp, vc, vn, cw[:, 2 * hw:3 * hw])
    kt = k.T
    for h in range(DN_HEADS):
        sl = slice(h * DN_DK, (h + 1) * DN_DK)
        qo_ref[h] = q[:, sl]
        ko_ref[h] = k[:, sl]
        vo_ref[h] = v[:, sl]
        for c in range(tm // DN_CHUNK):
            kto_ref[h, c] = kt[sl, c * DN_CHUNK:(c + 1) * DN_CHUNK]

    ab = ab_ref[...]
    lane = lax.broadcasted_iota(I32, ab.shape, 1)
    a = ab + dtb_ref[...]
    sp = jnp.maximum(a, 0.0) + jnp.log(1.0 + jnp.exp(-jnp.abs(a)))
    g = -jnp.exp(alog_ref[...]) * sp
    g = jnp.where(lane < 2 * DN_HEADS, g, 0.0)
    gpre = _dot_exact_lhs(trip_ref[...], g)
    gsuf = _dot_exact_lhs(tris_ref[...], g)
    gc = jnp.where(lane < DN_HEADS, gpre, jnp.where(lane < 2 * DN_HEADS, gsuf, jax.nn.sigmoid(ab)))
    gc_ref[...] = gc
    gt = gc.T
    for c in range(tm // DN_CHUNK):
        gct_ref[c] = gt[:4 * DN_HEADS, c * DN_CHUNK:(c + 1) * DN_CHUNK]


def _dnprep(proj, conv_w, a_log, dt_bias):
    b, l, _ = proj.shape
    tm = min(TOKEN_TILE, l)
    hw = DN_HEADS * DN_DK
    nc = l // DN_CHUNK
    r8 = tm // SUBLANES
    last8 = l // SUBLANES - 1

    def cur(col):
        return pl.BlockSpec((None, tm, hw), lambda bb, t: (bb, t, col // hw))

    def prev(col):
        return pl.BlockSpec((None, SUBLANES, hw), lambda bb, t: (bb, jnp.maximum(t * r8 - 1, 0), col // hw))

    def nxt(col):
        return pl.BlockSpec((None, SUBLANES, hw), lambda bb, t: (bb, jnp.minimum((t + 1) * r8, last8), col // hw))

    const = lambda shape: pl.BlockSpec(shape, lambda bb, t: (0,) * len(shape))
    alog = jnp.zeros((1, LANES), F32).at[0, :2 * DN_HEADS].set(a_log.astype(F32).reshape(-1))
    dtb = jnp.zeros((1, LANES), F32).at[0, :2 * DN_HEADS].set(dt_bias.astype(F32).reshape(-1))
    ch = jnp.arange(tm) // DN_CHUNK
    same = ch[:, None] == ch[None, :]
    idx = jnp.arange(tm)
    trip = (same & (idx[:, None] >= idx[None, :])).astype(BF16)
    tris = (same & (idx[:, None] <= idx[None, :])).astype(BF16)
    gmat = _group_matrix(hw, 1.0)
    in_specs = []
    args = []
    for col in (COL_DNQ, COL_DNK, COL_DNV):
        in_specs += [prev(col), cur(col), nxt(col)]
        args += [proj, proj, proj]
    in_specs += [pl.BlockSpec((None, tm, LANES), lambda bb, t: (bb, t, COL_AB // LANES)),
                 const((DN_CONV, 3 * hw)), const((1, LANES)), const((1, LANES)), const((hw, hw)),
                 const((tm, tm)), const((tm, tm))]
    args += [proj, conv_w.astype(F32), alog, dtb, gmat, trip, tris]
    head_spec = pl.BlockSpec((None, DN_HEADS, tm, DN_DK), lambda bb, t: (bb, 0, t, 0))
    head_shape = jax.ShapeDtypeStruct((b, DN_HEADS, l, DN_DK), F32)
    return pl.pallas_call(
        functools.partial(_dnprep_kernel, tm=tm),
        grid=(b, l // tm),
        in_specs=in_specs,
        out_specs=[head_spec, head_spec,
                   pl.BlockSpec((None, DN_HEADS, tm // DN_CHUNK, DN_DK, DN_CHUNK), lambda bb, t: (bb, 0, t, 0, 0)),
                   head_spec,
                   pl.BlockSpec((None, tm, LANES), lambda bb, t: (bb, t, 0)),
                   pl.BlockSpec((None, tm // DN_CHUNK, 4 * DN_HEADS, DN_CHUNK), lambda bb, t: (bb, t, 0, 0))],
        out_shape=[head_shape, head_shape,
                   jax.ShapeDtypeStruct((b, DN_HEADS, nc, DN_DK, DN_CHUNK), F32),
                   head_shape,
                   jax.ShapeDtypeStruct((b, l, LANES), F32),
                   jax.ShapeDtypeStruct((b, nc, 4 * DN_HEADS, DN_CHUNK), F32)],
        compiler_params=_cparams(("arbitrary", "arbitrary")),
        name="dn_prep",
    )(*args)


def _bmm(a, b):
    return jnp.einsum('nij,njk->nik', a.astype(BF16), b.astype(BF16), preferred_element_type=F32)


def _unit_tri_inverse(a, eye, blks):
    inner = jnp.where(blks[0], a, 0.0)
    x = eye - inner
    p = inner
    for _ in range(2):
        p = _bmm(p, p)
        x = x + _bmm(x, p)
    for level in range(1, len(blks) + 1):
        outer = jnp.where(blks[level], a, 0.0) if level < len(blks) else a
        x = x - _bmm(x, _bmm(outer - inner, x))
        inner = outer
    return x


def _dn_local(q, k, kt, v, gcol, grow, beta, g_last, incl, strict, eye, blks):
    decay = jnp.where(incl, jnp.exp(jnp.where(incl, gcol - grow, 0.0)), 0.0)
    kb = k * beta
    a = jnp.where(strict, _bmm(kb, kt) * decay, 0.0)
    x = _unit_tri_inverse(a, eye, blks)
    u = _bmm(x, v * beta)
    w = _bmm(x, kb * jnp.exp(gcol))
    qk = _bmm(q, kt) * decay
    return u, w, qk, q * jnp.exp(gcol), kt * jnp.exp(g_last - grow), jnp.exp(g_last)


def _dnscan_kernel(qf, kf, ktf, vf, gcf, gctf, qb, kb_, ktb, vb, gcb, gctb, s0_ref, of_ref, ob_ref, sout_ref, s_scr, *, cb):
    i = pl.program_id(1)
    last = pl.num_programs(1) - 1
    nh = DN_HEADS

    @pl.when(i == 0)
    def _():
        s_scr[...] = s0_ref[...]

    row = lax.broadcasted_iota(I32, (DN_CHUNK, DN_CHUNK), 0)
    col = lax.broadcasted_iota(I32, (DN_CHUNK, DN_CHUNK), 1)
    eye = (row == col).astype(F32)
    blks = tuple((row // n) == (col // n) for n in (8, 16, 32))
    dirs = ((0, qf, kf, ktf, vf, gcf, gctf, row >= col, row > col),
            (1, qb, kb_, ktb, vb, gcb, gctb, row <= col, row < col))
    local = []
    for d, q_ref, k_ref, kt_ref, v_ref, gc_ref, gct_ref, incl, strict in dirs:
        last_row = DN_CHUNK - 1 if d == 0 else 0
        qs, ks, kts, vs, gcols, grows, betas, glasts = [], [], [], [], [], [], [], []
        for c in range(cb):
            rs = slice(c * DN_CHUNK, (c + 1) * DN_CHUNK)
            gc = gc_ref[rs, :]
            gct = gct_ref[c]
            for h in range(nh):
                gl = d * nh + h
                bl = 2 * nh + d * nh + h
                qs.append(q_ref[h, rs, :])
                ks.append(k_ref[h, rs, :])
                kts.append(kt_ref[h, c])
                vs.append(v_ref[h, rs, :])
                gcols.append(gc[:, gl:gl + 1])
                grows.append(gct[gl:gl + 1, :])
                betas.append(gc[:, bl:bl + 1])
                glasts.append(gc[last_row:last_row + 1, gl:gl + 1])
        st = lambda xs: jnp.stack(xs, axis=0)
        local.append(_dn_local(st(qs), st(ks), st(kts), st(vs), st(gcols), st(grows), st(betas), st(glasts),
                               incl, strict, eye, blks))

    s_all = jnp.concatenate([s_scr[0], s_scr[1]], axis=0)
    for jj in range(cb):
        sel = (slice(jj * nh, (jj + 1) * nh), slice((cb - 1 - jj) * nh, (cb - jj) * nh))
        u, w, qk, qg, kdt, el = (jnp.concatenate([local[0][t][sel[0]], local[1][t][sel[1]]], axis=0) for t in range(6))
        v_new = u - _bmm(w, s_all)
        o = _bmm(qg, s_all) + _bmm(qk, v_new)
        s_all = s_all * el + _bmm(kdt, v_new)
        for d, o_ref in ((0, of_ref), (1, ob_ref)):
            c = jj if d == 0 else cb - 1 - jj
            for h in range(nh):
                o_ref[h, c * DN_CHUNK:(c + 1) * DN_CHUNK, :] = o[d * nh + h]
    s_scr[0] = s_all[:nh]
    s_scr[1] = s_all[nh:]

    @pl.when(i == last)
    def _():
        sout_ref[...] = s_scr[...]


def _dnscan(q, k, kt, v, gc, gct, s0):
    b, nh, l, dk = q.shape
    cb = min(DN_BLOCK_CHUNKS, l // DN_CHUNK)
    tb = cb * DN_CHUNK
    nblk = l // tb
    fwd = lambda i: i
    bwd = lambda i: nblk - 1 - i

    def specs(ix):
        head = pl.BlockSpec((None, nh, tb, dk), lambda bb, i: (bb, 0, ix(i), 0))
        return [head, head,
                pl.BlockSpec((None, nh, cb, dk, DN_CHUNK), lambda bb, i: (bb, 0, ix(i), 0, 0)),
                head,
                pl.BlockSpec((None, tb, LANES), lambda bb, i: (bb, ix(i), 0)),
                pl.BlockSpec((None, cb, 4 * DN_HEADS, DN_CHUNK), lambda bb, i: (bb, ix(i), 0, 0))]

    state_spec = pl.BlockSpec((None, 2, nh, dk, dk), lambda bb, i: (bb, 0, 0, 0, 0))
    o_shape = jax.ShapeDtypeStruct((b, nh, l, dk), F32)
    return pl.pallas_call(
        functools.partial(_dnscan_kernel, cb=cb),
        grid=(b, nblk),
        in_specs=specs(fwd) + specs(bwd) + [state_spec],
        out_specs=[pl.BlockSpec((None, nh, tb, dk), lambda bb, i: (bb, 0, fwd(i), 0)),
                   pl.BlockSpec((None, nh, tb, dk), lambda bb, i: (bb, 0, bwd(i), 0)),
                   state_spec],
        out_shape=[o_shape, o_shape, jax.ShapeDtypeStruct((b, 2, nh, dk, dk), F32)],
        scratch_shapes=[pltpu.VMEM((2, nh, dk, dk), F32)],
        compiler_params=_cparams(("arbitrary", "arbitrary")),
        name="dn_scan",
    )(q, k, kt, v, gc, gct, q, k, kt, v, gc, gct, s0)


def _mix0_out_kernel(x_ref, g1_ref, a_ref, of_ref, ob_ref, z_ref, nw_ref, w_ref, o_ref):
    o = of_ref[...] + ob_ref[...]
    on = o * lax.rsqrt(jnp.mean(o * o, axis=-1, keepdims=True) + EPS) * nw_ref[...]
    ot = jnp.concatenate([on[h] for h in range(DN_HEADS)], axis=1)
    z = z_ref[...]
    bmix = ot * (z * jax.nn.sigmoid(z))
    mix = jnp.concatenate([a_ref[...], bmix.astype(BF16)], axis=1)
    o_ref[...] = x_ref[...] + g1_ref[0] * _dot(mix, w_ref[...])


def _mix0_out(x, mod3, layer, row_fn, attn, o_f, o_b, proj, norm_w, w_out_bf16):
    b, l, d = x.shape
    tm = min(TOKEN_TILE, l)
    aw = Q_HEADS * HEAD_DIM
    zw = DN_HEADS * DN_DK
    head = pl.BlockSpec((None, DN_HEADS, tm, DN_DK), lambda bb, t: (bb, 0, t, 0))
    return pl.pallas_call(
        _mix0_out_kernel,
        grid=(b, l // tm),
        in_specs=[pl.BlockSpec((None, tm, d), lambda bb, t: (bb, t, 0)),
                  pl.BlockSpec((1, 1, d), _mod_spec(layer, row_fn, MOD_G1)),
                  pl.BlockSpec((None, tm, aw), lambda bb, t: (bb, t, 0)),
                  head, head,
                  pl.BlockSpec((None, tm, zw), lambda bb, t: (bb, t, COL_Z // zw)),
                  pl.BlockSpec((1, DN_DK), lambda bb, t: (0, 0)),
                  pl.BlockSpec((aw + zw, d), lambda bb, t: (0, 0))],
        out_specs=pl.BlockSpec((None, tm, d), lambda bb, t: (bb, t, 0)),
        out_shape=jax.ShapeDtypeStruct((b, l, d), F32),
        compiler_params=_cparams(("arbitrary", "arbitrary")),
        name="mix0_out",
    )(x, mod3, attn, o_f, o_b, proj, norm_w.astype(F32).reshape(1, DN_DK), w_out_bf16)


def _s5_kernel(*refs, nb, tm, pitch):
    x_ref = refs[0]
    sh_refs = refs[1:1 + nb]
    sc_refs = refs[1 + nb:1 + 2 * nb]
    wb_ref, wc_ref, lam_ref, x0_ref, y_ref, xfin_ref, bre, bim, st = refs[1 + 2 * nb:]
    d = pl.program_id(0)
    i = pl.program_id(1)
    nsub = SUBLANES
    nj = bre.shape[0] // nb
    half = nj * LANES
    for bb in range(nb):
        h = (_rms(x_ref[bb]) * (1.0 + sc_refs[bb][0]) + sh_refs[bb][0]).astype(BF16)
        for s in range(nsub):
            o = _dot(h[:, s * LANES:(s + 1) * LANES], wb_ref[s])
            for j in range(nj):
                bre[bb * nj + j, s * pitch:s * pitch + tm, :] = o[:, j * LANES:(j + 1) * LANES]
                bim[bb * nj + j, s * pitch:s * pitch + tm, :] = o[:, half + j * LANES:half + (j + 1) * LANES]

    @pl.when(i == 0)
    def _():
        st[...] = x0_ref[...]

    lam_re = lam_ref[0]
    lam_im = lam_ref[1]

    def body(k, carry):
        t = jnp.where(d == 0, k, tm - 1 - k)
        out = []
        for bb in range(nb):
            xr, xi = carry[2 * bb], carry[2 * bb + 1]
            br = jnp.concatenate([bre[bb * nj + j, pl.ds(t, nsub, stride=pitch), :] for j in range(nj)], axis=1)
            bi = jnp.concatenate([bim[bb * nj + j, pl.ds(t, nsub, stride=pitch), :] for j in range(nj)], axis=1)
            nr = lam_re * xr - lam_im * xi + br
            ni = lam_re * xi + lam_im * xr + bi
            for j in range(nj):
                bre[bb * nj + j, pl.ds(t, nsub, stride=pitch), :] = nr[:, j * LANES:(j + 1) * LANES]
                bim[bb * nj + j, pl.ds(t, nsub, stride=pitch), :] = ni[:, j * LANES:(j + 1) * LANES]
            out += [nr, ni]
        return tuple(out)

    init = tuple(st[bb, c] for bb in range(nb) for c in range(2))
    fin = lax.fori_loop(0, tm, body, init, unroll=8)
    for bb in range(nb):
        for c in range(2):
            st[bb, c] = fin[2 * bb + c]
            xfin_ref[bb, c] = fin[2 * bb + c]
    for bb in range(nb):
        for s in range(nsub):
            parts = [bre[bb * nj + j, s * pitch:s * pitch + tm, :] for j in range(nj)]
            parts += [bim[bb * nj + j, s * pitch:s * pitch + tm, :] for j in range(nj)]
            xs = jnp.concatenate(parts, axis=1).astype(BF16)
            y_ref[bb, :, s * LANES:(s + 1) * LANES] = _dot(xs, wc_ref[s])


def _s5_scan(x, mod3, layer, row_fn, wb, wc, lam, x0):
    b, l, d = x.shape
    tm = min(S5_TILE, l)
    nt = l // tm
    pitch = tm + S5_PITCH_PAD
    sw = wb.shape[-1] // 2
    nj = sw // LANES
    tile = lambda dd, i: jnp.where(dd == 0, i, nt - 1 - i)

    def mspec(which, bb):
        return pl.BlockSpec((1, 1, d), lambda dd, i: _mod_spec(layer, row_fn, which)(bb, i))

    state_spec = pl.BlockSpec((b, None, 2, SUBLANES, sw), lambda dd, i: (0, dd, 0, 0, 0))
    in_specs = [pl.BlockSpec((b, tm, d), lambda dd, i: (0, tile(dd, i), 0))]
    in_specs += [mspec(MOD_SH1, bb) for bb in range(b)] + [mspec(MOD_SC1, bb) for bb in range(b)]
    in_specs += [pl.BlockSpec((None, SUBLANES, LANES, 2 * sw), lambda dd, i: (dd, 0, 0, 0)),
                 pl.BlockSpec((None, SUBLANES, 2 * sw, LANES), lambda dd, i: (dd, 0, 0, 0)),
                 pl.BlockSpec((None, 2, SUBLANES, sw), lambda dd, i: (dd, 0, 0, 0)),
                 state_spec]
    return pl.pallas_call(
        functools.partial(_s5_kernel, nb=b, tm=tm, pitch=pitch),
        grid=(2, nt),
        in_specs=in_specs,
        out_specs=[pl.BlockSpec((None, b, tm, d), lambda dd, i: (dd, 0, tile(dd, i), 0)), state_spec],
        out_shape=[jax.ShapeDtypeStruct((2, b, l, d), F32),
                   jax.ShapeDtypeStruct((b, 2, 2, SUBLANES, sw), F32)],
        scratch_shapes=[pltpu.VMEM((b * nj, SUBLANES * pitch, LANES), F32),
                        pltpu.VMEM((b * nj, SUBLANES * pitch, LANES), F32),
                        pltpu.VMEM((b, 2, SUBLANES, sw), F32)],
        compiler_params=_cparams(("arbitrary", "arbitrary")),
        name="s5_scan",
    )(x, *([mod3] * (2 * b)), wb, wc, lam, x0)


def _s5_weights(lam_re, lam_im, log_step, b_re, b_im, c_re, c_im):
    lam = lax.complex(lam_re.astype(F32), lam_im.astype(F32))
    step = jnp.exp(log_step.astype(F32))[..., None]
    lam_bar = jnp.exp(lam * step)
    b_bar = ((lam_bar - 1.0) / lam)[..., None] * lax.complex(b_re.astype(F32), b_im.astype(F32))
    ng, p = lam.shape[1], lam.shape[2]
    gl = ng // SUBLANES
    eye = jnp.eye(gl, dtype=F32)

    def wb_of(part):
        t = part.reshape(2, SUBLANES, gl, p, S5_GROUP)
        w = jnp.einsum('dsgpi,gh->dsgihp', t, eye)
        return w.reshape(2, SUBLANES, gl * S5_GROUP, gl * p)

    def wc_of(part):
        t = part.reshape(2, SUBLANES, gl, S5_GROUP, p)
        w = jnp.einsum('dsgip,gh->dshpgi', t, eye)
        return w.reshape(2, SUBLANES, gl * p, gl * S5_GROUP)

    wb = jnp.concatenate([wb_of(jnp.real(b_bar)), wb_of(jnp.imag(b_bar))], axis=-1).astype(BF16)
    wc = jnp.concatenate([wc_of(c_re.astype(F32)), -wc_of(c_im.astype(F32))], axis=-2).astype(BF16)
    lam_t = jnp.stack([jnp.real(lam_bar).reshape(2, SUBLANES, gl * p),
                       jnp.imag(lam_bar).reshape(2, SUBLANES, gl * p)], axis=1)
    return wb, wc, lam_t


def _s5_glu_kernel(x_ref, sh_ref, sc_ref, g1_ref, yf_ref, yb_ref, dsk_ref, w1_ref, b1_ref, w2_ref, b2_ref, o_ref):
    x = x_ref[...]
    h = _rms(x) * (1.0 + sc_ref[0]) + sh_ref[0]
    y = dsk_ref[...] * h + yf_ref[...] + yb_ref[...]
    z = jax.nn.gelu(y).astype(BF16)
    o = (_dot(z, w1_ref[...]) + b1_ref[...]) * jax.nn.sigmoid(_dot(z, w2_ref[...]) + b2_ref[...])
    o_ref[...] = x + g1_ref[0] * o


def _s5_glu(x, mod3, layer, row_fn, y2, d_skip, w1, b1, w2, b2):
    b, l, d = x.shape
    tm = min(TOKEN_TILE, l)
    mspec = lambda which: pl.BlockSpec((1, 1, d), _mod_spec(layer, row_fn, which))
    const = lambda shape: pl.BlockSpec(shape, lambda bb, t: (0,) * len(shape))
    return pl.pallas_call(
        _s5_glu_kernel,
        grid=(b, l // tm),
        in_specs=[pl.BlockSpec((None, tm, d), lambda bb, t: (bb, t, 0)),
                  mspec(MOD_SH1), mspec(MOD_SC1), mspec(MOD_G1),
                  pl.BlockSpec((None, None, tm, d), lambda bb, t: (0, bb, t, 0)),
                  pl.BlockSpec((None, None, tm, d), lambda bb, t: (1, bb, t, 0)),
                  const((1, d)), const((d, d)), const((1, d)), const((d, d)), const((1, d))],
        out_specs=pl.BlockSpec((None, tm, d), lambda bb, t: (bb, t, 0)),
        out_shape=jax.ShapeDtypeStruct((b, l, d), F32),
        compiler_params=_cparams(("arbitrary", "arbitrary")),
        name="s5_glu",
    )(x, mod3, mod3, mod3, y2, y2, d_skip.astype(F32).reshape(1, d), w1.astype(BF16), b1.astype(F32).reshape(1, d),
      w2.astype(BF16), b2.astype(F32).reshape(1, d))


def _route_kernel(x_ref, sh_ref, sc_ref, rw_ref, rb_ref, cin_ref, tri_ref, hp_ref, meta_ref, cnt_ref, carry):
    first = (pl.program_id(0) == 0) & (pl.program_id(1) == 0)

    @pl.when(first)
    def _():
        carry[...] = cin_ref[...]

    h = _modulated(x_ref, sh_ref, sc_ref)
    half = h.shape[1] // 2
    lo = pltpu.bitcast(h[:, :half].astype(BF16).astype(F32), U32)
    hi = pltpu.bitcast(h[:, half:].astype(BF16).astype(F32), U32)
    packed = (lo >> 16) | (hi & jnp.uint32(0xFFFF0000))
    pieces = half // LANES
    for j in range(pieces):
        hp_ref[pl.ds(j, h.shape[0], stride=pieces), :] = packed[:, j * LANES:(j + 1) * LANES]

    logits = _dot_hi(h, rw_ref[...]) + rb_ref[...]
    lane = lax.broadcasted_iota(I32, logits.shape, 1)
    cur = jnp.where(lane < N_EXPERTS, logits, -jnp.inf)
    vals, idxs = [], []
    hot = jnp.zeros(logits.shape, F32)
    for _ in range(TOP_K):
        m = jnp.max(cur, axis=-1, keepdims=True)
        idx = jnp.min(jnp.where(cur == m, lane, LANES), axis=-1, keepdims=True)
        sel = lane == idx
        hot = hot + sel.astype(F32)
        cur = jnp.where(sel, -jnp.inf, cur)
        vals.append(m)
        idxs.append(idx)
    es = [jnp.exp(v - vals[0]) for v in vals]
    tot = es[0] + es[1] + es[2] + es[3]
    before = _dot(tri_ref[...], hot.astype(BF16)) + carry[...]
    meta = jnp.zeros(logits.shape, F32)
    for k in range(TOP_K):
        rank = jnp.sum(jnp.where(lane == idxs[k], before, 0.0), axis=-1, keepdims=True)
        meta = jnp.where(lane == k, idxs[k].astype(F32), meta)
        meta = jnp.where(lane == TOP_K + k, es[k] / tot, meta)
        meta = jnp.where(lane == 2 * TOP_K + k, rank, meta)
    meta_ref[...] = meta
    carry[...] = carry[...] + jnp.sum(hot, axis=0, keepdims=True)
    cnt_ref[...] = carry[...]


def _route(x, mod3, layer, row_fn, rw_pad, rb_pad, counts_in):
    b, l, d = x.shape
    tm = min(MOE_TOKEN_TILE, l)
    idx = jnp.arange(tm)
    tri = (idx[:, None] > idx[None, :]).astype(BF16)
    mspec = lambda which: pl.BlockSpec((1, 1, d), _mod_spec(layer, row_fn, which))
    const = lambda shape: pl.BlockSpec(shape, lambda bb, t: (0,) * len(shape))
    return pl.pallas_call(
        _route_kernel,
        grid=(b, l // tm),
        in_specs=[pl.BlockSpec((None, tm, d), lambda bb, t: (bb, t, 0)),
                  mspec(MOD_SH2), mspec(MOD_SC2),
                  const((d, LANES)), const((1, LANES)), const((1, LANES)), const((tm, tm))],
        out_specs=[pl.BlockSpec((None, tm * (d // 2 // LANES), LANES), lambda bb, t: (bb, t, 0)),
                   pl.BlockSpec((None, tm, LANES), lambda bb, t: (bb, t, 0)),
                   const((1, LANES))],
        out_shape=[jax.ShapeDtypeStruct((b, l * (d // 2 // LANES), LANES), U32),
                   jax.ShapeDtypeStruct((b, l, LANES), F32),
                   jax.ShapeDtypeStruct((1, LANES), F32)],
        scratch_shapes=[pltpu.VMEM((1, LANES), F32)],
        compiler_params=_cparams(("arbitrary", "arbitrary")),
        name="moe_route",
    )(x, mod3, mod3, rw_pad, rb_pad, counts_in, tri)


def _dest_kernel(meta_ref, ps_ref, o_ref):
    meta = meta_ref[...]
    lane = lax.broadcasted_iota(I32, meta.shape, 1)
    ps = ps_ref[...]
    dst = jnp.zeros(meta.shape, F32)
    for k in range(TOP_K):
        idx = meta[:, k:k + 1].astype(I32)
        base = jnp.sum(jnp.where(lane == idx, ps, 0.0), axis=-1, keepdims=True)
        dst = jnp.where(lane == k, base + meta[:, 2 * TOP_K + k:2 * TOP_K + k + 1], dst)
    dt = dst.T
    o_ref[0] = jnp.concatenate([dt[k:k + 1] for k in range(TOP_K)], axis=1).astype(I32)


def _dest(meta, pad_start_row):
    b, l, _ = meta.shape
    tt = min(MOE_TOKEN_TILE, l)
    nt = l // tt
    return pl.pallas_call(
        _dest_kernel,
        grid=(b, nt),
        in_specs=[pl.BlockSpec((None, tt, LANES), lambda bb, t: (bb, t, 0)),
                  pl.BlockSpec((1, LANES), lambda bb, t: (0, 0))],
        out_specs=pl.BlockSpec((1, 1, TOP_K * tt), lambda bb, t: (bb * nt + t, 0, 0)),
        out_shape=jax.ShapeDtypeStruct((b * nt, 1, TOP_K * tt), I32),
        compiler_params=_cparams(("arbitrary", "arbitrary")),
        name="moe_dest",
    )(meta, pad_start_row)


def _dispatch_kernel(dest_hbm, hp_ref, xs_in, xs_out, dest_smem, sem_i, sem, *, tt, pieces):
    del xs_in
    i = pl.program_id(0)
    cp = pltpu.make_async_copy(dest_hbm.at[i, 0], dest_smem, sem_i)
    cp.start()
    cp.wait()

    def row_copy(r, k):
        src = pl.multiple_of(r * pieces, pieces)
        dst = pl.multiple_of(dest_smem[k * tt + r] * pieces, pieces)
        return pltpu.make_async_copy(hp_ref.at[pl.ds(src, pieces)], xs_out.at[pl.ds(dst, pieces)], sem)

    def issue(r, c):
        for k in range(TOP_K):
            row_copy(r, k).start(priority=k % 2)
        return c

    lax.fori_loop(0, tt, issue, 0, unroll=4)

    def drain(r, c):
        for k in range(TOP_K):
            row_copy(r, k).wait()
        return c

    lax.fori_loop(0, tt, drain, 0, unroll=4)


def _dispatch(dest, hp, xs, pieces):
    n = hp.shape[0] // pieces
    tt = dest.shape[-1] // TOP_K
    return pl.pallas_call(
        functools.partial(_dispatch_kernel, tt=tt, pieces=pieces),
        grid=(n // tt,),
        in_specs=[pl.BlockSpec(memory_space=pl.ANY),
                  pl.BlockSpec((tt * pieces, LANES), lambda i: (i, 0)),
                  pl.BlockSpec(memory_space=pl.ANY)],
        out_specs=pl.BlockSpec(memory_space=pl.ANY),
        out_shape=jax.ShapeDtypeStruct(xs.shape, xs.dtype),
        scratch_shapes=[pltpu.SMEM((tt * TOP_K,), I32), pltpu.SemaphoreType.DMA, pltpu.SemaphoreType.DMA],
        input_output_aliases={2: 0},
        compiler_params=_cparams(("arbitrary",)),
        name="moe_dispatch",
    )(dest, hp, xs)


def _expert_kernel(be_ref, nact_ref, xs_ref, wgu_ref, bgu_ref, wd_ref, bd_ref, y_ref, wgu_bf, wd_bf):
    i = pl.program_id(0)
    e = be_ref[i]
    prev = be_ref[jnp.maximum(i - 1, 0)]

    @pl.when((i == 0) | (e != prev))
    def _():
        wgu_bf[...] = wgu_ref[0].astype(BF16)
        wd_bf[...] = wd_ref[0].astype(BF16)

    rows = MOE_ROWS
    in_pieces = xs_ref.shape[0] // rows
    out_pieces = y_ref.shape[0] // rows

    @pl.when(i < nact_ref[0])
    def _():
        ws = [xs_ref[pl.ds(j, rows, stride=in_pieces), :] for j in range(in_pieces)]
        lo = [pltpu.bitcast(w << 16, F32) for w in ws]
        hi = [pltpu.bitcast(w & jnp.uint32(0xFFFF0000), F32) for w in ws]
        x = jnp.concatenate(lo + hi, axis=1).astype(BF16)
        gu = _dot(x, wgu_bf[...]) + bgu_ref[0]
        de = gu.shape[1] // 2
        gate = jnp.minimum(gu[:, :de], SWIGLU_LIMIT)
        up = jnp.clip(gu[:, de:], -SWIGLU_LIMIT, SWIGLU_LIMIT)
        act = (up + 1.0) * gate * jax.nn.sigmoid(SWIGLU_ALPHA * gate)
        y = _dot(act.astype(BF16), wd_bf[...]) + bd_ref[0]
        for j in range(out_pieces):
            y_ref[pl.ds(j, rows, stride=out_pieces), :] = y[:, j * LANES:(j + 1) * LANES]

    @pl.when(i >= nact_ref[0])
    def _():
        y_ref[...] = jnp.zeros(y_ref.shape, y_ref.dtype)


def _experts(block_e, n_active, xs, w_gu, b_gu, w_d, b_d, layer):
    _, ne, d, de2 = w_gu.shape
    in_pieces = d // 2 // LANES
    out_pieces = d // LANES
    n_rows = xs.shape[0] // in_pieces
    nblk = n_rows // MOE_ROWS
    grid_spec = pltpu.PrefetchScalarGridSpec(
        num_scalar_prefetch=2,
        grid=(nblk,),
        in_specs=[pl.BlockSpec((MOE_ROWS * in_pieces, LANES), lambda i, be, na: (i, 0)),
                  pl.BlockSpec((None, 1, d, de2), lambda i, be, na: (layer, be[i], 0, 0)),
                  pl.BlockSpec((None, 1, 1, de2), lambda i, be, na: (layer, be[i], 0, 0)),
                  pl.BlockSpec((None, 1, de2 // 2, d), lambda i, be, na: (layer, be[i], 0, 0)),
                  pl.BlockSpec((None, 1, 1, d), lambda i, be, na: (layer, be[i], 0, 0))],
        out_specs=pl.BlockSpec((MOE_ROWS * out_pieces, LANES), lambda i, be, na: (i, 0)),
        scratch_shapes=[pltpu.VMEM((d, de2), BF16), pltpu.VMEM((de2 // 2, d), BF16)],
    )
    depth = w_gu.shape[0]
    return pl.pallas_call(
        _expert_kernel,
        grid_spec=grid_spec,
        out_shape=jax.ShapeDtypeStruct((n_rows * out_pieces, LANES), F32),
        compiler_params=_cparams(("arbitrary",)),
        name="moe_experts",
    )(block_e, n_active, xs, w_gu, b_gu.reshape(depth, ne, 1, de2), w_d, b_d.reshape(depth, ne, 1, d))


def _combine_kernel(*refs, tt, final):
    if final:
        dest_hbm, yb_hbm, x_ref, meta_ref, g2_ref, fw_ref, o_ref, dest_smem, buf, sem_i, sem = refs
    else:
        dest_hbm, yb_hbm, x_ref, meta_ref, g2_ref, o_ref, dest_smem, buf, sem_i, sem = refs
    i = pl.program_id(0) * pl.num_programs(1) + pl.program_id(1)
    cp = pltpu.make_async_copy(dest_hbm.at[i, 0], dest_smem, sem_i)
    cp.start()
    cp.wait()

    pieces = buf.shape[1] // tt

    def row_copy(r, k):
        src = pl.multiple_of(dest_smem[k * tt + r] * pieces, pieces)
        dst = pl.multiple_of(r * pieces, pieces)
        return pltpu.make_async_copy(yb_hbm.at[pl.ds(src, pieces)], buf.at[k, pl.ds(dst, pieces)], sem)

    def issue(r, c):
        for k in range(TOP_K):
            row_copy(r, k).start(priority=k % 2)
        return c

    lax.fori_loop(0, tt, issue, 0, unroll=4)

    def drain(r, c):
        for k in range(TOP_K):
            row_copy(r, k).wait()
        return c

    lax.fori_loop(0, tt, drain, 0, unroll=4)
    meta = meta_ref[...]
    g2 = g2_ref[0]
    cols = []
    for j in range(pieces):
        y = meta[:, TOP_K:TOP_K + 1] * buf[0, pl.ds(j, tt, stride=pieces), :]
        for k in range(1, TOP_K):
            y = y + meta[:, TOP_K + k:TOP_K + k + 1] * buf[k, pl.ds(j, tt, stride=pieces), :]
        sl = slice(j * LANES, (j + 1) * LANES)
        cols.append(x_ref[:, sl] + g2[:, sl] * y)
    xn = jnp.concatenate(cols, axis=1)
    if final:
        xn = _rms(xn) * fw_ref[...]
    o_ref[...] = xn


def _combine(dest, yb, x, meta, mod3, layer, row_fn, final_w=None):
    b, l, d = x.shape
    tt = min(MOE_TOKEN_TILE, l)
    nt = l // tt
    final = final_w is not None
    in_specs = [pl.BlockSpec(memory_space=pl.ANY), pl.BlockSpec(memory_space=pl.ANY),
                pl.BlockSpec((None, tt, d), lambda bb, t: (bb, t, 0)),
                pl.BlockSpec((None, tt, LANES), lambda bb, t: (bb, t, 0)),
                pl.BlockSpec((1, 1, d), _mod_spec(layer, row_fn, MOD_G2))]
    args = [dest, yb, x, meta, mod3]
    if final:
        in_specs.append(pl.BlockSpec((1, d), lambda bb, t: (0, 0)))
        args.append(final_w.astype(F32).reshape(1, d))
    return pl.pallas_call(
        functools.partial(_combine_kernel, tt=tt, final=final),
        grid=(b, nt),
        in_specs=in_specs,
        out_specs=pl.BlockSpec((None, tt, d), lambda bb, t: (bb, t, 0)),
        out_shape=jax.ShapeDtypeStruct((b, l, d), F32),
        scratch_shapes=[pltpu.SMEM((tt * TOP_K,), I32), pltpu.VMEM((TOP_K, tt * (d // LANES), LANES), F32),
                        pltpu.SemaphoreType.DMA, pltpu.SemaphoreType.DMA],
        compiler_params=_cparams(("arbitrary", "arbitrary")),
        name="moe_combine",
    )(*args)


def _moe(parts, mod3, layer, router_w, router_b, w_gu, b_gu, w_d, b_d, final_w=None):
    d = router_w.shape[0]
    rw_pad = jnp.zeros((d, LANES), F32).at[:, :N_EXPERTS].set(router_w.astype(F32))
    rb_pad = jnp.zeros((1, LANES), F32).at[0, :N_EXPERTS].set(router_b.astype(F32))
    counts = jnp.zeros((1, LANES), F32)
    routed = []
    for x, row_fn in parts:
        hp, meta, counts = _route(x, mod3, layer, row_fn, rw_pad, rb_pad, counts)
        routed.append((hp, meta))
    n_assign = sum(x.shape[0] * x.shape[1] for x, _ in parts) * TOP_K
    nblk = -(-(n_assign + N_EXPERTS * (MOE_ROWS - 1)) // MOE_ROWS)
    cnt = counts[0, :N_EXPERTS].astype(I32)
    padded = (cnt + MOE_ROWS - 1) // MOE_ROWS * MOE_ROWS
    pad_end = jnp.cumsum(padded)
    pad_start = pad_end - padded
    blk_row = jnp.arange(nblk, dtype=I32)[:, None] * MOE_ROWS
    block_e = jnp.minimum(jnp.sum((pad_end[None, :] <= blk_row).astype(I32), axis=1), N_EXPERTS - 1).astype(I32)
    n_active = (pad_end[-1] // MOE_ROWS).astype(I32).reshape(1)
    ps_row = jnp.zeros((1, LANES), F32).at[0, :N_EXPERTS].set(pad_start.astype(F32))
    pieces = d // 2 // LANES
    xs = jnp.zeros((nblk * MOE_ROWS * pieces, LANES), U32)
    dests = []
    for (x, _), (hp, meta) in zip(parts, routed):
        dest = _dest(meta, ps_row)
        dests.append(dest)
        xs = _dispatch(dest, hp.reshape(-1, LANES), xs, pieces)
    yb = _experts(block_e, n_active, xs, w_gu, b_gu, w_d, b_d, layer)
    outs = []
    for idx, ((x, row_fn), (hp, meta)) in enumerate(zip(parts, routed)):
        fw = final_w if idx == 0 else None
        outs.append(_combine(dests[idx], yb, x, meta, mod3, layer, row_fn, fw))
    return outs


def kernel(x, c, ctx, c_ctx, ada_w, ada_b, ab_w_in, ab_q_gain, ab_k_gain, dn_conv_w, dn_a_log, dn_dt_bias, dn_norm_w, ab_w_out, s5_lambda_re, s5_lambda_im, s5_log_step, s5_b_re, s5_b_im, s5_c_re, s5_c_im, s5_d, s5_glu_w1, s5_glu_b1, s5_glu_w2, s5_glu_b2, moe_router_w, moe_router_b, moe_w_gate_up, moe_b_gate_up, moe_w_down, moe_b_down, final_norm_w):
    b, l, d = x.shape
    lc = ctx.shape[1]
    depth = ada_w.shape[0]
    assert depth == 2 and b < MOD_ROWS
    lat_row = lambda bb: bb
    ctx_row = lambda bb: b

    rows = jnp.zeros((MOD_ROWS, d), F32).at[:b].set(c.astype(F32)).at[b].set(c_ctx.astype(F32))
    mod3 = _ada(rows, ada_w, ada_b).reshape(depth * MOD_ROWS * 6, 1, d)

    w_in = ab_w_in[0]
    aq, akv, bqk = Q_HEADS * HEAD_DIM, KV_HEADS * HEAD_DIM, DN_HEADS * DN_DK
    o_k, o_v, o_dn, o_ab, o_z = aq, aq + akv, aq + 2 * akv, aq + 2 * akv + 3 * bqk, aq + 2 * akv + 3 * bqk + 4 * DN_HEADS
    w_perm = jnp.concatenate([w_in[:, :aq], w_in[:, o_dn:o_ab], w_in[:, o_z:], w_in[:, o_k:o_v], w_in[:, o_v:o_dn],
                              w_in[:, o_ab:o_z], jnp.zeros((d, IN_COLS_PAD - w_in.shape[1]), w_in.dtype)], axis=1).astype(BF16)
    proj_l = _inproj(x, mod3, 0, lat_row, w_perm)
    proj_c = _inproj(ctx, mod3, 0, ctx_row, w_perm)

    cos, sin = _rope_tables(l)
    ones = jnp.ones((lc, KV_HEADS * HEAD_DIM), F32)
    q_l, kt_l, v_l = _qkprep(proj_l, ab_q_gain[0], ab_k_gain[0], cos, sin, min(ATTN_K_TILE, l))
    q_c, kt_c, v_c = _qkprep(proj_c, ab_q_gain[0], ab_k_gain[0], ones, jnp.zeros_like(ones), lc)
    a_l = _attention(q_l, kt_c, v_c, kt_l, v_l)
    a_c = _attention(q_c, kt_c, v_c)

    dq_c, dk_c, dkt_c, dv_c, gc_c, gct_c = _dnprep(proj_c, dn_conv_w[0], dn_a_log[0], dn_dt_bias[0])
    dq_l, dk_l, dkt_l, dv_l, gc_l, gct_l = _dnprep(proj_l, dn_conv_w[0], dn_a_log[0], dn_dt_bias[0])
    s_zero = jnp.zeros((b, 2, DN_HEADS, DN_DK, DN_DK), F32)
    of_c, ob_c, s_ctx = _dnscan(dq_c, dk_c, dkt_c, dv_c, gc_c, gct_c, s_zero)
    of_l, ob_l, _ = _dnscan(dq_l, dk_l, dkt_l, dv_l, gc_l, gct_l, s_ctx)

    w_out = ab_w_out[0].astype(BF16)
    x = _mix0_out(x, mod3, 0, lat_row, a_l, of_l, ob_l, proj_l, dn_norm_w[0], w_out)
    ctx = _mix0_out(ctx, mod3, 0, ctx_row, a_c, of_c, ob_c, proj_c, dn_norm_w[0], w_out)

    x, ctx = _moe([(x, lat_row), (ctx, ctx_row)], mod3, 0, moe_router_w[0], moe_router_b[0],
                  moe_w_gate_up, moe_b_gate_up, moe_w_down, moe_b_down)

    wb, wc, lam_t = _s5_weights(s5_lambda_re[0], s5_lambda_im[0], s5_log_step[0], s5_b_re[0], s5_b_im[0],
                                s5_c_re[0], s5_c_im[0])
    sw = lam_t.shape[-1]
    x0 = jnp.zeros((b, 2, 2, SUBLANES, sw), F32)
    _, x_ctx = _s5_scan(ctx, mod3, 1, ctx_row, wb, wc, lam_t, x0)
    y2, _ = _s5_scan(x, mod3, 1, lat_row, wb, wc, lam_t, x_ctx)
    x = _s5_glu(x, mod3, 1, lat_row, y2, s5_d[0], s5_glu_w1[0], s5_glu_b1[0], s5_glu_w2[0], s5_glu_b2[0])

    (x,) = _moe([(x, lat_row)], mod3, 1, moe_router_w[1], moe_router_b[1], moe_w_gate_up, moe_b_gate_up,
                moe_w_down, moe_b_down, final_w=final_norm_w)
    return x
```

```python
import functools
import math

import jax
import jax.numpy as jnp
from jax import lax
from jax.experimental import pallas as pl
from jax.experimental.pallas import tpu as pltpu

F32 = jnp.float32
BF16 = jnp.bfloat16
U32 = jnp.uint32
I32 = jnp.int32

EPS = 1e-6
GRID_W = 64
HEAD_DIM = 64
Q_HEADS = 8
KV_HEADS = 2
GROUP = Q_HEADS // KV_HEADS
ROPE_THETA = 10000.0
DN_HEADS = 8
DN_DK = 64
DN_CONV = 5
DN_CHUNK = 64
S5_GROUP = 16
S5_STATE = 64
N_EXPERTS = 32
TOP_K = 4
SWIGLU_LIMIT = 7.0
SWIGLU_ALPHA = 1.702

LANES = 128
SUBLANES = 8
TOKEN_TILE = 1024
DN_PREP_TILE = 512
MOE_TOKEN_TILE = 1024
ATTN_Q_TILE = 256
ATTN_K_TILE = 2048
DN_BLOCK_CHUNKS = 4
MOE_ROWS = 896
S5_TILE = 256
S5_PITCH_PAD = 4
VMEM_LIMIT = 56 * 1024 * 1024

MOD_SH1, MOD_SC1, MOD_G1, MOD_SH2, MOD_SC2, MOD_G2 = range(6)
MOD_ROWS = 8

COL_Q = 0
COL_DNQ = 512
COL_DNK = 1024
COL_DNV = 1536
COL_Z = 2048
COL_K = 2560
COL_V = 2688
COL_AB = 2816
IN_COLS_PAD = 2944


def _cparams(sem):
    return pltpu.CompilerParams(dimension_semantics=sem, vmem_limit_bytes=VMEM_LIMIT)


def _split3(a):
    a1 = a.astype(BF16)
    r = a - a1.astype(F32)
    a2 = r.astype(BF16)
    a3 = (r - a2.astype(F32)).astype(BF16)
    return a1, a2, a3


def _dot(a, b):
    return jnp.dot(a, b, preferred_element_type=F32)


def _dot_exact_lhs(m_bf16, x):
    x1, x2, x3 = _split3(x)
    return _dot(m_bf16, x1) + _dot(m_bf16, x2) + _dot(m_bf16, x3)


def _dot_exact_rhs(x, m_bf16):
    x1, x2, x3 = _split3(x)
    return _dot(x1, m_bf16) + _dot(x2, m_bf16) + _dot(x3, m_bf16)


def _dot_hi(a, b):
    a1, a2, _ = _split3(a)
    b1, b2, _ = _split3(b)
    return _dot(a1, b1) + _dot(a1, b2) + _dot(a2, b1)


def _rms(x):
    return x * lax.rsqrt(jnp.mean(x * x, axis=-1, keepdims=True) + EPS)


def _modulated(x_ref, sh_ref, sc_ref):
    return _rms(x_ref[...]) * (1.0 + sc_ref[0]) + sh_ref[0]


def _mod_spec(layer, row_fn, which):
    d = None

    def imap(b, t):
        return ((layer * MOD_ROWS + row_fn(b)) * 6 + which, 0, 0)

    return imap


def _ada_kernel(a_ref, w_ref, b_ref, o_ref):
    a = a_ref[...]
    a = a * jax.nn.sigmoid(a)
    o_ref[0] = _dot_hi(a, w_ref[0]) + b_ref[0]


def _ada(rows, ada_w, ada_b):
    depth, d, n = ada_w.shape
    tn = 1536
    return pl.pallas_call(
        _ada_kernel,
        grid=(depth, n // tn),
        in_specs=[pl.BlockSpec((MOD_ROWS, d), lambda l, j: (0, 0)),
                  pl.BlockSpec((1, d, tn), lambda l, j: (l, 0, j)),
                  pl.BlockSpec((1, 1, tn), lambda l, j: (l, 0, j))],
        out_specs=pl.BlockSpec((1, MOD_ROWS, tn), lambda l, j: (l, 0, j)),
        out_shape=jax.ShapeDtypeStruct((depth, MOD_ROWS, n), F32),
        compiler_params=_cparams(("arbitrary", "arbitrary")),
        name="ada_mod",
    )(rows, ada_w, ada_b.reshape(depth, 1, n))


def _inproj_kernel(x_ref, sh_ref, sc_ref, w_ref, o_ref):
    h = _modulated(x_ref, sh_ref, sc_ref)
    o_ref[...] = _dot(h.astype(BF16), w_ref[...])


def _inproj(x, mod3, layer, row_fn, w_bf16):
    b, l, d = x.shape
    tm = min(TOKEN_TILE, l)
    n = w_bf16.shape[1]
    mspec = lambda which: pl.BlockSpec((1, 1, d), _mod_spec(layer, row_fn, which))
    return pl.pallas_call(
        _inproj_kernel,
        grid=(b, l // tm),
        in_specs=[pl.BlockSpec((None, tm, d), lambda bb, t: (bb, t, 0)),
                  mspec(MOD_SH1), mspec(MOD_SC1),
                  pl.BlockSpec((d, n), lambda bb, t: (0, 0))],
        out_specs=pl.BlockSpec((None, tm, n), lambda bb, t: (bb, t, 0)),
        out_shape=jax.ShapeDtypeStruct((b, l, n), F32),
        compiler_params=_cparams(("arbitrary", "arbitrary")),
        name="in_proj",
    )(x, mod3, mod3, w_bf16)


def _head_rot(x, width):
    lane = lax.broadcasted_iota(I32, x.shape, 1)
    first = (lane % HEAD_DIM) < (HEAD_DIM // 2)
    return jnp.where(first, pltpu.roll(x, width - HEAD_DIM // 2, 1), pltpu.roll(x, HEAD_DIM // 2, 1))


def _qkprep_kernel(q_ref, k_ref, v_ref, qg_ref, kg_ref, cos_ref, sin_ref, gq_ref, gk_ref, qo_ref, kt_ref, vo_ref):
    cos = cos_ref[...]
    sin = sin_ref[...]
    q = q_ref[...]
    qn = q * lax.rsqrt(_dot_exact_rhs(q * q, gq_ref[...]) + EPS) * qg_ref[...]
    cos4 = jnp.concatenate([cos] * 4, axis=1)
    sin4 = jnp.concatenate([sin] * 4, axis=1)
    qr = qn * cos4 + _head_rot(qn, Q_HEADS * HEAD_DIM) * sin4
    qo_ref[...] = (qr * (HEAD_DIM ** -0.5 * math.log2(math.e))).astype(BF16)
    k = k_ref[...]
    kn = k * lax.rsqrt(_dot_exact_rhs(k * k, gk_ref[...]) + EPS) * kg_ref[...]
    kr = kn * cos + _head_rot(kn, KV_HEADS * HEAD_DIM) * sin
    kt = kr.T
    kt_ref[0] = kt[:HEAD_DIM].astype(BF16)
    kt_ref[1] = kt[HEAD_DIM:].astype(BF16)
    v = v_ref[...]
    ones = jnp.ones((v.shape[0], HEAD_DIM), F32)
    vo_ref[0] = jnp.concatenate([v[:, :HEAD_DIM], ones], axis=1).astype(BF16)
    vo_ref[1] = jnp.concatenate([v[:, HEAD_DIM:], ones], axis=1).astype(BF16)


def _group_matrix(width, scale):
    g = jnp.arange(width) // HEAD_DIM
    return ((g[:, None] == g[None, :]).astype(F32) * scale).astype(BF16)


def _qkprep(proj, q_gain, k_gain, cos, sin, tk):
    b, l, _ = proj.shape
    qw = Q_HEADS * HEAD_DIM
    kw = KV_HEADS * HEAD_DIM
    qg = jnp.tile(q_gain.astype(F32), Q_HEADS).reshape(1, qw)
    kg = jnp.tile(k_gain.astype(F32), KV_HEADS).reshape(1, kw)
    const = lambda shape: pl.BlockSpec(shape, lambda bb, t: (0,) * len(shape))
    return pl.pallas_call(
        _qkprep_kernel,
        grid=(b, l // tk),
        in_specs=[pl.BlockSpec((None, tk, qw), lambda bb, t: (bb, t, COL_Q // qw)),
                  pl.BlockSpec((None, tk, kw), lambda bb, t: (bb, t, COL_K // kw)),
                  pl.BlockSpec((None, tk, kw), lambda bb, t: (bb, t, COL_V // kw)),
                  const((1, qw)), const((1, kw)),
                  pl.BlockSpec((tk, kw), lambda bb, t: (t, 0)),
                  pl.BlockSpec((tk, kw), lambda bb, t: (t, 0)),
                  const((qw, qw)), const((kw, kw))],
        out_specs=[pl.BlockSpec((None, tk, qw), lambda bb, t: (bb, t, 0)),
                   pl.BlockSpec((None, KV_HEADS, None, HEAD_DIM, tk), lambda bb, t: (bb, 0, t, 0, 0)),
                   pl.BlockSpec((None, KV_HEADS, tk, 2 * HEAD_DIM), lambda bb, t: (bb, 0, t, 0))],
        out_shape=[jax.ShapeDtypeStruct((b, l, qw), BF16),
                   jax.ShapeDtypeStruct((b, KV_HEADS, l // tk, HEAD_DIM, tk), BF16),
                   jax.ShapeDtypeStruct((b, KV_HEADS, l, 2 * HEAD_DIM), BF16)],
        compiler_params=_cparams(("arbitrary", "arbitrary")),
        name="qk_prep",
    )(proj, proj, proj, qg, kg, cos, sin, _group_matrix(qw, 1.0 / HEAD_DIM), _group_matrix(kw, 1.0 / HEAD_DIM))


def _rope_tables(length):
    rows = length // GRID_W
    row = jnp.broadcast_to(jnp.arange(rows, dtype=I32)[:, None], (rows, GRID_W)).reshape(-1).astype(F32)
    col = jnp.broadcast_to(jnp.arange(GRID_W, dtype=I32)[None, :], (rows, GRID_W)).reshape(-1).astype(F32)
    n_axis = HEAD_DIM // 4
    inv_freq = ROPE_THETA ** (-jnp.arange(n_axis, dtype=F32) / n_axis)
    ang = jnp.concatenate([row[:, None] * inv_freq, col[:, None] * inv_freq], axis=-1)
    c, s = jnp.cos(ang), jnp.sin(ang)
    cos_h = jnp.concatenate([c, c], axis=-1)
    sin_h = jnp.concatenate([-s, s], axis=-1)
    return jnp.tile(cos_h, (1, KV_HEADS)), jnp.tile(sin_h, (1, KV_HEADS))


def _attn_kernel(*refs, tq, n_lat_chunks, has_lat):
    if has_lat:
        q_ref, ktc_ref, vc_ref, ktl_ref, vl_ref, o_ref = refs
    else:
        q_ref, ktc_ref, vc_ref, o_ref = refs
    q = q_ref[...]
    qs = jnp.concatenate([q[:, g * HEAD_DIM:(g + 1) * HEAD_DIM] for g in range(GROUP)], axis=0)
    rows = GROUP * tq

    def step(kt, v_ones, m, acc):
        s = _dot(qs, kt)
        m_new = jnp.maximum(m, jnp.max(s, axis=-1, keepdims=True))
        p = jnp.exp2((s - m_new).astype(BF16))
        acc = jnp.exp2(m - m_new) * acc + _dot(p, v_ones)
        return m_new, acc

    m0 = jnp.full((rows, 1), -1e30, F32)
    a0 = jnp.zeros((rows, 2 * HEAD_DIM), F32)
    carry = step(ktc_ref[0], vc_ref[...], m0, a0)
    if has_lat:
        tk = ktl_ref.shape[-1]

        def body(j, c):
            off = pl.multiple_of(j * tk, tk)
            return step(ktl_ref[j], vl_ref[pl.ds(off, tk), :], *c)

        carry = lax.fori_loop(0, n_lat_chunks, body, carry, unroll=True)
    _, acc = carry
    o = acc[:, :HEAD_DIM] / acc[:, HEAD_DIM:HEAD_DIM + 1]
    o_ref[...] = jnp.concatenate([o[g * tq:(g + 1) * tq] for g in range(GROUP)], axis=1).astype(o_ref.dtype)


def _attention(q, ktc, vc, ktl=None, vl=None):
    b, lq, qw = q.shape
    tq = min(ATTN_Q_TILE, lq)
    gw = GROUP * HEAD_DIM
    lc = vc.shape[2]
    has_lat = ktl is not None
    in_specs = [pl.BlockSpec((None, tq, gw), lambda bb, h, i: (bb, i, h)),
                pl.BlockSpec((None, None, 1, HEAD_DIM, lc), lambda bb, h, i: (bb, h, 0, 0, 0)),
                pl.BlockSpec((None, None, lc, 2 * HEAD_DIM), lambda bb, h, i: (bb, h, 0, 0))]
    args = [q, ktc, vc]
    n_chunks = 0
    if has_lat:
        n_chunks, tk = ktl.shape[2], ktl.shape[4]
        ll = vl.shape[2]
        in_specs += [pl.BlockSpec((None, None, n_chunks, HEAD_DIM, tk), lambda bb, h, i: (bb, h, 0, 0, 0)),
                     pl.BlockSpec((None, None, ll, 2 * HEAD_DIM), lambda bb, h, i: (bb, h, 0, 0))]
        args += [ktl, vl]
    return pl.pallas_call(
        functools.partial(_attn_kernel, tq=tq, n_lat_chunks=n_chunks, has_lat=has_lat),
        grid=(b, KV_HEADS, lq // tq),
        in_specs=in_specs,
        out_specs=pl.BlockSpec((None, tq, gw), lambda bb, h, i: (bb, i, h)),
        out_shape=jax.ShapeDtypeStruct((b, lq, qw), BF16),
        compiler_params=_cparams(("arbitrary", "arbitrary", "arbitrary")),
        name="gqa_attention",
    )(*args)


def _dnprep_kernel(qp, qc, qn, kp, kc, kn, vp, vc, vn, ab_ref, cw_ref, alog_ref, dtb_ref, gm_ref, trip_ref, tris_ref,
                   qo_ref, ko_ref, kto_ref, vo_ref, gc_ref, gct_ref, *, tm):
    t = pl.program_id(1)
    nt = pl.num_programs(1)
    mp = (t > 0).astype(F32)
    mn = (t < nt - 1).astype(F32)
    n_ext = tm + 2 * SUBLANES
    pad = DN_CONV // 2

    def conv(p_ref, c_ref, n_ref, w):
        ext = jnp.concatenate([p_ref[...] * mp, c_ref[...], n_ref[...] * mn], axis=0)
        acc = None
        for j in range(DN_CONV):
            s = (pad - j) % n_ext
            r = ext if s == 0 else pltpu.roll(ext, s, 0)
            term = r[SUBLANES:SUBLANES + tm] * w[j:j + 1, :]
            acc = term if acc is None else acc + term
        return acc * jax.nn.sigmoid(acc)

    cw = cw_ref[...]
    hw = DN_HEADS * DN_DK
    gm = gm_ref[...]
    q = conv(qp, qc, qn, cw[:, 0:hw])
    q = q * lax.rsqrt(_dot_exact_rhs(q * q, gm) + EPS) * (DN_DK ** -0.5)
    k = conv(kp, kc, kn, cw[:, hw:2 * hw])
    k = k * lax.rsqrt(_dot_exact_rhs(k * k, gm) + EPS)
    v = conv(vp, vc, vn, cw[:, 2 * hw:3 * hw])
    kt = k.T
    for h in range(DN_HEADS):
        sl = slice(h * DN_DK, (h + 1) * DN_DK)
        qo_ref[h] = q[:, sl]
        ko_ref[h] = k[:, sl]
        vo_ref[h] = v[:, sl]
        for c in range(tm // DN_CHUNK):
            kto_ref[h, c] = kt[sl, c * DN_CHUNK:(c + 1) * DN_CHUNK]

    ab = ab_ref[...]
    lane = lax.broadcasted_iota(I32, ab.shape, 1)
    a = ab + dtb_ref[...]
    sp = jnp.maximum(a, 0.0) + jnp.log(1.0 + jnp.exp(-jnp.abs(a)))
    g = -jnp.exp(alog_ref[...]) * sp
    g = jnp.where(lane < 2 * DN_HEADS, g, 0.0)
    gpre = _dot_exact_lhs(trip_ref[...], g)
    gsuf = _dot_exact_lhs(tris_ref[...], g)
    gc = jnp.where(lane < DN_HEADS, gpre, jnp.where(lane < 2 * DN_HEADS, gsuf, jax.nn.sigmoid(ab)))
    gc_ref[...] = gc
    gt = gc.T
    for c in range(tm // DN_CHUNK):
        gct_ref[c] = gt[:4 * DN_HEADS, c * DN_CHUNK:(c + 1) * DN_CHUNK]


def _dnprep(proj, conv_w, a_log, dt_bias):
    b, l, _ = proj.shape
    tm = min(DN_PREP_TILE, l)
    hw = DN_HEADS * DN_DK
    nc = l // DN_CHUNK
    r8 = tm // SUBLANES
    last8 = l // SUBLANES - 1

    def cur(col):
        return pl.BlockSpec((None, tm, hw), lambda bb, t: (bb, t, col // hw))

    def prev(col):
        return pl.BlockSpec((None, SUBLANES, hw), lambda bb, t: (bb, jnp.maximum(t * r8 - 1, 0), col // hw))

    def nxt(col):
        return pl.BlockSpec((None, SUBLANES, hw), lambda bb, t: (bb, jnp.minimum((t + 1) * r8, last8), col // hw))

    const = lambda shape: pl.BlockSpec(shape, lambda bb, t: (0,) * len(shape))
    alog = jnp.zeros((1, LANES), F32).at[0, :2 * DN_HEADS].set(a_log.astype(F32).reshape(-1))
    dtb = jnp.zeros((1, LANES), F32).at[0, :2 * DN_HEADS].set(dt_bias.astype(F32).reshape(-1))
    ch = jnp.arange(tm) // DN_CHUNK
    same = ch[:, None] == ch[None, :]
    idx = jnp.arange(tm)
    trip = (same & (idx[:, None] >= idx[None, :])).astype(BF16)
    tris = (same & (idx[:, None] <= idx[None, :])).astype(BF16)
    gmat = _group_matrix(hw, 1.0)
    in_specs = []
    args = []
    for col in (COL_DNQ, COL_DNK, COL_DNV):
        in_specs += [prev(col), cur(col), nxt(col)]
        args += [proj, proj, proj]
    in_specs += [pl.BlockSpec((None, tm, LANES), lambda bb, t: (bb, t, COL_AB // LANES)),
                 const((DN_CONV, 3 * hw)), const((1, LANES)), const((1, LANES)), const((hw, hw)),
                 const((tm, tm)), const((tm, tm))]
    args += [proj, conv_w.astype(F32), alog, dtb, gmat, trip, tris]
    head_spec = pl.BlockSpec((None, DN_HEADS, tm, DN_DK), lambda bb, t: (bb, 0, t, 0))
    head_shape = jax.ShapeDtypeStruct((b, DN_HEADS, l, DN_DK), F32)
    return pl.pallas_call(
        functools.partial(_dnprep_kernel, tm=tm),
        grid=(b, l // tm),
        in_specs=in_specs,
        out_specs=[head_spec, head_spec,
                   pl.BlockSpec((None, DN_HEADS, tm // DN_CHUNK, DN_DK, DN_CHUNK), lambda bb, t: (bb, 0, t, 0, 0)),
                   head_spec,
                   pl.BlockSpec((None, tm, LANES), lambda bb, t: (bb, t, 0)),
                   pl.BlockSpec((None, tm // DN_CHUNK, 4 * DN_HEADS, DN_CHUNK), lambda bb, t: (bb, t, 0, 0))],
        out_shape=[head_shape, head_shape,
                   jax.ShapeDtypeStruct((b, DN_HEADS, nc, DN_DK, DN_CHUNK), F32),
                   head_shape,
                   jax.ShapeDtypeStruct((b, l, LANES), F32),
                   jax.ShapeDtypeStruct((b, nc, 4 * DN_HEADS, DN_CHUNK), F32)],
        compiler_params=_cparams(("arbitrary", "arbitrary")),
        name="dn_prep",
    )(*args)


def _bmm(a, b):
    return jnp.einsum('nij,njk->nik', a.astype(BF16), b.astype(BF16), preferred_element_type=F32)


def _unit_tri_inverse(a, eye, blks):
    inner = jnp.where(blks[0], a, 0.0)
    x = eye - inner
    p = inner
    for _ in range(2):
        p = _bmm(p, p)
        x = x + _bmm(x, p)
    for level in range(1, len(blks) + 1):
        outer = jnp.where(blks[level], a, 0.0) if level < len(blks) else a
        x = x - _bmm(x, _bmm(outer - inner, x))
        inner = outer
    return x


def _dn_local(q, k, kt, v, gcol, grow, beta, g_last, incl, strict, eye, blks):
    decay = jnp.where(incl, jnp.exp(jnp.where(incl, gcol - grow, 0.0)), 0.0)
    kb = k * beta
    a = jnp.where(strict, _bmm(kb, kt) * decay, 0.0)
    x = _unit_tri_inverse(a, eye, blks)
    u = _bmm(x, v * beta)
    w = _bmm(x, kb * jnp.exp(gcol))
    qk = _bmm(q, kt) * decay
    return u, w, qk, q * jnp.exp(gcol), kt * jnp.exp(g_last - grow), jnp.exp(g_last)


def _dnscan_kernel(qf, kf, ktf, vf, gcf, gctf, qb, kb_, ktb, vb, gcb, gctb, s0_ref, of_ref, ob_ref, sout_ref, s_scr, *, cb):
    i = pl.program_id(1)
    last = pl.num_programs(1) - 1
    nh = DN_HEADS

    @pl.when(i == 0)
    def _():
        s_scr[...] = s0_ref[...]

    row = lax.broadcasted_iota(I32, (DN_CHUNK, DN_CHUNK), 0)
    col = lax.broadcasted_iota(I32, (DN_CHUNK, DN_CHUNK), 1)
    eye = (row == col).astype(F32)
    blks = tuple((row // n) == (col // n) for n in (8, 16, 32))
    dirs = ((0, qf, kf, ktf, vf, gcf, gctf, row >= col, row > col),
            (1, qb, kb_, ktb, vb, gcb, gctb, row <= col, row < col))
    local = []
    for d, q_ref, k_ref, kt_ref, v_ref, gc_ref, gct_ref, incl, strict in dirs:
        last_row = DN_CHUNK - 1 if d == 0 else 0
        qs, ks, kts, vs, gcols, grows, betas, glasts = [], [], [], [], [], [], [], []
        for c in range(cb):
            rs = slice(c * DN_CHUNK, (c + 1) * DN_CHUNK)
            gc = gc_ref[rs, :]
            gct = gct_ref[c]
            for h in range(nh):
                gl = d * nh + h
                bl = 2 * nh + d * nh + h
                qs.append(q_ref[h, rs, :])
                ks.append(k_ref[h, rs, :])
                kts.append(kt_ref[h, c])
                vs.append(v_ref[h, rs, :])
                gcols.append(gc[:, gl:gl + 1])
                grows.append(gct[gl:gl + 1, :])
                betas.append(gc[:, bl:bl + 1])
                glasts.append(gc[last_row:last_row + 1, gl:gl + 1])
        st = lambda xs: jnp.stack(xs, axis=0)
        local.append(_dn_local(st(qs), st(ks), st(kts), st(vs), st(gcols), st(grows), st(betas), st(glasts),
                               incl, strict, eye, blks))

    s_all = jnp.concatenate([s_scr[0], s_scr[1]], axis=0)
    for jj in range(cb):
        sel = (slice(jj * nh, (jj + 1) * nh), slice((cb - 1 - jj) * nh, (cb - jj) * nh))
        u, w, qk, qg, kdt, el = (jnp.concatenate([local[0][t][sel[0]], local[1][t][sel[1]]], axis=0) for t in range(6))
        v_new = u - _bmm(w, s_all)
        o = _bmm(qg, s_all) + _bmm(qk, v_new)
        s_all = s_all * el + _bmm(kdt, v_new)
        for d, o_ref in ((0, of_ref), (1, ob_ref)):
            c = jj if d == 0 else cb - 1 - jj
            for h in range(nh):
                o_ref[h, c * DN_CHUNK:(c + 1) * DN_CHUNK, :] = o[d * nh + h]
    s_scr[0] = s_all[:nh]
    s_scr[1] = s_all[nh:]

    @pl.when(i == last)
    def _():
        sout_ref[...] = s_scr[...]


def _dnscan(q, k, kt, v, gc, gct, s0):
    b, nh, l, dk = q.shape
    cb = min(DN_BLOCK_CHUNKS, l // DN_CHUNK)
    tb = cb * DN_CHUNK
    nblk = l // tb
    fwd = lambda i: i
    bwd = lambda i: nblk - 1 - i

    def specs(ix):
        head = pl.BlockSpec((None, nh, tb, dk), lambda bb, i: (bb, 0, ix(i), 0))
        return [head, head,
                pl.BlockSpec((None, nh, cb, dk, DN_CHUNK), lambda bb, i: (bb, 0, ix(i), 0, 0)),
                head,
                pl.BlockSpec((None, tb, LANES), lambda bb, i: (bb, ix(i), 0)),
                pl.BlockSpec((None, cb, 4 * DN_HEADS, DN_CHUNK), lambda bb, i: (bb, ix(i), 0, 0))]

    state_spec = pl.BlockSpec((None, 2, nh, dk, dk), lambda bb, i: (bb, 0, 0, 0, 0))
    o_shape = jax.ShapeDtypeStruct((b, nh, l, dk), F32)
    return pl.pallas_call(
        functools.partial(_dnscan_kernel, cb=cb),
        grid=(b, nblk),
        in_specs=specs(fwd) + specs(bwd) + [state_spec],
        out_specs=[pl.BlockSpec((None, nh, tb, dk), lambda bb, i: (bb, 0, fwd(i), 0)),
                   pl.BlockSpec((None, nh, tb, dk), lambda bb, i: (bb, 0, bwd(i), 0)),
                   state_spec],
        out_shape=[o_shape, o_shape, jax.ShapeDtypeStruct((b, 2, nh, dk, dk), F32)],
        scratch_shapes=[pltpu.VMEM((2, nh, dk, dk), F32)],
        compiler_params=_cparams(("arbitrary", "arbitrary")),
        name="dn_scan",
    )(q, k, kt, v, gc, gct, q, k, kt, v, gc, gct, s0)


def _mix0_out_kernel(x_ref, g1_ref, a_ref, of_ref, ob_ref, z_ref, nw_ref, w_ref, o_ref):
    o = of_ref[...] + ob_ref[...]
    on = o * lax.rsqrt(jnp.mean(o * o, axis=-1, keepdims=True) + EPS) * nw_ref[...]
    ot = jnp.concatenate([on[h] for h in range(DN_HEADS)], axis=1)
    z = z_ref[...]
    bmix = ot * (z * jax.nn.sigmoid(z))
    mix = jnp.concatenate([a_ref[...], bmix.astype(BF16)], axis=1)
    o_ref[...] = x_ref[...] + g1_ref[0] * _dot(mix, w_ref[...])


def _mix0_out(x, mod3, layer, row_fn, attn, o_f, o_b, proj, norm_w, w_out_bf16):
    b, l, d = x.shape
    tm = min(TOKEN_TILE, l)
    aw = Q_HEADS * HEAD_DIM
    zw = DN_HEADS * DN_DK
    head = pl.BlockSpec((None, DN_HEADS, tm, DN_DK), lambda bb, t: (bb, 0, t, 0))
    return pl.pallas_call(
        _mix0_out_kernel,
        grid=(b, l // tm),
        in_specs=[pl.BlockSpec((None, tm, d), lambda bb, t: (bb, t, 0)),
                  pl.BlockSpec((1, 1, d), _mod_spec(layer, row_fn, MOD_G1)),
                  pl.BlockSpec((None, tm, aw), lambda bb, t: (bb, t, 0)),
                  head, head,
                  pl.BlockSpec((None, tm, zw), lambda bb, t: (bb, t, COL_Z // zw)),
                  pl.BlockSpec((1, DN_DK), lambda bb, t: (0, 0)),
                  pl.BlockSpec((aw + zw, d), lambda bb, t: (0, 0))],
        out_specs=pl.BlockSpec((None, tm, d), lambda bb, t: (bb, t, 0)),
        out_shape=jax.ShapeDtypeStruct((b, l, d), F32),
        compiler_params=_cparams(("arbitrary", "arbitrary")),
        name="mix0_out",
    )(x, mod3, attn, o_f, o_b, proj, norm_w.astype(F32).reshape(1, DN_DK), w_out_bf16)


def _s5_kernel(*refs, nb, tm, pitch):
    x_ref = refs[0]
    sh_refs = refs[1:1 + nb]
    sc_refs = refs[1 + nb:1 + 2 * nb]
    wb_ref, wc_ref, lam_ref, x0_ref, y_ref, xfin_ref, bre, bim, st = refs[1 + 2 * nb:]
    d = pl.program_id(0)
    i = pl.program_id(1)
    nsub = SUBLANES
    nj = bre.shape[0] // nb
    half = nj * LANES
    for bb in range(nb):
        h = (_rms(x_ref[bb]) * (1.0 + sc_refs[bb][0]) + sh_refs[bb][0]).astype(BF16)
        for s in range(nsub):
            o = _dot(h[:, s * LANES:(s + 1) * LANES], wb_ref[s])
            for j in range(nj):
                bre[bb * nj + j, s * pitch:s * pitch + tm, :] = o[:, j * LANES:(j + 1) * LANES]
                bim[bb * nj + j, s * pitch:s * pitch + tm, :] = o[:, half + j * LANES:half + (j + 1) * LANES]

    @pl.when(i == 0)
    def _():
        st[...] = x0_ref[...]

    lam_re = lam_ref[0]
    lam_im = lam_ref[1]

    def body(k, carry):
        t = jnp.where(d == 0, k, tm - 1 - k)
        out = []
        for bb in range(nb):
            xr, xi = carry[2 * bb], carry[2 * bb + 1]
            br = jnp.concatenate([bre[bb * nj + j, pl.ds(t, nsub, stride=pitch), :] for j in range(nj)], axis=1)
            bi = jnp.concatenate([bim[bb * nj + j, pl.ds(t, nsub, stride=pitch), :] for j in range(nj)], axis=1)
            nr = lam_re * xr - lam_im * xi + br
            ni = lam_re * xi + lam_im * xr + bi
            for j in range(nj):
                bre[bb * nj + j, pl.ds(t, nsub, stride=pitch), :] = nr[:, j * LANES:(j + 1) * LANES]
                bim[bb * nj + j, pl.ds(t, nsub, stride=pitch), :] = ni[:, j * LANES:(j + 1) * LANES]
            out += [nr, ni]
        return tuple(out)

    init = tuple(st[bb, c] for bb in range(nb) for c in range(2))
    fin = lax.fori_loop(0, tm, body, init, unroll=8)
    for bb in range(nb):
        for c in range(2):
            st[bb, c] = fin[2 * bb + c]
            xfin_ref[bb, c] = fin[2 * bb + c]
    for bb in range(nb):
        for s in range(nsub):
            parts = [bre[bb * nj + j, s * pitch:s * pitch + tm, :] for j in range(nj)]
            parts += [bim[bb * nj + j, s * pitch:s * pitch + tm, :] for j in range(nj)]
            xs = jnp.concatenate(parts, axis=1).astype(BF16)
            y_ref[bb, :, s * LANES:(s + 1) * LANES] = _dot(xs, wc_ref[s])


def _s5_scan(x, mod3, layer, row_fn, wb, wc, lam, x0):
    b, l, d = x.shape
    tm = min(S5_TILE, l)
    nt = l // tm
    pitch = tm + S5_PITCH_PAD
    sw = wb.shape[-1] // 2
    nj = sw // LANES
    tile = lambda dd, i: jnp.where(dd == 0, i, nt - 1 - i)

    def mspec(which, bb):
        return pl.BlockSpec((1, 1, d), lambda dd, i: _mod_spec(layer, row_fn, which)(bb, i))

    state_spec = pl.BlockSpec((b, None, 2, SUBLANES, sw), lambda dd, i: (0, dd, 0, 0, 0))
    in_specs = [pl.BlockSpec((b, tm, d), lambda dd, i: (0, tile(dd, i), 0))]
    in_specs += [mspec(MOD_SH1, bb) for bb in range(b)] + [mspec(MOD_SC1, bb) for bb in range(b)]
    in_specs += [pl.BlockSpec((None, SUBLANES, LANES, 2 * sw), lambda dd, i: (dd, 0, 0, 0)),
                 pl.BlockSpec((None, SUBLANES, 2 * sw, LANES), lambda dd, i: (dd, 0, 0, 0)),
                 pl.BlockSpec((None, 2, SUBLANES, sw), lambda dd, i: (dd, 0, 0, 0)),
                 state_spec]
    return pl.pallas_call(
        functools.partial(_s5_kernel, nb=b, tm=tm, pitch=pitch),
        grid=(2, nt),
        in_specs=in_specs,
        out_specs=[pl.BlockSpec((None, b, tm, d), lambda dd, i: (dd, 0, tile(dd, i), 0)), state_spec],
        out_shape=[jax.ShapeDtypeStruct((2, b, l, d), F32),
                   jax.ShapeDtypeStruct((b, 2, 2, SUBLANES, sw), F32)],
        scratch_shapes=[pltpu.VMEM((b * nj, SUBLANES * pitch, LANES), F32),
                        pltpu.VMEM((b * nj, SUBLANES * pitch, LANES), F32),
                        pltpu.VMEM((b, 2, SUBLANES, sw), F32)],
        compiler_params=_cparams(("arbitrary", "arbitrary")),
        name="s5_scan",
    )(x, *([mod3] * (2 * b)), wb, wc, lam, x0)


def _s5_weights(lam_re, lam_im, log_step, b_re, b_im, c_re, c_im):
    lam = lax.complex(lam_re.astype(F32), lam_im.astype(F32))
    step = jnp.exp(log_step.astype(F32))[..., None]
    lam_bar = jnp.exp(lam * step)
    b_bar = ((lam_bar - 1.0) / lam)[..., None] * lax.complex(b_re.astype(F32), b_im.astype(F32))
    ng, p = lam.shape[1], lam.shape[2]
    gl = ng // SUBLANES
    eye = jnp.eye(gl, dtype=F32)

    def wb_of(part):
        t = part.reshape(2, SUBLANES, gl, p, S5_GROUP)
        w = jnp.einsum('dsgpi,gh->dsgihp', t, eye)
        return w.reshape(2, SUBLANES, gl * S5_GROUP, gl * p)

    def wc_of(part):
        t = part.reshape(2, SUBLANES, gl, S5_GROUP, p)
        w = jnp.einsum('dsgip,gh->dshpgi', t, eye)
        return w.reshape(2, SUBLANES, gl * p, gl * S5_GROUP)

    wb = jnp.concatenate([wb_of(jnp.real(b_bar)), wb_of(jnp.imag(b_bar))], axis=-1).astype(BF16)
    wc = jnp.concatenate([wc_of(c_re.astype(F32)), -wc_of(c_im.astype(F32))], axis=-2).astype(BF16)
    lam_t = jnp.stack([jnp.real(lam_bar).reshape(2, SUBLANES, gl * p),
                       jnp.imag(lam_bar).reshape(2, SUBLANES, gl * p)], axis=1)
    return wb, wc, lam_t


def _s5_glu_kernel(x_ref, sh_ref, sc_ref, g1_ref, yf_ref, yb_ref, dsk_ref, w1_ref, b1_ref, w2_ref, b2_ref, o_ref):
    x = x_ref[...]
    h = _rms(x) * (1.0 + sc_ref[0]) + sh_ref[0]
    y = dsk_ref[...] * h + yf_ref[...] + yb_ref[...]
    z = jax.nn.gelu(y).astype(BF16)
    o = (_dot(z, w1_ref[...]) + b1_ref[...]) * jax.nn.sigmoid(_dot(z, w2_ref[...]) + b2_ref[...])
    o_ref[...] = x + g1_ref[0] * o


def _s5_glu(x, mod3, layer, row_fn, y2, d_skip, w1, b1, w2, b2):
    b, l, d = x.shape
    tm = min(TOKEN_TILE, l)
    mspec = lambda which: pl.BlockSpec((1, 1, d), _mod_spec(layer, row_fn, which))
    const = lambda shape: pl.BlockSpec(shape, lambda bb, t: (0,) * len(shape))
    return pl.pallas_call(
        _s5_glu_kernel,
        grid=(b, l // tm),
        in_specs=[pl.BlockSpec((None, tm, d), lambda bb, t: (bb, t, 0)),
                  mspec(MOD_SH1), mspec(MOD_SC1), mspec(MOD_G1),
                  pl.BlockSpec((None, None, tm, d), lambda bb, t: (0, bb, t, 0)),
                  pl.BlockSpec((None, None, tm, d), lambda bb, t: (1, bb, t, 0)),
                  const((1, d)), const((d, d)), const((1, d)), const((d, d)), const((1, d))],
        out_specs=pl.BlockSpec((None, tm, d), lambda bb, t: (bb, t, 0)),
        out_shape=jax.ShapeDtypeStruct((b, l, d), F32),
        compiler_params=_cparams(("arbitrary", "arbitrary")),
        name="s5_glu",
    )(x, mod3, mod3, mod3, y2, y2, d_skip.astype(F32).reshape(1, d), w1.astype(BF16), b1.astype(F32).reshape(1, d),
      w2.astype(BF16), b2.astype(F32).reshape(1, d))


def _route_kernel(x_ref, sh_ref, sc_ref, rw_ref, rb_ref, cin_ref, tri_ref, hp_ref, meta_ref, cnt_ref, carry):
    first = (pl.program_id(0) == 0) & (pl.program_id(1) == 0)

    @pl.when(first)
    def _():
        carry[...] = cin_ref[...]

    h = _modulated(x_ref, sh_ref, sc_ref)
    half = h.shape[1] // 2
    lo = pltpu.bitcast(h[:, :half].astype(BF16).astype(F32), U32)
    hi = pltpu.bitcast(h[:, half:].astype(BF16).astype(F32), U32)
    packed = (lo >> 16) | (hi & jnp.uint32(0xFFFF0000))
    pieces = half // LANES
    for j in range(pieces):
        hp_ref[pl.ds(j, h.shape[0], stride=pieces), :] = packed[:, j * LANES:(j + 1) * LANES]

    logits = _dot_hi(h, rw_ref[...]) + rb_ref[...]
    lane = lax.broadcasted_iota(I32, logits.shape, 1)
    cur = jnp.where(lane < N_EXPERTS, logits, -jnp.inf)
    vals, idxs = [], []
    hot = jnp.zeros(logits.shape, F32)
    for _ in range(TOP_K):
        m = jnp.max(cur, axis=-1, keepdims=True)
        idx = jnp.min(jnp.where(cur == m, lane, LANES), axis=-1, keepdims=True)
        sel = lane == idx
        hot = hot + sel.astype(F32)
        cur = jnp.where(sel, -jnp.inf, cur)
        vals.append(m)
        idxs.append(idx)
    es = [jnp.exp(v - vals[0]) for v in vals]
    tot = es[0] + es[1] + es[2] + es[3]
    before = _dot(tri_ref[...], hot.astype(BF16)) + carry[...]
    meta = jnp.zeros(logits.shape, F32)
    for k in range(TOP_K):
        rank = jnp.sum(jnp.where(lane == idxs[k], before, 0.0), axis=-1, keepdims=True)
        meta = jnp.where(lane == k, idxs[k].astype(F32), meta)
        meta = jnp.where(lane == TOP_K + k, es[k] / tot, meta)
        meta = jnp.where(lane == 2 * TOP_K + k, rank, meta)
    meta_ref[...] = meta
    carry[...] = carry[...] + jnp.sum(hot, axis=0, keepdims=True)
    cnt_ref[...] = carry[...]


def _route(x, mod3, layer, row_fn, rw_pad, rb_pad, counts_in):
    b, l, d = x.shape
    tm = min(MOE_TOKEN_TILE, l)
    idx = jnp.arange(tm)
    tri = (idx[:, None] > idx[None, :]).astype(BF16)
    mspec = lambda which: pl.BlockSpec((1, 1, d), _mod_spec(layer, row_fn, which))
    const = lambda shape: pl.BlockSpec(shape, lambda bb, t: (0,) * len(shape))
    return pl.pallas_call(
        _route_kernel,
        grid=(b, l // tm),
        in_specs=[pl.BlockSpec((None, tm, d), lambda bb, t: (bb, t, 0)),
                  mspec(MOD_SH2), mspec(MOD_SC2),
                  const((d, LANES)), const((1, LANES)), const((1, LANES)), const((tm, tm))],
        out_specs=[pl.BlockSpec((None, tm * (d // 2 // LANES), LANES), lambda bb, t: (bb, t, 0)),
                   pl.BlockSpec((None, tm, LANES), lambda bb, t: (bb, t, 0)),
                   const((1, LANES))],
        out_shape=[jax.ShapeDtypeStruct((b, l * (d // 2 // LANES), LANES), U32),
                   jax.ShapeDtypeStruct((b, l, LANES), F32),
                   jax.ShapeDtypeStruct((1, LANES), F32)],
        scratch_shapes=[pltpu.VMEM((1, LANES), F32)],
        compiler_params=_cparams(("arbitrary", "arbitrary")),
        name="moe_route",
    )(x, mod3, mod3, rw_pad, rb_pad, counts_in, tri)


def _dest_kernel(meta_ref, ps_ref, o_ref):
    meta = meta_ref[...]
    lane = lax.broadcasted_iota(I32, meta.shape, 1)
    ps = ps_ref[...]
    dst = jnp.zeros(meta.shape, F32)
    for k in range(TOP_K):
        idx = meta[:, k:k + 1].astype(I32)
        base = jnp.sum(jnp.where(lane == idx, ps, 0.0), axis=-1, keepdims=True)
        dst = jnp.where(lane == k, base + meta[:, 2 * TOP_K + k:2 * TOP_K + k + 1], dst)
    dt = dst.T
    o_ref[0] = jnp.concatenate([dt[k:k + 1] for k in range(TOP_K)], axis=1).astype(I32)


def _dest(meta, pad_start_row):
    b, l, _ = meta.shape
    tt = min(MOE_TOKEN_TILE, l)
    nt = l // tt
    return pl.pallas_call(
        _dest_kernel,
        grid=(b, nt),
        in_specs=[pl.BlockSpec((None, tt, LANES), lambda bb, t: (bb, t, 0)),
                  pl.BlockSpec((1, LANES), lambda bb, t: (0, 0))],
        out_specs=pl.BlockSpec((1, 1, TOP_K * tt), lambda bb, t: (bb * nt + t, 0, 0)),
        out_shape=jax.ShapeDtypeStruct((b * nt, 1, TOP_K * tt), I32),
        compiler_params=_cparams(("arbitrary", "arbitrary")),
        name="moe_dest",
    )(meta, pad_start_row)


def _dispatch_kernel(*refs, tt, pieces):
    dest_hbm, hp_ref = refs[:2]
    xs_out, dest_smem, sem_i, sem = refs[-4:]
    i = pl.program_id(0)
    cp = pltpu.make_async_copy(dest_hbm.at[i, 0], dest_smem, sem_i)
    cp.start()
    cp.wait()

    def row_copy(r, k):
        src = pl.multiple_of(r * pieces, pieces)
        dst = pl.multiple_of(dest_smem[k * tt + r] * pieces, pieces)
        return pltpu.make_async_copy(hp_ref.at[pl.ds(src, pieces)], xs_out.at[pl.ds(dst, pieces)], sem)

    def issue(r, c):
        for k in range(TOP_K):
            row_copy(r, k).start(priority=k % 2)
        return c

    lax.fori_loop(0, tt, issue, 0, unroll=4)

    def drain(r, c):
        for k in range(TOP_K):
            row_copy(r, k).wait()
        return c

    lax.fori_loop(0, tt, drain, 0, unroll=4)


def _dispatch(dest, hp, xs, pieces, n_rows):
    n = hp.shape[0] // pieces
    tt = dest.shape[-1] // TOP_K
    in_specs = [pl.BlockSpec(memory_space=pl.ANY), pl.BlockSpec((tt * pieces, LANES), lambda i: (i, 0))]
    args = [dest, hp]
    aliases = {}
    if xs is not None:
        in_specs.append(pl.BlockSpec(memory_space=pl.ANY))
        args.append(xs)
        aliases = {2: 0}
    return pl.pallas_call(
        functools.partial(_dispatch_kernel, tt=tt, pieces=pieces),
        grid=(n // tt,),
        in_specs=in_specs,
        out_specs=pl.BlockSpec(memory_space=pl.ANY),
        out_shape=jax.ShapeDtypeStruct((n_rows * pieces, LANES), U32),
        scratch_shapes=[pltpu.SMEM((tt * TOP_K,), I32), pltpu.SemaphoreType.DMA, pltpu.SemaphoreType.DMA],
        input_output_aliases=aliases,
        compiler_params=_cparams(("arbitrary",)),
        name="moe_dispatch",
    )(*args)


def _expert_kernel(be_ref, nact_ref, nvalid_ref, xs_ref, wgu_ref, bgu_ref, wd_ref, bd_ref, y_ref, wgu_bf, wd_bf):
    i = pl.program_id(0)
    e = be_ref[i]
    prev = be_ref[jnp.maximum(i - 1, 0)]

    @pl.when((i == 0) | (e != prev))
    def _():
        wgu_bf[...] = wgu_ref[0].astype(BF16)
        wd_bf[...] = wd_ref[0].astype(BF16)

    rows = MOE_ROWS
    in_pieces = xs_ref.shape[0] // rows
    out_pieces = y_ref.shape[0] // rows

    @pl.when(i < nact_ref[0])
    def _():
        live = lax.broadcasted_iota(I32, (rows, LANES), 0) < nvalid_ref[i]
        ws = [jnp.where(live, xs_ref[pl.ds(j, rows, stride=in_pieces), :], jnp.uint32(0)) for j in range(in_pieces)]
        lo = [pltpu.bitcast(w << 16, F32) for w in ws]
        hi = [pltpu.bitcast(w & jnp.uint32(0xFFFF0000), F32) for w in ws]
        x = jnp.concatenate(lo + hi, axis=1).astype(BF16)
        gu = _dot(x, wgu_bf[...]) + bgu_ref[0]
        de = gu.shape[1] // 2
        gate = jnp.minimum(gu[:, :de], SWIGLU_LIMIT)
        up = jnp.clip(gu[:, de:], -SWIGLU_LIMIT, SWIGLU_LIMIT)
        act = (up + 1.0) * gate * jax.nn.sigmoid(SWIGLU_ALPHA * gate)
        y = _dot(act.astype(BF16), wd_bf[...]) + bd_ref[0]
        for j in range(out_pieces):
            y_ref[pl.ds(j, rows, stride=out_pieces), :] = y[:, j * LANES:(j + 1) * LANES]

    @pl.when(i >= nact_ref[0])
    def _():
        y_ref[...] = jnp.zeros(y_ref.shape, y_ref.dtype)


def _experts(block_e, n_active, n_valid, xs, w_gu, b_gu, w_d, b_d, layer):
    _, ne, d, de2 = w_gu.shape
    in_pieces = d // 2 // LANES
    out_pieces = d // LANES
    n_rows = xs.shape[0] // in_pieces
    nblk = n_rows // MOE_ROWS
    grid_spec = pltpu.PrefetchScalarGridSpec(
        num_scalar_prefetch=3,
        grid=(nblk,),
        in_specs=[pl.BlockSpec((MOE_ROWS * in_pieces, LANES), lambda i, be, na, nv: (i, 0)),
                  pl.BlockSpec((None, 1, d, de2), lambda i, be, na, nv: (layer, be[i], 0, 0)),
                  pl.BlockSpec((None, 1, 1, de2), lambda i, be, na, nv: (layer, be[i], 0, 0)),
                  pl.BlockSpec((None, 1, de2 // 2, d), lambda i, be, na, nv: (layer, be[i], 0, 0)),
                  pl.BlockSpec((None, 1, 1, d), lambda i, be, na, nv: (layer, be[i], 0, 0))],
        out_specs=pl.BlockSpec((MOE_ROWS * out_pieces, LANES), lambda i, be, na, nv: (i, 0)),
        scratch_shapes=[pltpu.VMEM((d, de2), BF16), pltpu.VMEM((de2 // 2, d), BF16)],
    )
    depth = w_gu.shape[0]
    return pl.pallas_call(
        _expert_kernel,
        grid_spec=grid_spec,
        out_shape=jax.ShapeDtypeStruct((n_rows * out_pieces, LANES), F32),
        compiler_params=_cparams(("arbitrary",)),
        name="moe_experts",
    )(block_e, n_active, n_valid, xs, w_gu, b_gu.reshape(depth, ne, 1, de2), w_d, b_d.reshape(depth, ne, 1, d))


def _combine_kernel(*refs, tt, final):
    if final:
        dest_hbm, yb_hbm, x_ref, meta_ref, g2_ref, fw_ref, o_ref, dest_smem, buf, sem_i, sem = refs
    else:
        dest_hbm, yb_hbm, x_ref, meta_ref, g2_ref, o_ref, dest_smem, buf, sem_i, sem = refs
    i = pl.program_id(0) * pl.num_programs(1) + pl.program_id(1)
    cp = pltpu.make_async_copy(dest_hbm.at[i, 0], dest_smem, sem_i)
    cp.start()
    cp.wait()

    pieces = buf.shape[1] // tt

    def row_copy(r, k):
        src = pl.multiple_of(dest_smem[k * tt + r] * pieces, pieces)
        dst = pl.multiple_of(r * pieces, pieces)
        return pltpu.make_async_copy(yb_hbm.at[pl.ds(src, pieces)], buf.at[k, pl.ds(dst, pieces)], sem)

    def issue(r, c):
        for k in range(TOP_K):
            row_copy(r, k).start(priority=k % 2)
        return c

    lax.fori_loop(0, tt, issue, 0, unroll=4)

    def drain(r, c):
        for k in range(TOP_K):
            row_copy(r, k).wait()
        return c

    lax.fori_loop(0, tt, drain, 0, unroll=4)
    meta = meta_ref[...]
    g2 = g2_ref[0]
    cols = []
    for j in range(pieces):
        y = meta[:, TOP_K:TOP_K + 1] * buf[0, pl.ds(j, tt, stride=pieces), :]
        for k in range(1, TOP_K):
            y = y + meta[:, TOP_K + k:TOP_K + k + 1] * buf[k, pl.ds(j, tt, stride=pieces), :]
        sl = slice(j * LANES, (j + 1) * LANES)
        cols.append(x_ref[:, sl] + g2[:, sl] * y)
    xn = jnp.concatenate(cols, axis=1)
    if final:
        xn = _rms(xn) * fw_ref[...]
    o_ref[...] = xn


def _combine(dest, yb, x, meta, mod3, layer, row_fn, final_w=None):
    b, l, d = x.shape
    tt = min(MOE_TOKEN_TILE, l)
    nt = l // tt
    final = final_w is not None
    in_specs = [pl.BlockSpec(memory_space=pl.ANY), pl.BlockSpec(memory_space=pl.ANY),
                pl.BlockSpec((None, tt, d), lambda bb, t: (bb, t, 0)),
                pl.BlockSpec((None, tt, LANES), lambda bb, t: (bb, t, 0)),
                pl.BlockSpec((1, 1, d), _mod_spec(layer, row_fn, MOD_G2))]
    args = [dest, yb, x, meta, mod3]
    if final:
        in_specs.append(pl.BlockSpec((1, d), lambda bb, t: (0, 0)))
        args.append(final_w.astype(F32).reshape(1, d))
    return pl.pallas_call(
        functools.partial(_combine_kernel, tt=tt, final=final),
        grid=(b, nt),
        in_specs=in_specs,
        out_specs=pl.BlockSpec((None, tt, d), lambda bb, t: (bb, t, 0)),
        out_shape=jax.ShapeDtypeStruct((b, l, d), F32),
        scratch_shapes=[pltpu.SMEM((tt * TOP_K,), I32), pltpu.VMEM((TOP_K, tt * (d // LANES), LANES), F32),
                        pltpu.SemaphoreType.DMA, pltpu.SemaphoreType.DMA],
        compiler_params=_cparams(("arbitrary", "arbitrary")),
        name="moe_combine",
    )(*args)


def _moe(parts, mod3, layer, router_w, router_b, w_gu, b_gu, w_d, b_d, final_w=None):
    d = router_w.shape[0]
    rw_pad = jnp.zeros((d, LANES), F32).at[:, :N_EXPERTS].set(router_w.astype(F32))
    rb_pad = jnp.zeros((1, LANES), F32).at[0, :N_EXPERTS].set(router_b.astype(F32))
    counts = jnp.zeros((1, LANES), F32)
    routed = []
    for x, row_fn in parts:
        hp, meta, counts = _route(x, mod3, layer, row_fn, rw_pad, rb_pad, counts)
        routed.append((hp, meta))
    n_assign = sum(x.shape[0] * x.shape[1] for x, _ in parts) * TOP_K
    nblk = -(-(n_assign + N_EXPERTS * (MOE_ROWS - 1)) // MOE_ROWS)
    cnt = counts[0, :N_EXPERTS].astype(I32)
    padded = (cnt + MOE_ROWS - 1) // MOE_ROWS * MOE_ROWS
    pad_end = jnp.cumsum(padded)
    pad_start = pad_end - padded
    blk_row = jnp.arange(nblk, dtype=I32)[:, None] * MOE_ROWS
    block_e = jnp.minimum(jnp.sum((pad_end[None, :] <= blk_row).astype(I32), axis=1), N_EXPERTS - 1).astype(I32)
    n_active = (pad_end[-1] // MOE_ROWS).astype(I32).reshape(1)
    ps_row = jnp.zeros((1, LANES), F32).at[0, :N_EXPERTS].set(pad_start.astype(F32))
    first_blk = pad_start // MOE_ROWS
    n_valid = jnp.clip(cnt[block_e] - (jnp.arange(nblk, dtype=I32) - first_blk[block_e]) * MOE_ROWS, 0, MOE_ROWS).astype(I32)
    pieces = d // 2 // LANES
    xs = None
    dests = []
    for (x, _), (hp, meta) in zip(parts, routed):
        dest = _dest(meta, ps_row)
        dests.append(dest)
        xs = _dispatch(dest, hp.reshape(-1, LANES), xs, pieces, nblk * MOE_ROWS)
    yb = _experts(block_e, n_active, n_valid, xs, w_gu, b_gu, w_d, b_d, layer)
    outs = []
    for idx, ((x, row_fn), (hp, meta)) in enumerate(zip(parts, routed)):
        fw = final_w if idx == 0 else None
        outs.append(_combine(dests[idx], yb, x, meta, mod3, layer, row_fn, fw))
    return outs


def kernel(x, c, ctx, c_ctx, ada_w, ada_b, ab_w_in, ab_q_gain, ab_k_gain, dn_conv_w, dn_a_log, dn_dt_bias, dn_norm_w, ab_w_out, s5_lambda_re, s5_lambda_im, s5_log_step, s5_b_re, s5_b_im, s5_c_re, s5_c_im, s5_d, s5_glu_w1, s5_glu_b1, s5_glu_w2, s5_glu_b2, moe_router_w, moe_router_b, moe_w_gate_up, moe_b_gate_up, moe_w_down, moe_b_down, final_norm_w):
    b, l, d = x.shape
    lc = ctx.shape[1]
    depth = ada_w.shape[0]
    assert depth == 2 and b < MOD_ROWS
    lat_row = lambda bb: bb
    ctx_row = lambda bb: b

    rows = jnp.zeros((MOD_ROWS, d), F32).at[:b].set(c.astype(F32)).at[b].set(c_ctx.astype(F32))
    mod3 = _ada(rows, ada_w, ada_b).reshape(depth * MOD_ROWS * 6, 1, d)

    w_in = ab_w_in[0]
    aq, akv, bqk = Q_HEADS * HEAD_DIM, KV_HEADS * HEAD_DIM, DN_HEADS * DN_DK
    o_k, o_v, o_dn, o_ab, o_z = aq, aq + akv, aq + 2 * akv, aq + 2 * akv + 3 * bqk, aq + 2 * akv + 3 * bqk + 4 * DN_HEADS
    w_perm = jnp.concatenate([w_in[:, :aq], w_in[:, o_dn:o_ab], w_in[:, o_z:], w_in[:, o_k:o_v], w_in[:, o_v:o_dn],
                              w_in[:, o_ab:o_z], jnp.zeros((d, IN_COLS_PAD - w_in.shape[1]), w_in.dtype)], axis=1).astype(BF16)
    proj_l = _inproj(x, mod3, 0, lat_row, w_perm)
    proj_c = _inproj(ctx, mod3, 0, ctx_row, w_perm)

    cos, sin = _rope_tables(l)
    ones = jnp.ones((lc, KV_HEADS * HEAD_DIM), F32)
    q_l, kt_l, v_l = _qkprep(proj_l, ab_q_gain[0], ab_k_gain[0], cos, sin, min(ATTN_K_TILE, l))
    q_c, kt_c, v_c = _qkprep(proj_c, ab_q_gain[0], ab_k_gain[0], ones, jnp.zeros_like(ones), lc)
    a_l = _attention(q_l, kt_c, v_c, kt_l, v_l)
    a_c = _attention(q_c, kt_c, v_c)

    dq_c, dk_c, dkt_c, dv_c, gc_c, gct_c = _dnprep(proj_c, dn_conv_w[0], dn_a_log[0], dn_dt_bias[0])
    dq_l, dk_l, dkt_l, dv_l, gc_l, gct_l = _dnprep(proj_l, dn_conv_w[0], dn_a_log[0], dn_dt_bias[0])
    s_zero = jnp.zeros((b, 2, DN_HEADS, DN_DK, DN_DK), F32)
    of_c, ob_c, s_ctx = _dnscan(dq_c, dk_c, dkt_c, dv_c, gc_c, gct_c, s_zero)
    of_l, ob_l, _ = _dnscan(dq_l, dk_l, dkt_l, dv_l, gc_l, gct_l, s_ctx)

    w_out = ab_w_out[0].astype(BF16)
    x = _mix0_out(x, mod3, 0, lat_row, a_l, of_l, ob_l, proj_l, dn_norm_w[0], w_out)
    ctx = _mix0_out(ctx, mod3, 0, ctx_row, a_c, of_c, ob_c, proj_c, dn_norm_w[0], w_out)

    x, ctx = _moe([(x, lat_row), (ctx, ctx_row)], mod3, 0, moe_router_w[0], moe_router_b[0],
                  moe_w_gate_up, moe_b_gate_up, moe_w_down, moe_b_down)

    wb, wc, lam_t = _s5_weights(s5_lambda_re[0], s5_lambda_im[0], s5_log_step[0], s5_b_re[0], s5_b_im[0],
                                s5_c_re[0], s5_c_im[0])
    sw = lam_t.shape[-1]
    x0 = jnp.zeros((b, 2, 2, SUBLANES, sw), F32)
    _, x_ctx = _s5_scan(ctx, mod3, 1, ctx_row, wb, wc, lam_t, x0)
    y2, _ = _s5_scan(x, mod3, 1, lat_row, wb, wc, lam_t, x_ctx)
    x = _s5_glu(x, mod3, 1, lat_row, y2, s5_d[0], s5_glu_w1[0], s5_glu_b1[0], s5_glu_w2[0], s5_glu_b2[0])

    (x,) = _moe([(x, lat_row)], mod3, 1, moe_router_w[1], moe_router_b[1], moe_w_gate_up, moe_b_gate_up,
                moe_w_down, moe_b_down, final_w=final_norm_w)
    return x
```

```python
import functools
import math

import jax
import jax.numpy as jnp
from jax import lax
from jax.experimental import pallas as pl
from jax.experimental.pallas import tpu as pltpu

F32 = jnp.float32
BF16 = jnp.bfloat16
U32 = jnp.uint32
I32 = jnp.int32

EPS = 1e-6
GRID_W = 64
HEAD_DIM = 64
Q_HEADS = 8
KV_HEADS = 2
GROUP = Q_HEADS // KV_HEADS
ROPE_THETA = 10000.0
DN_HEADS = 8
DN_DK = 64
DN_CONV = 5
DN_CHUNK = 64
S5_GROUP = 16
S5_STATE = 64
N_EXPERTS = 32
TOP_K = 4
SWIGLU_LIMIT = 7.0
SWIGLU_ALPHA = 1.702

LANES = 128
SUBLANES = 8
TOKEN_TILE = 1024
DN_PREP_TILE = 512
MOE_TOKEN_TILE = 1024
ATTN_Q_TILE = 256
ATTN_K_TILE = 2048
DN_BLOCK_CHUNKS = 4
MOE_ROWS = 896
S5_TILE = 256
S5_PITCH_PAD = 4
VMEM_LIMIT = 56 * 1024 * 1024

MOD_SH1, MOD_SC1, MOD_G1, MOD_SH2, MOD_SC2, MOD_G2 = range(6)
MOD_ROWS = 8

COL_Q = 0
COL_DNQ = 512
COL_DNK = 1024
COL_DNV = 1536
COL_Z = 2048
COL_K = 2560
COL_V = 2688
COL_AB = 2816
IN_COLS_PAD = 2944


def _cparams(sem):
    return pltpu.CompilerParams(dimension_semantics=sem, vmem_limit_bytes=VMEM_LIMIT)


def _split3(a):
    a1 = a.astype(BF16)
    r = a - a1.astype(F32)
    a2 = r.astype(BF16)
    a3 = (r - a2.astype(F32)).astype(BF16)
    return a1, a2, a3


def _dot(a, b):
    return jnp.dot(a, b, preferred_element_type=F32)


def _dot_exact_lhs(m_bf16, x):
    x1, x2, x3 = _split3(x)
    return _dot(m_bf16, x1) + _dot(m_bf16, x2) + _dot(m_bf16, x3)


def _dot_exact_rhs(x, m_bf16):
    x1, x2, x3 = _split3(x)
    return _dot(x1, m_bf16) + _dot(x2, m_bf16) + _dot(x3, m_bf16)


def _dot_hi(a, b):
    a1, a2, _ = _split3(a)
    b1, b2, _ = _split3(b)
    return _dot(a1, b1) + _dot(a1, b2) + _dot(a2, b1)


def _rms(x):
    return x * lax.rsqrt(jnp.mean(x * x, axis=-1, keepdims=True) + EPS)


def _modulated(x_ref, sh_ref, sc_ref):
    return _rms(x_ref[...]) * (1.0 + sc_ref[0]) + sh_ref[0]


def _mod_spec(layer, row_fn, which):
    d = None

    def imap(b, t):
        return ((layer * MOD_ROWS + row_fn(b)) * 6 + which, 0, 0)

    return imap


def _ada_kernel(a_ref, w_ref, b_ref, o_ref):
    a = a_ref[...]
    a = a * jax.nn.sigmoid(a)
    o_ref[0] = _dot_hi(a, w_ref[0]) + b_ref[0]


def _ada(rows, ada_w, ada_b):
    depth, d, n = ada_w.shape
    tn = 1536
    return pl.pallas_call(
        _ada_kernel,
        grid=(depth, n // tn),
        in_specs=[pl.BlockSpec((MOD_ROWS, d), lambda l, j: (0, 0)),
                  pl.BlockSpec((1, d, tn), lambda l, j: (l, 0, j)),
                  pl.BlockSpec((1, 1, tn), lambda l, j: (l, 0, j))],
        out_specs=pl.BlockSpec((1, MOD_ROWS, tn), lambda l, j: (l, 0, j)),
        out_shape=jax.ShapeDtypeStruct((depth, MOD_ROWS, n), F32),
        compiler_params=_cparams(("arbitrary", "arbitrary")),
        name="ada_mod",
    )(rows, ada_w, ada_b.reshape(depth, 1, n))


def _inproj_kernel(x_ref, sh_ref, sc_ref, w_ref, o_ref):
    h = _modulated(x_ref, sh_ref, sc_ref)
    o_ref[...] = _dot(h.astype(BF16), w_ref[...])


def _inproj(x, mod3, layer, row_fn, w_bf16):
    b, l, d = x.shape
    tm = min(TOKEN_TILE, l)
    n = w_bf16.shape[1]
    mspec = lambda which: pl.BlockSpec((1, 1, d), _mod_spec(layer, row_fn, which))
    return pl.pallas_call(
        _inproj_kernel,
        grid=(b, l // tm),
        in_specs=[pl.BlockSpec((None, tm, d), lambda bb, t: (bb, t, 0)),
                  mspec(MOD_SH1), mspec(MOD_SC1),
                  pl.BlockSpec((d, n), lambda bb, t: (0, 0))],
        out_specs=pl.BlockSpec((None, tm, n), lambda bb, t: (bb, t, 0)),
        out_shape=jax.ShapeDtypeStruct((b, l, n), F32),
        compiler_params=_cparams(("arbitrary", "arbitrary")),
        name="in_proj",
    )(x, mod3, mod3, w_bf16)


def _head_rot(x, width):
    lane = lax.broadcasted_iota(I32, x.shape, 1)
    first = (lane % HEAD_DIM) < (HEAD_DIM // 2)
    return jnp.where(first, pltpu.roll(x, width - HEAD_DIM // 2, 1), pltpu.roll(x, HEAD_DIM // 2, 1))


def _qkprep_kernel(q_ref, k_ref, v_ref, qg_ref, kg_ref, cos_ref, sin_ref, gq_ref, gk_ref, qo_ref, kt_ref, vo_ref):
    cos = cos_ref[...]
    sin = sin_ref[...]
    q = q_ref[...]
    qn = q * lax.rsqrt(_dot_exact_rhs(q * q, gq_ref[...]) + EPS) * qg_ref[...]
    cos4 = jnp.concatenate([cos] * 4, axis=1)
    sin4 = jnp.concatenate([sin] * 4, axis=1)
    qr = qn * cos4 + _head_rot(qn, Q_HEADS * HEAD_DIM) * sin4
    qo_ref[...] = (qr * (HEAD_DIM ** -0.5 * math.log2(math.e))).astype(BF16)
    k = k_ref[...]
    kn = k * lax.rsqrt(_dot_exact_rhs(k * k, gk_ref[...]) + EPS) * kg_ref[...]
    kr = kn * cos + _head_rot(kn, KV_HEADS * HEAD_DIM) * sin
    kt = kr.T
    kt_ref[0] = kt[:HEAD_DIM].astype(BF16)
    kt_ref[1] = kt[HEAD_DIM:].astype(BF16)
    v = v_ref[...]
    ones = jnp.ones((v.shape[0], HEAD_DIM), F32)
    vo_ref[0] = jnp.concatenate([v[:, :HEAD_DIM], ones], axis=1).astype(BF16)
    vo_ref[1] = jnp.concatenate([v[:, HEAD_DIM:], ones], axis=1).astype(BF16)


def _group_matrix(width, scale):
    g = jnp.arange(width) // HEAD_DIM
    return ((g[:, None] == g[None, :]).astype(F32) * scale).astype(BF16)


def _qkprep(proj, q_gain, k_gain, cos, sin, tk):
    b, l, _ = proj.shape
    qw = Q_HEADS * HEAD_DIM
    kw = KV_HEADS * HEAD_DIM
    qg = jnp.tile(q_gain.astype(F32), Q_HEADS).reshape(1, qw)
    kg = jnp.tile(k_gain.astype(F32), KV_HEADS).reshape(1, kw)
    const = lambda shape: pl.BlockSpec(shape, lambda bb, t: (0,) * len(shape))
    return pl.pallas_call(
        _qkprep_kernel,
        grid=(b, l // tk),
        in_specs=[pl.BlockSpec((None, tk, qw), lambda bb, t: (bb, t, COL_Q // qw)),
                  pl.BlockSpec((None, tk, kw), lambda bb, t: (bb, t, COL_K // kw)),
                  pl.BlockSpec((None, tk, kw), lambda bb, t: (bb, t, COL_V // kw)),
                  const((1, qw)), const((1, kw)),
                  pl.BlockSpec((tk, kw), lambda bb, t: (t, 0)),
                  pl.BlockSpec((tk, kw), lambda bb, t: (t, 0)),
                  const((qw, qw)), const((kw, kw))],
        out_specs=[pl.BlockSpec((None, tk, qw), lambda bb, t: (bb, t, 0)),
                   pl.BlockSpec((None, KV_HEADS, None, HEAD_DIM, tk), lambda bb, t: (bb, 0, t, 0, 0)),
                   pl.BlockSpec((None, KV_HEADS, tk, 2 * HEAD_DIM), lambda bb, t: (bb, 0, t, 0))],
        out_shape=[jax.ShapeDtypeStruct((b, l, qw), BF16),
                   jax.ShapeDtypeStruct((b, KV_HEADS, l // tk, HEAD_DIM, tk), BF16),
                   jax.ShapeDtypeStruct((b, KV_HEADS, l, 2 * HEAD_DIM), BF16)],
        compiler_params=_cparams(("arbitrary", "arbitrary")),
        name="qk_prep",
    )(proj, proj, proj, qg, kg, cos, sin, _group_matrix(qw, 1.0 / HEAD_DIM), _group_matrix(kw, 1.0 / HEAD_DIM))


def _rope_tables(length):
    rows = length // GRID_W
    row = jnp.broadcast_to(jnp.arange(rows, dtype=I32)[:, None], (rows, GRID_W)).reshape(-1).astype(F32)
    col = jnp.broadcast_to(jnp.arange(GRID_W, dtype=I32)[None, :], (rows, GRID_W)).reshape(-1).astype(F32)
    n_axis = HEAD_DIM // 4
    inv_freq = ROPE_THETA ** (-jnp.arange(n_axis, dtype=F32) / n_axis)
    ang = jnp.concatenate([row[:, None] * inv_freq, col[:, None] * inv_freq], axis=-1)
    c, s = jnp.cos(ang), jnp.sin(ang)
    cos_h = jnp.concatenate([c, c], axis=-1)
    sin_h = jnp.concatenate([-s, s], axis=-1)
    return jnp.tile(cos_h, (1, KV_HEADS)), jnp.tile(sin_h, (1, KV_HEADS))


def _attn_kernel(*refs, tq, n_lat_chunks, has_lat):
    if has_lat:
        q_ref, ktc_ref, vc_ref, ktl_ref, vl_ref, o_ref = refs
    else:
        q_ref, ktc_ref, vc_ref, o_ref = refs
    q = q_ref[...]
    qs = jnp.concatenate([q[:, g * HEAD_DIM:(g + 1) * HEAD_DIM] for g in range(GROUP)], axis=0)
    rows = GROUP * tq

    def step(kt, v_ones, m, acc):
        s = _dot(qs, kt)
        m_new = jnp.maximum(m, jnp.max(s, axis=-1, keepdims=True))
        p = jnp.exp2((s - m_new).astype(BF16))
        acc = jnp.exp2(m - m_new) * acc + _dot(p, v_ones)
        return m_new, acc

    m0 = jnp.full((rows, 1), -1e30, F32)
    a0 = jnp.zeros((rows, 2 * HEAD_DIM), F32)
    carry = step(ktc_ref[0], vc_ref[...], m0, a0)
    if has_lat:
        tk = ktl_ref.shape[-1]

        def body(j, c):
            off = pl.multiple_of(j * tk, tk)
            return step(ktl_ref[j], vl_ref[pl.ds(off, tk), :], *c)

        carry = lax.fori_loop(0, n_lat_chunks, body, carry, unroll=True)
    _, acc = carry
    o = acc[:, :HEAD_DIM] / acc[:, HEAD_DIM:HEAD_DIM + 1]
    o_ref[...] = jnp.concatenate([o[g * tq:(g + 1) * tq] for g in range(GROUP)], axis=1).astype(o_ref.dtype)


def _attention(q, ktc, vc, ktl=None, vl=None):
    b, lq, qw = q.shape
    tq = min(ATTN_Q_TILE, lq)
    gw = GROUP * HEAD_DIM
    lc = vc.shape[2]
    has_lat = ktl is not None
    in_specs = [pl.BlockSpec((None, tq, gw), lambda bb, h, i: (bb, i, h)),
                pl.BlockSpec((None, None, 1, HEAD_DIM, lc), lambda bb, h, i: (bb, h, 0, 0, 0)),
                pl.BlockSpec((None, None, lc, 2 * HEAD_DIM), lambda bb, h, i: (bb, h, 0, 0))]
    args = [q, ktc, vc]
    n_chunks = 0
    if has_lat:
        n_chunks, tk = ktl.shape[2], ktl.shape[4]
        ll = vl.shape[2]
        in_specs += [pl.BlockSpec((None, None, n_chunks, HEAD_DIM, tk), lambda bb, h, i: (bb, h, 0, 0, 0)),
                     pl.BlockSpec((None, None, ll, 2 * HEAD_DIM), lambda bb, h, i: (bb, h, 0, 0))]
        args += [ktl, vl]
    return pl.pallas_call(
        functools.partial(_attn_kernel, tq=tq, n_lat_chunks=n_chunks, has_lat=has_lat),
        grid=(b, KV_HEADS, lq // tq),
        in_specs=in_specs,
        out_specs=pl.BlockSpec((None, tq, gw), lambda bb, h, i: (bb, i, h)),
        out_shape=jax.ShapeDtypeStruct((b, lq, qw), BF16),
        compiler_params=_cparams(("arbitrary", "arbitrary", "arbitrary")),
        name="gqa_attention",
    )(*args)


def _dnprep_kernel(qp, qc, qn, kp, kc, kn, vp, vc, vn, ab_ref, cw_ref, alog_ref, dtb_ref, gm_ref, trip_ref, tris_ref,
                   qo_ref, ko_ref, kto_ref, vo_ref, gc_ref, gct_ref, *, tm):
    t = pl.program_id(1)
    nt = pl.num_programs(1)
    mp = (t > 0).astype(F32)
    mn = (t < nt - 1).astype(F32)
    n_ext = tm + 2 * SUBLANES
    pad = DN_CONV // 2

    def conv(p_ref, c_ref, n_ref, w):
        ext = jnp.concatenate([p_ref[...] * mp, c_ref[...], n_ref[...] * mn], axis=0)
        acc = None
        for j in range(DN_CONV):
            s = (pad - j) % n_ext
            r = ext if s == 0 else pltpu.roll(ext, s, 0)
            term = r[SUBLANES:SUBLANES + tm] * w[j:j + 1, :]
            acc = term if acc is None else acc + term
        return acc * jax.nn.sigmoid(acc)

    cw = cw_ref[...]
    hw = DN_HEADS * DN_DK
    gm = gm_ref[...]
    q = conv(qp, qc, qn, cw[:, 0:hw])
    q = q * lax.rsqrt(_dot_exact_rhs(q * q, gm) + EPS) * (DN_DK ** -0.5)
    k = conv(kp, kc, kn, cw[:, hw:2 * hw])
    k = k * lax.rsqrt(_dot_exact_rhs(k * k, gm) + EPS)
    v = conv(vp, vc, vn, cw[:, 2 * hw:3 * hw])
    kt = k.T
    for h in range(DN_HEADS):
        sl = slice(h * DN_DK, (h + 1) * DN_DK)
        qo_ref[h] = q[:, sl]
        ko_ref[h] = k[:, sl]
        vo_ref[h] = v[:, sl]
        for c in range(tm // DN_CHUNK):
            kto_ref[h, c] = kt[sl, c * DN_CHUNK:(c + 1) * DN_CHUNK]

    ab = ab_ref[...]
    lane = lax.broadcasted_iota(I32, ab.shape, 1)
    a = ab + dtb_ref[...]
    sp = jnp.maximum(a, 0.0) + jnp.log(1.0 + jnp.exp(-jnp.abs(a)))
    g = -jnp.exp(alog_ref[...]) * sp
    g = jnp.where(lane < 2 * DN_HEADS, g, 0.0)
    gpre = _dot_exact_lhs(trip_ref[...], g)
    gsuf = _dot_exact_lhs(tris_ref[...], g)
    gc = jnp.where(lane < DN_HEADS, gpre, jnp.where(lane < 2 * DN_HEADS, gsuf, jax.nn.sigmoid(ab)))
    gc_ref[...] = gc
    gt = gc.T
    for c in range(tm // DN_CHUNK):
        gct_ref[c] = gt[:4 * DN_HEADS, c * DN_CHUNK:(c + 1) * DN_CHUNK]


def _dnprep(proj, conv_w, a_log, dt_bias):
    b, l, _ = proj.shape
    tm = min(DN_PREP_TILE, l)
    hw = DN_HEADS * DN_DK
    nc = l // DN_CHUNK
    r8 = tm // SUBLANES
    last8 = l // SUBLANES - 1

    def cur(col):
        return pl.BlockSpec((None, tm, hw), lambda bb, t: (bb, t, col // hw))

    def prev(col):
        return pl.BlockSpec((None, SUBLANES, hw), lambda bb, t: (bb, jnp.maximum(t * r8 - 1, 0), col // hw))

    def nxt(col):
        return pl.BlockSpec((None, SUBLANES, hw), lambda bb, t: (bb, jnp.minimum((t + 1) * r8, last8), col // hw))

    const = lambda shape: pl.BlockSpec(shape, lambda bb, t: (0,) * len(shape))
    alog = jnp.zeros((1, LANES), F32).at[0, :2 * DN_HEADS].set(a_log.astype(F32).reshape(-1))
    dtb = jnp.zeros((1, LANES), F32).at[0, :2 * DN_HEADS].set(dt_bias.astype(F32).reshape(-1))
    ch = jnp.arange(tm) // DN_CHUNK
    same = ch[:, None] == ch[None, :]
    idx = jnp.arange(tm)
    trip = (same & (idx[:, None] >= idx[None, :])).astype(BF16)
    tris = (same & (idx[:, None] <= idx[None, :])).astype(BF16)
    gmat = _group_matrix(hw, 1.0)
    in_specs = []
    args = []
    for col in (COL_DNQ, COL_DNK, COL_DNV):
        in_specs += [prev(col), cur(col), nxt(col)]
        args += [proj, proj, proj]
    in_specs += [pl.BlockSpec((None, tm, LANES), lambda bb, t: (bb, t, COL_AB // LANES)),
                 const((DN_CONV, 3 * hw)), const((1, LANES)), const((1, LANES)), const((hw, hw)),
                 const((tm, tm)), const((tm, tm))]
    args += [proj, conv_w.astype(F32), alog, dtb, gmat, trip, tris]
    head_spec = pl.BlockSpec((None, DN_HEADS, tm, DN_DK), lambda bb, t: (bb, 0, t, 0))
    head_shape = jax.ShapeDtypeStruct((b, DN_HEADS, l, DN_DK), F32)
    return pl.pallas_call(
        functools.partial(_dnprep_kernel, tm=tm),
        grid=(b, l // tm),
        in_specs=in_specs,
        out_specs=[head_spec, head_spec,
                   pl.BlockSpec((None, DN_HEADS, tm // DN_CHUNK, DN_DK, DN_CHUNK), lambda bb, t: (bb, 0, t, 0, 0)),
                   head_spec,
                   pl.BlockSpec((None, tm, LANES), lambda bb, t: (bb, t, 0)),
                   pl.BlockSpec((None, tm // DN_CHUNK, 4 * DN_HEADS, DN_CHUNK), lambda bb, t: (bb, t, 0, 0))],
        out_shape=[head_shape, head_shape,
                   jax.ShapeDtypeStruct((b, DN_HEADS, nc, DN_DK, DN_CHUNK), F32),
                   head_shape,
                   jax.ShapeDtypeStruct((b, l, LANES), F32),
                   jax.ShapeDtypeStruct((b, nc, 4 * DN_HEADS, DN_CHUNK), F32)],
        compiler_params=_cparams(("arbitrary", "arbitrary")),
        name="dn_prep",
    )(*args)


def _bmm(a, b):
    return jnp.einsum('nij,njk->nik', a.astype(BF16), b.astype(BF16), preferred_element_type=F32)


def _unit_tri_inverse(a, eye, blks):
    inner = jnp.where(blks[0], a, 0.0)
    x = eye - inner
    p = inner
    for _ in range(2):
        p = _bmm(p, p)
        x = x + _bmm(x, p)
    for level in range(1, len(blks) + 1):
        outer = jnp.where(blks[level], a, 0.0) if level < len(blks) else a
        x = x - _bmm(x, _bmm(outer - inner, x))
        inner = outer
    return x


def _dn_local(q, k, kt, v, gcol, grow, beta, g_last, incl, strict, eye, blks):
    decay = jnp.where(incl, jnp.exp(jnp.where(incl, gcol - grow, 0.0)), 0.0)
    kb = k * beta
    a = jnp.where(strict, _bmm(kb, kt) * decay, 0.0)
    x = _unit_tri_inverse(a, eye, blks)
    sol = _bmm(x, jnp.concatenate([v * beta, kb * jnp.exp(gcol)], axis=2))
    u = sol[:, :, :DN_DK]
    w = sol[:, :, DN_DK:]
    qk = _bmm(q, kt) * decay
    return u, w, qk, q * jnp.exp(gcol), kt * jnp.exp(g_last - grow), jnp.exp(g_last)


def _dnscan_kernel(qf, kf, ktf, vf, gcf, gctf, qb, kb_, ktb, vb, gcb, gctb, s0_ref, of_ref, ob_ref, sout_ref, s_scr, *, cb):
    i = pl.program_id(1)
    last = pl.num_programs(1) - 1
    nh = DN_HEADS

    @pl.when(i == 0)
    def _():
        s_scr[...] = s0_ref[...]

    row = lax.broadcasted_iota(I32, (DN_CHUNK, DN_CHUNK), 0)
    col = lax.broadcasted_iota(I32, (DN_CHUNK, DN_CHUNK), 1)
    eye = (row == col).astype(F32)
    blks = tuple((row // n) == (col // n) for n in (8, 16, 32))
    dirs = ((0, qf, kf, ktf, vf, gcf, gctf, row >= col, row > col),
            (1, qb, kb_, ktb, vb, gcb, gctb, row <= col, row < col))
    local = []
    for d, q_ref, k_ref, kt_ref, v_ref, gc_ref, gct_ref, incl, strict in dirs:
        last_row = DN_CHUNK - 1 if d == 0 else 0
        qs, ks, kts, vs, gcols, grows, betas, glasts = [], [], [], [], [], [], [], []
        for c in range(cb):
            rs = slice(c * DN_CHUNK, (c + 1) * DN_CHUNK)
            gc = gc_ref[rs, :]
            gct = gct_ref[c]
            for h in range(nh):
                gl = d * nh + h
                bl = 2 * nh + d * nh + h
                qs.append(q_ref[h, rs, :])
                ks.append(k_ref[h, rs, :])
                kts.append(kt_ref[h, c])
                vs.append(v_ref[h, rs, :])
                gcols.append(gc[:, gl:gl + 1])
                grows.append(gct[gl:gl + 1, :])
                betas.append(gc[:, bl:bl + 1])
                glasts.append(gc[last_row:last_row + 1, gl:gl + 1])
        st = lambda xs: jnp.stack(xs, axis=0)
        local.append(_dn_local(st(qs), st(ks), st(kts), st(vs), st(gcols), st(grows), st(betas), st(glasts),
                               incl, strict, eye, blks))

    s_all = jnp.concatenate([s_scr[0], s_scr[1]], axis=0)
    for jj in range(cb):
        sel = (slice(jj * nh, (jj + 1) * nh), slice((cb - 1 - jj) * nh, (cb - jj) * nh))
        u, w, qk, qg, kdt, el = (jnp.concatenate([local[0][t][sel[0]], local[1][t][sel[1]]], axis=0) for t in range(6))
        from_state = _bmm(jnp.concatenate([w, qg], axis=1), s_all)
        v_new = u - from_state[:, :DN_CHUNK]
        from_v = _bmm(jnp.concatenate([kdt, qk], axis=1), v_new)
        o = from_state[:, DN_CHUNK:] + from_v[:, DN_CHUNK:]
        s_all = s_all * el + from_v[:, :DN_CHUNK]
        for d, o_ref in ((0, of_ref), (1, ob_ref)):
            c = jj if d == 0 else cb - 1 - jj
            for h in range(nh):
                o_ref[h, c * DN_CHUNK:(c + 1) * DN_CHUNK, :] = o[d * nh + h]
    s_scr[0] = s_all[:nh]
    s_scr[1] = s_all[nh:]

    @pl.when(i == last)
    def _():
        sout_ref[...] = s_scr[...]


def _dnscan(q, k, kt, v, gc, gct, s0):
    b, nh, l, dk = q.shape
    cb = min(DN_BLOCK_CHUNKS, l // DN_CHUNK)
    tb = cb * DN_CHUNK
    nblk = l // tb
    fwd = lambda i: i
    bwd = lambda i: nblk - 1 - i

    def specs(ix):
        head = pl.BlockSpec((None, nh, tb, dk), lambda bb, i: (bb, 0, ix(i), 0))
        return [head, head,
                pl.BlockSpec((None, nh, cb, dk, DN_CHUNK), lambda bb, i: (bb, 0, ix(i), 0, 0)),
                head,
                pl.BlockSpec((None, tb, LANES), lambda bb, i: (bb, ix(i), 0)),
                pl.BlockSpec((None, cb, 4 * DN_HEADS, DN_CHUNK), lambda bb, i: (bb, ix(i), 0, 0))]

    state_spec = pl.BlockSpec((None, 2, nh, dk, dk), lambda bb, i: (bb, 0, 0, 0, 0))
    o_shape = jax.ShapeDtypeStruct((b, nh, l, dk), F32)
    return pl.pallas_call(
        functools.partial(_dnscan_kernel, cb=cb),
        grid=(b, nblk),
        in_specs=specs(fwd) + specs(bwd) + [state_spec],
        out_specs=[pl.BlockSpec((None, nh, tb, dk), lambda bb, i: (bb, 0, fwd(i), 0)),
                   pl.BlockSpec((None, nh, tb, dk), lambda bb, i: (bb, 0, bwd(i), 0)),
                   state_spec],
        out_shape=[o_shape, o_shape, jax.ShapeDtypeStruct((b, 2, nh, dk, dk), F32)],
        scratch_shapes=[pltpu.VMEM((2, nh, dk, dk), F32)],
        compiler_params=_cparams(("arbitrary", "arbitrary")),
        name="dn_scan",
    )(q, k, kt, v, gc, gct, q, k, kt, v, gc, gct, s0)


def _mix0_out_kernel(x_ref, g1_ref, a_ref, of_ref, ob_ref, z_ref, nw_ref, w_ref, o_ref):
    o = of_ref[...] + ob_ref[...]
    on = o * lax.rsqrt(jnp.mean(o * o, axis=-1, keepdims=True) + EPS) * nw_ref[...]
    ot = jnp.concatenate([on[h] for h in range(DN_HEADS)], axis=1)
    z = z_ref[...]
    bmix = ot * (z * jax.nn.sigmoid(z))
    mix = jnp.concatenate([a_ref[...], bmix.astype(BF16)], axis=1)
    o_ref[...] = x_ref[...] + g1_ref[0] * _dot(mix, w_ref[...])


def _mix0_out(x, mod3, layer, row_fn, attn, o_f, o_b, proj, norm_w, w_out_bf16):
    b, l, d = x.shape
    tm = min(TOKEN_TILE, l)
    aw = Q_HEADS * HEAD_DIM
    zw = DN_HEADS * DN_DK
    head = pl.BlockSpec((None, DN_HEADS, tm, DN_DK), lambda bb, t: (bb, 0, t, 0))
    return pl.pallas_call(
        _mix0_out_kernel,
        grid=(b, l // tm),
        in_specs=[pl.BlockSpec((None, tm, d), lambda bb, t: (bb, t, 0)),
                  pl.BlockSpec((1, 1, d), _mod_spec(layer, row_fn, MOD_G1)),
                  pl.BlockSpec((None, tm, aw), lambda bb, t: (bb, t, 0)),
                  head, head,
                  pl.BlockSpec((None, tm, zw), lambda bb, t: (bb, t, COL_Z // zw)),
                  pl.BlockSpec((1, DN_DK), lambda bb, t: (0, 0)),
                  pl.BlockSpec((aw + zw, d), lambda bb, t: (0, 0))],
        out_specs=pl.BlockSpec((None, tm, d), lambda bb, t: (bb, t, 0)),
        out_shape=jax.ShapeDtypeStruct((b, l, d), F32),
        compiler_params=_cparams(("arbitrary", "arbitrary")),
        name="mix0_out",
    )(x, mod3, attn, o_f, o_b, proj, norm_w.astype(F32).reshape(1, DN_DK), w_out_bf16)


def _s5_kernel(*refs, nb, tm, pitch):
    x_ref = refs[0]
    sh_refs = refs[1:1 + nb]
    sc_refs = refs[1 + nb:1 + 2 * nb]
    wb_ref, wc_ref, lam_ref, x0_ref, y_ref, xfin_ref, bre, bim, st = refs[1 + 2 * nb:]
    d = pl.program_id(0)
    i = pl.program_id(1)
    nsub = SUBLANES
    nj = bre.shape[0] // nb
    half = nj * LANES
    for bb in range(nb):
        h = (_rms(x_ref[bb]) * (1.0 + sc_refs[bb][0]) + sh_refs[bb][0]).astype(BF16)
        for s in range(nsub):
            o = _dot(h[:, s * LANES:(s + 1) * LANES], wb_ref[s])
            for j in range(nj):
                bre[bb * nj + j, s * pitch:s * pitch + tm, :] = o[:, j * LANES:(j + 1) * LANES]
                bim[bb * nj + j, s * pitch:s * pitch + tm, :] = o[:, half + j * LANES:half + (j + 1) * LANES]

    @pl.when(i == 0)
    def _():
        st[...] = x0_ref[...]

    lam_re = lam_ref[0]
    lam_im = lam_ref[1]

    def body(k, carry):
        t = jnp.where(d == 0, k, tm - 1 - k)
        out = []
        for bb in range(nb):
            xr, xi = carry[2 * bb], carry[2 * bb + 1]
            br = jnp.concatenate([bre[bb * nj + j, pl.ds(t, nsub, stride=pitch), :] for j in range(nj)], axis=1)
            bi = jnp.concatenate([bim[bb * nj + j, pl.ds(t, nsub, stride=pitch), :] for j in range(nj)], axis=1)
            nr = lam_re * xr - lam_im * xi + br
            ni = lam_re * xi + lam_im * xr + bi
            for j in range(nj):
                bre[bb * nj + j, pl.ds(t, nsub, stride=pitch), :] = nr[:, j * LANES:(j + 1) * LANES]
                bim[bb * nj + j, pl.ds(t, nsub, stride=pitch), :] = ni[:, j * LANES:(j + 1) * LANES]
            out += [nr, ni]
        return tuple(out)

    init = tuple(st[bb, c] for bb in range(nb) for c in range(2))
    fin = lax.fori_loop(0, tm, body, init, unroll=8)
    for bb in range(nb):
        for c in range(2):
            st[bb, c] = fin[2 * bb + c]
            xfin_ref[bb, c] = fin[2 * bb + c]
    for bb in range(nb):
        for s in range(nsub):
            parts = [bre[bb * nj + j, s * pitch:s * pitch + tm, :] for j in range(nj)]
            parts += [bim[bb * nj + j, s * pitch:s * pitch + tm, :] for j in range(nj)]
            xs = jnp.concatenate(parts, axis=1).astype(BF16)
            y_ref[bb, :, s * LANES:(s + 1) * LANES] = _dot(xs, wc_ref[s])


def _s5_scan(x, mod3, layer, row_fn, wb, wc, lam, x0):
    b, l, d = x.shape
    tm = min(S5_TILE, l)
    nt = l // tm
    pitch = tm + S5_PITCH_PAD
    sw = wb.shape[-1] // 2
    nj = sw // LANES
    tile = lambda dd, i: jnp.where(dd == 0, i, nt - 1 - i)

    def mspec(which, bb):
        return pl.BlockSpec((1, 1, d), lambda dd, i: _mod_spec(layer, row_fn, which)(bb, i))

    state_spec = pl.BlockSpec((b, None, 2, SUBLANES, sw), lambda dd, i: (0, dd, 0, 0, 0))
    in_specs = [pl.BlockSpec((b, tm, d), lambda dd, i: (0, tile(dd, i), 0))]
    in_specs += [mspec(MOD_SH1, bb) for bb in range(b)] + [mspec(MOD_SC1, bb) for bb in range(b)]
    in_specs += [pl.BlockSpec((None, SUBLANES, LANES, 2 * sw), lambda dd, i: (dd, 0, 0, 0)),
                 pl.BlockSpec((None, SUBLANES, 2 * sw, LANES), lambda dd, i: (dd, 0, 0, 0)),
                 pl.BlockSpec((None, 2, SUBLANES, sw), lambda dd, i: (dd, 0, 0, 0)),
                 state_spec]
    return pl.pallas_call(
        functools.partial(_s5_kernel, nb=b, tm=tm, pitch=pitch),
        grid=(2, nt),
        in_specs=in_specs,
        out_specs=[pl.BlockSpec((None, b, tm, d), lambda dd, i: (dd, 0, tile(dd, i), 0)), state_spec],
        out_shape=[jax.ShapeDtypeStruct((2, b, l, d), F32),
                   jax.ShapeDtypeStruct((b, 2, 2, SUBLANES, sw), F32)],
        scratch_shapes=[pltpu.VMEM((b * nj, SUBLANES * pitch, LANES), F32),
                        pltpu.VMEM((b * nj, SUBLANES * pitch, LANES), F32),
                        pltpu.VMEM((b, 2, SUBLANES, sw), F32)],
        compiler_params=_cparams(("arbitrary", "arbitrary")),
        name="s5_scan",
    )(x, *([mod3] * (2 * b)), wb, wc, lam, x0)


def _s5_weights(lam_re, lam_im, log_step, b_re, b_im, c_re, c_im):
    lam = lax.complex(lam_re.astype(F32), lam_im.astype(F32))
    step = jnp.exp(log_step.astype(F32))[..., None]
    lam_bar = jnp.exp(lam * step)
    b_bar = ((lam_bar - 1.0) / lam)[..., None] * lax.complex(b_re.astype(F32), b_im.astype(F32))
    ng, p = lam.shape[1], lam.shape[2]
    gl = ng // SUBLANES
    eye = jnp.eye(gl, dtype=F32)

    def wb_of(part):
        t = part.reshape(2, SUBLANES, gl, p, S5_GROUP)
        w = jnp.einsum('dsgpi,gh->dsgihp', t, eye)
        return w.reshape(2, SUBLANES, gl * S5_GROUP, gl * p)

    def wc_of(part):
        t = part.reshape(2, SUBLANES, gl, S5_GROUP, p)
        w = jnp.einsum('dsgip,gh->dshpgi', t, eye)
        return w.reshape(2, SUBLANES, gl * p, gl * S5_GROUP)

    wb = jnp.concatenate([wb_of(jnp.real(b_bar)), wb_of(jnp.imag(b_bar))], axis=-1).astype(BF16)
    wc = jnp.concatenate([wc_of(c_re.astype(F32)), -wc_of(c_im.astype(F32))], axis=-2).astype(BF16)
    lam_t = jnp.stack([jnp.real(lam_bar).reshape(2, SUBLANES, gl * p),
                       jnp.imag(lam_bar).reshape(2, SUBLANES, gl * p)], axis=1)
    return wb, wc, lam_t


def _s5_glu_kernel(x_ref, sh_ref, sc_ref, g1_ref, yf_ref, yb_ref, dsk_ref, w1_ref, b1_ref, w2_ref, b2_ref, o_ref):
    x = x_ref[...]
    h = _rms(x) * (1.0 + sc_ref[0]) + sh_ref[0]
    y = dsk_ref[...] * h + yf_ref[...] + yb_ref[...]
    z = jax.nn.gelu(y).astype(BF16)
    o = (_dot(z, w1_ref[...]) + b1_ref[...]) * jax.nn.sigmoid(_dot(z, w2_ref[...]) + b2_ref[...])
    o_ref[...] = x + g1_ref[0] * o


def _s5_glu(x, mod3, layer, row_fn, y2, d_skip, w1, b1, w2, b2):
    b, l, d = x.shape
    tm = min(TOKEN_TILE, l)
    mspec = lambda which: pl.BlockSpec((1, 1, d), _mod_spec(layer, row_fn, which))
    const = lambda shape: pl.BlockSpec(shape, lambda bb, t: (0,) * len(shape))
    return pl.pallas_call(
        _s5_glu_kernel,
        grid=(b, l // tm),
        in_specs=[pl.BlockSpec((None, tm, d), lambda bb, t: (bb, t, 0)),
                  mspec(MOD_SH1), mspec(MOD_SC1), mspec(MOD_G1),
                  pl.BlockSpec((None, None, tm, d), lambda bb, t: (0, bb, t, 0)),
                  pl.BlockSpec((None, None, tm, d), lambda bb, t: (1, bb, t, 0)),
                  const((1, d)), const((d, d)), const((1, d)), const((d, d)), const((1, d))],
        out_specs=pl.BlockSpec((None, tm, d), lambda bb, t: (bb, t, 0)),
        out_shape=jax.ShapeDtypeStruct((b, l, d), F32),
        compiler_params=_cparams(("arbitrary", "arbitrary")),
        name="s5_glu",
    )(x, mod3, mod3, mod3, y2, y2, d_skip.astype(F32).reshape(1, d), w1.astype(BF16), b1.astype(F32).reshape(1, d),
      w2.astype(BF16), b2.astype(F32).reshape(1, d))


def _route_kernel(x_ref, sh_ref, sc_ref, rw_ref, rb_ref, cin_ref, tri_ref, hp_ref, meta_ref, cnt_ref, carry):
    first = (pl.program_id(0) == 0) & (pl.program_id(1) == 0)

    @pl.when(first)
    def _():
        carry[...] = cin_ref[...]

    h = _modulated(x_ref, sh_ref, sc_ref)
    half = h.shape[1] // 2
    lo = pltpu.bitcast(h[:, :half].astype(BF16).astype(F32), U32)
    hi = pltpu.bitcast(h[:, half:].astype(BF16).astype(F32), U32)
    packed = (lo >> 16) | (hi & jnp.uint32(0xFFFF0000))
    pieces = half // LANES
    for j in range(pieces):
        hp_ref[pl.ds(j, h.shape[0], stride=pieces), :] = packed[:, j * LANES:(j + 1) * LANES]

    logits = _dot_hi(h, rw_ref[...]) + rb_ref[...]
    lane = lax.broadcasted_iota(I32, logits.shape, 1)
    cur = jnp.where(lane < N_EXPERTS, logits, -jnp.inf)
    vals, idxs = [], []
    hot = jnp.zeros(logits.shape, F32)
    for _ in range(TOP_K):
        m = jnp.max(cur, axis=-1, keepdims=True)
        idx = jnp.min(jnp.where(cur == m, lane, LANES), axis=-1, keepdims=True)
        sel = lane == idx
        hot = hot + sel.astype(F32)
        cur = jnp.where(sel, -jnp.inf, cur)
        vals.append(m)
        idxs.append(idx)
    es = [jnp.exp(v - vals[0]) for v in vals]
    tot = es[0] + es[1] + es[2] + es[3]
    before = _dot(tri_ref[...], hot.astype(BF16)) + carry[...]
    meta = jnp.zeros(logits.shape, F32)
    for k in range(TOP_K):
        rank = jnp.sum(jnp.where(lane == idxs[k], before, 0.0), axis=-1, keepdims=True)
        meta = jnp.where(lane == k, idxs[k].astype(F32), meta)
        meta = jnp.where(lane == TOP_K + k, es[k] / tot, meta)
        meta = jnp.where(lane == 2 * TOP_K + k, rank, meta)
    meta_ref[...] = meta
    carry[...] = carry[...] + jnp.sum(hot, axis=0, keepdims=True)
    cnt_ref[...] = carry[...]


def _route(x, mod3, layer, row_fn, rw_pad, rb_pad, counts_in):
    b, l, d = x.shape
    tm = min(MOE_TOKEN_TILE, l)
    idx = jnp.arange(tm)
    tri = (idx[:, None] > idx[None, :]).astype(BF16)
    mspec = lambda which: pl.BlockSpec((1, 1, d), _mod_spec(layer, row_fn, which))
    const = lambda shape: pl.BlockSpec(shape, lambda bb, t: (0,) * len(shape))
    return pl.pallas_call(
        _route_kernel,
        grid=(b, l // tm),
        in_specs=[pl.BlockSpec((None, tm, d), lambda bb, t: (bb, t, 0)),
                  mspec(MOD_SH2), mspec(MOD_SC2),
                  const((d, LANES)), const((1, LANES)), const((1, LANES)), const((tm, tm))],
        out_specs=[pl.BlockSpec((None, tm * (d // 2 // LANES), LANES), lambda bb, t: (bb, t, 0)),
                   pl.BlockSpec((None, tm, LANES), lambda bb, t: (bb, t, 0)),
                   const((1, LANES))],
        out_shape=[jax.ShapeDtypeStruct((b, l * (d // 2 // LANES), LANES), U32),
                   jax.ShapeDtypeStruct((b, l, LANES), F32),
                   jax.ShapeDtypeStruct((1, LANES), F32)],
        scratch_shapes=[pltpu.VMEM((1, LANES), F32)],
        compiler_params=_cparams(("arbitrary", "arbitrary")),
        name="moe_route",
    )(x, mod3, mod3, rw_pad, rb_pad, counts_in, tri)


def _dest_kernel(meta_ref, ps_ref, o_ref):
    meta = meta_ref[...]
    lane = lax.broadcasted_iota(I32, meta.shape, 1)
    ps = ps_ref[...]
    dst = jnp.zeros(meta.shape, F32)
    for k in range(TOP_K):
        idx = meta[:, k:k + 1].astype(I32)
        base = jnp.sum(jnp.where(lane == idx, ps, 0.0), axis=-1, keepdims=True)
        dst = jnp.where(lane == k, base + meta[:, 2 * TOP_K + k:2 * TOP_K + k + 1], dst)
    dt = dst.T
    o_ref[0] = jnp.concatenate([dt[k:k + 1] for k in range(TOP_K)], axis=1).astype(I32)


def _dest(meta, pad_start_row):
    b, l, _ = meta.shape
    tt = min(MOE_TOKEN_TILE, l)
    nt = l // tt
    return pl.pallas_call(
        _dest_kernel,
        grid=(b, nt),
        in_specs=[pl.BlockSpec((None, tt, LANES), lambda bb, t: (bb, t, 0)),
                  pl.BlockSpec((1, LANES), lambda bb, t: (0, 0))],
        out_specs=pl.BlockSpec((1, 1, TOP_K * tt), lambda bb, t: (bb * nt + t, 0, 0)),
        out_shape=jax.ShapeDtypeStruct((b * nt, 1, TOP_K * tt), I32),
        compiler_params=_cparams(("arbitrary", "arbitrary")),
        name="moe_dest",
    )(meta, pad_start_row)


def _dispatch_kernel(*refs, tt, pieces):
    dest_hbm, hp_ref = refs[:2]
    xs_out, dest_smem, sem_i, sem = refs[-4:]
    i = pl.program_id(0)
    cp = pltpu.make_async_copy(dest_hbm.at[i, 0], dest_smem, sem_i)
    cp.start()
    cp.wait()

    def row_copy(r, k):
        src = pl.multiple_of(r * pieces, pieces)
        dst = pl.multiple_of(dest_smem[k * tt + r] * pieces, pieces)
        return pltpu.make_async_copy(hp_ref.at[pl.ds(src, pieces)], xs_out.at[pl.ds(dst, pieces)], sem)

    def issue(r, c):
        for k in range(TOP_K):
            row_copy(r, k).start(priority=k % 2)
        return c

    lax.fori_loop(0, tt, issue, 0, unroll=4)

    def drain(r, c):
        for k in range(TOP_K):
            row_copy(r, k).wait()
        return c

    lax.fori_loop(0, tt, drain, 0, unroll=4)


def _dispatch(dest, hp, xs, pieces, n_rows):
    n = hp.shape[0] // pieces
    tt = dest.shape[-1] // TOP_K
    in_specs = [pl.BlockSpec(memory_space=pl.ANY), pl.BlockSpec((tt * pieces, LANES), lambda i: (i, 0))]
    args = [dest, hp]
    aliases = {}
    if xs is not None:
        in_specs.append(pl.BlockSpec(memory_space=pl.ANY))
        args.append(xs)
        aliases = {2: 0}
    return pl.pallas_call(
        functools.partial(_dispatch_kernel, tt=tt, pieces=pieces),
        grid=(n // tt,),
        in_specs=in_specs,
        out_specs=pl.BlockSpec(memory_space=pl.ANY),
        out_shape=jax.ShapeDtypeStruct((n_rows * pieces, LANES), U32),
        scratch_shapes=[pltpu.SMEM((tt * TOP_K,), I32), pltpu.SemaphoreType.DMA, pltpu.SemaphoreType.DMA],
        input_output_aliases=aliases,
        compiler_params=_cparams(("arbitrary",)),
        name="moe_dispatch",
    )(*args)


def _expert_kernel(be_ref, nact_ref, nvalid_ref, xs_ref, wgu_ref, bgu_ref, wd_ref, bd_ref, y_ref, wgu_bf, wd_bf):
    i = pl.program_id(0)
    e = be_ref[i]
    prev = be_ref[jnp.maximum(i - 1, 0)]

    @pl.when((i == 0) | (e != prev))
    def _():
        wgu_bf[...] = wgu_ref[0].astype(BF16)
        wd_bf[...] = wd_ref[0].astype(BF16)

    rows = MOE_ROWS
    in_pieces = xs_ref.shape[0] // rows
    out_pieces = y_ref.shape[0] // rows

    @pl.when(i < nact_ref[0])
    def _():
        live = lax.broadcasted_iota(I32, (rows, LANES), 0) < nvalid_ref[i]
        ws = [jnp.where(live, xs_ref[pl.ds(j, rows, stride=in_pieces), :], jnp.uint32(0)) for j in range(in_pieces)]
        lo = [pltpu.bitcast(w << 16, F32) for w in ws]
        hi = [pltpu.bitcast(w & jnp.uint32(0xFFFF0000), F32) for w in ws]
        x = jnp.concatenate(lo + hi, axis=1).astype(BF16)
        gu = _dot(x, wgu_bf[...]) + bgu_ref[0]
        de = gu.shape[1] // 2
        gate = jnp.minimum(gu[:, :de], SWIGLU_LIMIT)
        up = jnp.clip(gu[:, de:], -SWIGLU_LIMIT, SWIGLU_LIMIT)
        act = (up + 1.0) * gate * jax.nn.sigmoid(SWIGLU_ALPHA * gate)
        y = _dot(act.astype(BF16), wd_bf[...]) + bd_ref[0]
        for j in range(out_pieces):
            y_ref[pl.ds(j, rows, stride=out_pieces), :] = y[:, j * LANES:(j + 1) * LANES]

    @pl.when(i >= nact_ref[0])
    def _():
        y_ref[...] = jnp.zeros(y_ref.shape, y_ref.dtype)


def _experts(block_e, n_active, n_valid, xs, w_gu, b_gu, w_d, b_d, layer):
    _, ne, d, de2 = w_gu.shape
    in_pieces = d // 2 // LANES
    out_pieces = d // LANES
    n_rows = xs.shape[0] // in_pieces
    nblk = n_rows // MOE_ROWS
    grid_spec = pltpu.PrefetchScalarGridSpec(
        num_scalar_prefetch=3,
        grid=(nblk,),
        in_specs=[pl.BlockSpec((MOE_ROWS * in_pieces, LANES), lambda i, be, na, nv: (i, 0)),
                  pl.BlockSpec((None, 1, d, de2), lambda i, be, na, nv: (layer, be[i], 0, 0)),
                  pl.BlockSpec((None, 1, 1, de2), lambda i, be, na, nv: (layer, be[i], 0, 0)),
                  pl.BlockSpec((None, 1, de2 // 2, d), lambda i, be, na, nv: (layer, be[i], 0, 0)),
                  pl.BlockSpec((None, 1, 1, d), lambda i, be, na, nv: (layer, be[i], 0, 0))],
        out_specs=pl.BlockSpec((MOE_ROWS * out_pieces, LANES), lambda i, be, na, nv: (i, 0)),
        scratch_shapes=[pltpu.VMEM((d, de2), BF16), pltpu.VMEM((de2 // 2, d), BF16)],
    )
    depth = w_gu.shape[0]
    return pl.pallas_call(
        _expert_kernel,
        grid_spec=grid_spec,
        out_shape=jax.ShapeDtypeStruct((n_rows * out_pieces, LANES), F32),
        compiler_params=_cparams(("arbitrary",)),
        name="moe_experts",
    )(block_e, n_active, n_valid, xs, w_gu, b_gu.reshape(depth, ne, 1, de2), w_d, b_d.reshape(depth, ne, 1, d))


def _combine_kernel(*refs, tt, final):
    if final:
        dest_hbm, yb_hbm, x_ref, meta_ref, g2_ref, fw_ref, o_ref, dest_smem, buf, sem_i, sem = refs
    else:
        dest_hbm, yb_hbm, x_ref, meta_ref, g2_ref, o_ref, dest_smem, buf, sem_i, sem = refs
    i = pl.program_id(0) * pl.num_programs(1) + pl.program_id(1)
    cp = pltpu.make_async_copy(dest_hbm.at[i, 0], dest_smem, sem_i)
    cp.start()
    cp.wait()

    pieces = buf.shape[1] // tt

    def row_copy(r, k):
        src = pl.multiple_of(dest_smem[k * tt + r] * pieces, pieces)
        dst = pl.multiple_of(r * pieces, pieces)
        return pltpu.make_async_copy(yb_hbm.at[pl.ds(src, pieces)], buf.at[k, pl.ds(dst, pieces)], sem)

    def issue(r, c):
        for k in range(TOP_K):
            row_copy(r, k).start(priority=k % 2)
        return c

    lax.fori_loop(0, tt, issue, 0, unroll=4)

    def drain(r, c):
        for k in range(TOP_K):
            row_copy(r, k).wait()
        return c

    lax.fori_loop(0, tt, drain, 0, unroll=4)
    meta = meta_ref[...]
    g2 = g2_ref[0]
    cols = []
    for j in range(pieces):
        y = meta[:, TOP_K:TOP_K + 1] * buf[0, pl.ds(j, tt, stride=pieces), :]
        for k in range(1, TOP_K):
            y = y + meta[:, TOP_K + k:TOP_K + k + 1] * buf[k, pl.ds(j, tt, stride=pieces), :]
        sl = slice(j * LANES, (j + 1) * LANES)
        cols.append(x_ref[:, sl] + g2[:, sl] * y)
    xn = jnp.concatenate(cols, axis=1)
    if final:
        xn = _rms(xn) * fw_ref[...]
    o_ref[...] = xn


def _combine(dest, yb, x, meta, mod3, layer, row_fn, final_w=None):
    b, l, d = x.shape
    tt = min(MOE_TOKEN_TILE, l)
    nt = l // tt
    final = final_w is not None
    in_specs = [pl.BlockSpec(memory_space=pl.ANY), pl.BlockSpec(memory_space=pl.ANY),
                pl.BlockSpec((None, tt, d), lambda bb, t: (bb, t, 0)),
                pl.BlockSpec((None, tt, LANES), lambda bb, t: (bb, t, 0)),
                pl.BlockSpec((1, 1, d), _mod_spec(layer, row_fn, MOD_G2))]
    args = [dest, yb, x, meta, mod3]
    if final:
        in_specs.append(pl.BlockSpec((1, d), lambda bb, t: (0, 0)))
        args.append(final_w.astype(F32).reshape(1, d))
    return pl.pallas_call(
        functools.partial(_combine_kernel, tt=tt, final=final),
        grid=(b, nt),
        in_specs=in_specs,
        out_specs=pl.BlockSpec((None, tt, d), lambda bb, t: (bb, t, 0)),
        out_shape=jax.ShapeDtypeStruct((b, l, d), F32),
        scratch_shapes=[pltpu.SMEM((tt * TOP_K,), I32), pltpu.VMEM((TOP_K, tt * (d // LANES), LANES), F32),
                        pltpu.SemaphoreType.DMA, pltpu.SemaphoreType.DMA],
        compiler_params=_cparams(("arbitrary", "arbitrary")),
        name="moe_combine",
    )(*args)


def _moe(parts, mod3, layer, router_w, router_b, w_gu, b_gu, w_d, b_d, final_w=None):
    d = router_w.shape[0]
    rw_pad = jnp.zeros((d, LANES), F32).at[:, :N_EXPERTS].set(router_w.astype(F32))
    rb_pad = jnp.zeros((1, LANES), F32).at[0, :N_EXPERTS].set(router_b.astype(F32))
    counts = jnp.zeros((1, LANES), F32)
    routed = []
    for x, row_fn in parts:
        hp, meta, counts = _route(x, mod3, layer, row_fn, rw_pad, rb_pad, counts)
        routed.append((hp, meta))
    n_assign = sum(x.shape[0] * x.shape[1] for x, _ in parts) * TOP_K
    nblk = -(-(n_assign + N_EXPERTS * (MOE_ROWS - 1)) // MOE_ROWS)
    cnt = counts[0, :N_EXPERTS].astype(I32)
    padded = (cnt + MOE_ROWS - 1) // MOE_ROWS * MOE_ROWS
    pad_end = jnp.cumsum(padded)
    pad_start = pad_end - padded
    blk_row = jnp.arange(nblk, dtype=I32)[:, None] * MOE_ROWS
    block_e = jnp.minimum(jnp.sum((pad_end[None, :] <= blk_row).astype(I32), axis=1), N_EXPERTS - 1).astype(I32)
    n_active = (pad_end[-1] // MOE_ROWS).astype(I32).reshape(1)
    ps_row = jnp.zeros((1, LANES), F32).at[0, :N_EXPERTS].set(pad_start.astype(F32))
    first_blk = pad_start // MOE_ROWS
    n_valid = jnp.clip(cnt[block_e] - (jnp.arange(nblk, dtype=I32) - first_blk[block_e]) * MOE_ROWS, 0, MOE_ROWS).astype(I32)
    pieces = d // 2 // LANES
    xs = None
    dests = []
    for (x, _), (hp, meta) in zip(parts, routed):
        dest = _dest(meta, ps_row)
        dests.append(dest)
        xs = _dispatch(dest, hp.reshape(-1, LANES), xs, pieces, nblk * MOE_ROWS)
    yb = _experts(block_e, n_active, n_valid, xs, w_gu, b_gu, w_d, b_d, layer)
    outs = []
    for idx, ((x, row_fn), (hp, meta)) in enumerate(zip(parts, routed)):
        fw = final_w if idx == 0 else None
        outs.append(_combine(dests[idx], yb, x, meta, mod3, layer, row_fn, fw))
    return outs


def kernel(x, c, ctx, c_ctx, ada_w, ada_b, ab_w_in, ab_q_gain, ab_k_gain, dn_conv_w, dn_a_log, dn_dt_bias, dn_norm_w, ab_w_out, s5_lambda_re, s5_lambda_im, s5_log_step, s5_b_re, s5_b_im, s5_c_re, s5_c_im, s5_d, s5_glu_w1, s5_glu_b1, s5_glu_w2, s5_glu_b2, moe_router_w, moe_router_b, moe_w_gate_up, moe_b_gate_up, moe_w_down, moe_b_down, final_norm_w):
    b, l, d = x.shape
    lc = ctx.shape[1]
    depth = ada_w.shape[0]
    assert depth == 2 and b < MOD_ROWS
    lat_row = lambda bb: bb
    ctx_row = lambda bb: b

    rows = jnp.zeros((MOD_ROWS, d), F32).at[:b].set(c.astype(F32)).at[b].set(c_ctx.astype(F32))
    mod3 = _ada(rows, ada_w, ada_b).reshape(depth * MOD_ROWS * 6, 1, d)

    w_in = ab_w_in[0]
    aq, akv, bqk = Q_HEADS * HEAD_DIM, KV_HEADS * HEAD_DIM, DN_HEADS * DN_DK
    o_k, o_v, o_dn, o_ab, o_z = aq, aq + akv, aq + 2 * akv, aq + 2 * akv + 3 * bqk, aq + 2 * akv + 3 * bqk + 4 * DN_HEADS
    w_perm = jnp.concatenate([w_in[:, :aq], w_in[:, o_dn:o_ab], w_in[:, o_z:], w_in[:, o_k:o_v], w_in[:, o_v:o_dn],
                              w_in[:, o_ab:o_z], jnp.zeros((d, IN_COLS_PAD - w_in.shape[1]), w_in.dtype)], axis=1).astype(BF16)
    proj_l = _inproj(x, mod3, 0, lat_row, w_perm)
    proj_c = _inproj(ctx, mod3, 0, ctx_row, w_perm)

    cos, sin = _rope_tables(l)
    ones = jnp.ones((lc, KV_HEADS * HEAD_DIM), F32)
    q_l, kt_l, v_l = _qkprep(proj_l, ab_q_gain[0], ab_k_gain[0], cos, sin, min(ATTN_K_TILE, l))
    q_c, kt_c, v_c = _qkprep(proj_c, ab_q_gain[0], ab_k_gain[0], ones, jnp.zeros_like(ones), lc)
    a_l = _attention(q_l, kt_c, v_c, kt_l, v_l)
    a_c = _attention(q_c, kt_c, v_c)

    dq_c, dk_c, dkt_c, dv_c, gc_c, gct_c = _dnprep(proj_c, dn_conv_w[0], dn_a_log[0], dn_dt_bias[0])
    dq_l, dk_l, dkt_l, dv_l, gc_l, gct_l = _dnprep(proj_l, dn_conv_w[0], dn_a_log[0], dn_dt_bias[0])
    s_zero = jnp.zeros((b, 2, DN_HEADS, DN_DK, DN_DK), F32)
    of_c, ob_c, s_ctx = _dnscan(dq_c, dk_c, dkt_c, dv_c, gc_c, gct_c, s_zero)
    of_l, ob_l, _ = _dnscan(dq_l, dk_l, dkt_l, dv_l, gc_l, gct_l, s_ctx)

    w_out = ab_w_out[0].astype(BF16)
    x = _mix0_out(x, mod3, 0, lat_row, a_l, of_l, ob_l, proj_l, dn_norm_w[0], w_out)
    ctx = _mix0_out(ctx, mod3, 0, ctx_row, a_c, of_c, ob_c, proj_c, dn_norm_w[0], w_out)

    x, ctx = _moe([(x, lat_row), (ctx, ctx_row)], mod3, 0, moe_router_w[0], moe_router_b[0],
                  moe_w_gate_up, moe_b_gate_up, moe_w_down, moe_b_down)

    wb, wc, lam_t = _s5_weights(s5_lambda_re[0], s5_lambda_im[0], s5_log_step[0], s5_b_re[0], s5_b_im[0],
                                s5_c_re[0], s5_c_im[0])
    sw = lam_t.shape[-1]
    x0 = jnp.zeros((b, 2, 2, SUBLANES, sw), F32)
    _, x_ctx = _s5_scan(ctx, mod3, 1, ctx_row, wb, wc, lam_t, x0)
    y2, _ = _s5_scan(x, mod3, 1, lat_row, wb, wc, lam_t, x_ctx)
    x = _s5_glu(x, mod3, 1, lat_row, y2, s5_d[0], s5_glu_w1[0], s5_glu_b1[0], s5_glu_w2[0], s5_glu_b2[0])

    (x,) = _moe([(x, lat_row)], mod3, 1, moe_router_w[1], moe_router_b[1], moe_w_gate_up, moe_b_gate_up,
                moe_w_down, moe_b_down, final_w=final_norm_w)
    return x
```
